```python
import math
import jax, jax.numpy as jnp
from jax import lax
import numpy as np

D_MODEL = 4096
BATCH = 2
SEQ = 4096
DEPTH = 1
DEC_BATCH = 128
DEC_SEQ = 4
PAST_LEN = 2048
PAGE_SIZE = 128

NSA_HEADS = 16
NSA_GROUPS = 4
NSA_HPG = NSA_HEADS // NSA_GROUPS
HEAD_DIM = 128
CMP_BLOCK = 32
CMP_STRIDE = 16
CMP_HIDDEN = 256
SLC_BLOCK = 64
N_SELECT = 16
WINDOW = 512
Q_BLOCK = 128
MLSTM_HEADS = 8
MLSTM_DQK = 256
MLSTM_DV = 512
MLSTM_CHUNK = 64
N_GROUPS = 8
EXPERTS_PER_GROUP = 8
N_EXPERTS = N_GROUPS * EXPERTS_PER_GROUP
TOP_K_WITHIN = 2
D_EXPERT = 1024
MOE_BLOCK = 64

RMS_EPS = 1e-6
NEG_INF = -1e30
FORCE_BONUS = 1e4

_IN_SIZES = (
    NSA_HEADS * HEAD_DIM,
    3 * 2 * NSA_GROUPS * HEAD_DIM,
    3 * NSA_HEADS,
    MLSTM_HEADS * MLSTM_DQK,
    MLSTM_HEADS * MLSTM_DQK,
    MLSTM_HEADS * MLSTM_DV,
    2 * MLSTM_HEADS,
    MLSTM_HEADS * MLSTM_DV,
    2 * D_MODEL,
)
D_IN = sum(_IN_SIZES)

kernel_name = "nsa_mlstm_hier_moe_step"


def _split_points():
    return [int(v) for v in np.cumsum(_IN_SIZES)[:-1]]


def _rmsnorm(x, g):
    xf = x.astype(jnp.float32)
    y = xf * lax.rsqrt(jnp.mean(xf * xf, axis=-1, keepdims=True) + RMS_EPS)
    return (y * g.astype(jnp.float32)).astype(x.dtype)


def _masked_softmax(s, mask):
    s = jnp.where(mask, s, NEG_INF)
    p = jnp.where(mask, jnp.exp(s - jnp.max(s, axis=-1, keepdims=True)), 0.0)
    return p / jnp.maximum(jnp.sum(p, axis=-1, keepdims=True), 1.0)


def _alibi_slopes():
    h = np.arange(1, NSA_HEADS + 1, dtype=np.float32)
    return jnp.asarray(2.0 ** (-8.0 * h / NSA_HEADS), jnp.float32).reshape(NSA_GROUPS, NSA_HPG)


def _overlap_matrix(n_cmp, n_slc):
    cs = np.arange(n_cmp)[:, None] * CMP_STRIDE
    ss = np.arange(n_slc)[None, :] * SLC_BLOCK
    ov = np.clip(np.minimum(cs + CMP_BLOCK, ss + SLC_BLOCK) - np.maximum(cs, ss), 0, None)
    return jnp.asarray(ov.astype(np.float32) / np.float32(CMP_BLOCK))


def _compress(kv, cmp_pos, w1, b1, w2):
    B, Tk = kv.shape[:2]
    n_cmp = (Tk - CMP_BLOCK) // CMP_STRIDE + 1
    R = CMP_BLOCK // CMP_STRIDE
    n_chunk = n_cmp + R - 1
    chunks = kv[:, :n_chunk * CMP_STRIDE].reshape(B, n_chunk, CMP_STRIDE, 2, NSA_GROUPS, HEAD_DIM)
    w1r = w1.reshape(R, CMP_STRIDE, 2, HEAD_DIM, CMP_HIDDEN)
    parts = jnp.einsum('bnlcgd,rlcdf->rbncgf', chunks, w1r)
    pos_bias = b1 + jnp.einsum('lcd,lcdf->cf', cmp_pos, w1)
    h = parts[0][:, :n_cmp]
    for r in range(1, R):
        h = h + parts[r][:, r:r + n_cmp]
    h = jax.nn.gelu(h + pos_bias[:, None, :])
    return jnp.einsum('bncgf,cfd->bncgd', h, w2)


def _selected_attention(q, top_i, sel_ok, blocks, qpos, slopes):
    B, T = q.shape[:2]
    qb = math.gcd(T, Q_BLOCK)
    nq = T // qb
    k_sel = top_i.shape[-1]
    qpos_c = jnp.asarray(qpos.reshape(nq, qb), jnp.int32)
    g_idx = np.arange(NSA_GROUPS)[None, :, None]

    def one_seq(args):
        q_s, i_s, ok_s, blk_s = args
        kb = jnp.moveaxis(blk_s, 3, 0)

        def one_chunk(cargs):
            qc, ic, okc, pc = cargs
            kv = kb[g_idx, ic].reshape(qb, NSA_GROUPS, k_sel * SLC_BLOCK, 2, HEAD_DIM)
            kpos = (ic[..., None] * SLC_BLOCK + np.arange(SLC_BLOCK)).reshape(qb, NSA_GROUPS, -1)
            dist = pc[:, None, None] - kpos
            mask = jnp.repeat(okc, SLC_BLOCK, axis=-1) & (dist >= 0)
            s = jnp.einsum('tghd,tgsd->tghs', qc, kv[..., 0, :]).astype(jnp.float32)
            s = s - slopes[None, :, :, None] * dist[:, :, None, :].astype(jnp.float32)
            p = _masked_softmax(s, mask[:, :, None, :])
            return jnp.einsum('tghs,tgsd->tghd', p.astype(qc.dtype), kv[..., 1, :])

        out = lax.map(one_chunk, (q_s.reshape(nq, qb, NSA_GROUPS, NSA_HPG, HEAD_DIM),
                                  i_s.reshape(nq, qb, NSA_GROUPS, k_sel),
                                  ok_s.reshape(nq, qb, NSA_GROUPS, k_sel), qpos_c))
        return out.reshape(T, NSA_GROUPS, NSA_HPG, HEAD_DIM)

    return lax.map(one_seq, (q, top_i, sel_ok, blocks))


def _window_attention(q, kv_keys, kpos0, qpos, slopes):
    B, T = q.shape[:2]
    past_len = kv_keys.shape[1] - T
    qb = math.gcd(T, Q_BLOCK)
    nq = T // qb
    lk = past_len + qb
    kidx = np.arange(nq)[:, None] * qb + np.arange(lk)[None, :]
    kpos = kpos0 + kidx
    dist = qpos.reshape(nq, qb)[:, :, None] - kpos[:, None, :]
    mask = (kpos[:, None, :] >= 0) & (dist >= 0) & (dist < WINDOW)
    kvb = kv_keys[:, kidx]
    qc = q.reshape(B, nq, qb, NSA_GROUPS, NSA_HPG, HEAD_DIM)
    s = jnp.einsum('bcqghd,bckgd->bcghqk', qc, kvb[:, :, :, 0]).astype(jnp.float32)
    s = s - slopes[None, None, :, :, None, None] * jnp.asarray(dist[None, :, None, None].astype(np.float32))
    p = _masked_softmax(s, mask[None, :, None, None])
    o = jnp.einsum('bcghqk,bckgd->bcqghd', p.astype(q.dtype), kvb[:, :, :, 1])
    return o.reshape(B, T, NSA_GROUPS, NSA_HPG, HEAD_DIM)


def _nsa(q, kv_cmp_all, kv_slc_all, win_keys, win_kpos0, qpos, cmp_pos, w_cmp1, b_cmp1, w_cmp2):
    B = q.shape[0]
    Tk = kv_cmp_all.shape[1]
    slopes = _alibi_slopes()
    kvc = _compress(kv_cmp_all, cmp_pos, w_cmp1, b_cmp1, w_cmp2)
    n_cmp = kvc.shape[1]
    c_end = np.arange(n_cmp) * CMP_STRIDE + CMP_BLOCK - 1
    dist_c = (qpos[:, None] - c_end[None, :])
    s = jnp.einsum('btghd,bngd->bghtn', q, kvc[:, :, 0]).astype(jnp.float32)
    s = s - slopes[:, :, None, None] * jnp.asarray(dist_c.astype(np.float32))
    p_cmp = _masked_softmax(s, dist_c >= 0)
    o_cmp = jnp.einsum('bghtn,bngd->btghd', p_cmp.astype(q.dtype), kvc[:, :, 1])
    n_slc = -(-Tk // SLC_BLOCK)
    imp = jnp.einsum('bghtn,ni->btgi', p_cmp, _overlap_matrix(n_cmp, n_slc))
    blk = np.arange(n_slc)[None, :]
    cur = (qpos // SLC_BLOCK)[:, None]
    valid = (blk <= cur)[:, None, :]
    forced = ((blk == 0) | (blk == cur) | (blk == cur - 1))[:, None, :].astype(np.float32)
    imp = jnp.where(valid, imp + np.float32(FORCE_BONUS) * forced, NEG_INF)
    top_v, top_i = lax.top_k(imp, min(N_SELECT, n_slc))
    kv_pad = jnp.pad(kv_slc_all, ((0, 0), (0, n_slc * SLC_BLOCK - Tk), (0, 0), (0, 0), (0, 0)))
    blocks = kv_pad.reshape(B, n_slc, SLC_BLOCK, 2, NSA_GROUPS, HEAD_DIM)
    o_slc = _selected_attention(q, top_i, top_v > 0.5 * NEG_INF, blocks, qpos, slopes)
    o_win = _window_attention(q, win_keys, win_kpos0, qpos, slopes)
    return o_cmp, o_slc, o_win


def _mlstm(q, k, v, i_pre, log_f, C0, n0, m0):
    B, T, H, _ = q.shape
    L = math.gcd(T, MLSTM_CHUNK)
    nc = T // L
    f32 = jnp.float32

    def chunks(a):
        a = a.astype(f32).reshape(B, nc, L, H, *a.shape[3:])
        return jnp.moveaxis(a, (1, 3), (0, 2))

    causal = np.tril(np.ones((L, L), dtype=bool))

    def step(carry, inp):
        C, n, m = carry
        qc, kc, vc, ic, fc = inp
        b = jnp.cumsum(fc, axis=-1)
        dlog = jnp.where(causal, b[..., :, None] - b[..., None, :] + ic[..., None, :], NEG_INF)
        g = b + m[..., None]
        m_t = jnp.maximum(g, jnp.max(dlog, axis=-1))
        w = jnp.exp(dlog - m_t[..., None])
        gw = jnp.exp(g - m_t)
        qk = jnp.einsum('bhtd,bhsd->bhts', qc, kc) * w
        num = gw[..., None] * jnp.einsum('bhtd,bhde->bhte', qc, C) + jnp.einsum('bhts,bhse->bhte', qk, vc)
        den = gw * jnp.einsum('bhtd,bhd->bht', qc, n) + jnp.sum(qk, axis=-1)
        h = num / jnp.maximum(jnp.abs(den), jnp.exp(-m_t))[..., None]
        b_last = b[..., -1]
        ws = b_last[..., None] - b + ic
        m_new = jnp.maximum(b_last + m, jnp.max(ws, axis=-1))
        sw = jnp.exp(ws - m_new[..., None])
        cw = jnp.exp(b_last + m - m_new)
        C_new = cw[..., None, None] * C + jnp.einsum('bhs,bhsd,bhse->bhde', sw, kc, vc)
        n_new = cw[..., None] * n + jnp.einsum('bhs,bhsd->bhd', sw, kc)
        return (C_new, n_new, m_new), h

    xs = (chunks(q), chunks(k), chunks(v), chunks(i_pre), chunks(log_f))
    (C, n, m), h = lax.scan(step, (C0.astype(f32), n0.astype(f32), m0.astype(f32)), xs)
    h = jnp.moveaxis(h, (0, 2), (1, 3)).reshape(B, T, H, -1)
    return h, C, n, m


def _moe(xn, w_rg, b_rg, w_re, b_re, w_eg, w_eu, w_ed):
    B, T, D = xn.shape
    x = xn.reshape(-1, D)
    N = x.shape[0]
    g_prob = jax.nn.softmax((x @ w_rg).astype(jnp.float32) + b_rg, axis=-1)
    grp = jnp.argmax(g_prob, axis=-1)
    g_w = jnp.take_along_axis(g_prob, grp[:, None], axis=-1)[:, 0]
    e_logits = ((x @ w_re).astype(jnp.float32) + b_re).reshape(N, N_GROUPS, EXPERTS_PER_GROUP)
    in_grp = jnp.take_along_axis(e_logits, grp[:, None, None], axis=1)[:, 0]
    top_w, top_i = lax.top_k(jax.nn.softmax(in_grp, axis=-1), TOP_K_WITHIN)
    weights = g_w[:, None] * top_w / jnp.sum(top_w, axis=-1, keepdims=True)
    experts = grp[:, None] * EXPERTS_PER_GROUP + top_i
    A = N * TOP_K_WITHIN
    e_flat = experts.reshape(-1)
    tok = jnp.repeat(jnp.arange(N, dtype=jnp.int32), TOP_K_WITHIN)
    order = jnp.argsort(e_flat)
    e_sorted, tok_sorted, w_sorted = e_flat[order], tok[order], weights.reshape(-1)[order]
    counts = jnp.bincount(e_flat, length=N_EXPERTS)
    padded = (counts + MOE_BLOCK - 1) // MOE_BLOCK * MOE_BLOCK
    pad_end = jnp.cumsum(padded)
    pad_start = pad_end - padded
    cnt_start = jnp.cumsum(counts) - counts
    dest = pad_start[e_sorted] + jnp.arange(A) - cnt_start[e_sorted]
    n_blocks = -(-(A + N_EXPERTS * (MOE_BLOCK - 1)) // MOE_BLOCK)
    row_tok = jnp.full((n_blocks * MOE_BLOCK,), N, jnp.int32).at[dest].set(tok_sorted)
    blk_expert = jnp.minimum(jnp.searchsorted(pad_end, jnp.arange(n_blocks) * MOE_BLOCK, side='right'), N_EXPERTS - 1)
    x_pad = jnp.concatenate([x, jnp.zeros((1, D), x.dtype)], axis=0)

    def run(args):
        rows, e = args
        xb = x_pad[rows]
        h = jax.nn.silu(xb @ w_eg[e]) * (xb @ w_eu[e])
        return h @ w_ed[e]

    y_rows = lax.map(run, (row_tok.reshape(n_blocks, MOE_BLOCK), blk_expert)).reshape(-1, D)
    y_assign = y_rows[dest] * w_sorted[:, None].astype(x.dtype)
    y = jax.ops.segment_sum(y_assign, tok_sorted, num_segments=N)
    return y.reshape(B, T, D)


def _layer(x, pos0, cmp_past, slc_past, win_past, win_kpos0, C0, n0, m0, win_buf, p):
    (g_norm_mix, w_in, b_in, cmp_pos, w_cmp1, b_cmp1, w_cmp2, g_mlstm_norm, w_branch_a, w_branch_b,
     w_out, g_norm_ffn, w_rg, b_rg, w_re, b_re, w_eg, w_eu, w_ed) = p
    B, T, _ = x.shape
    qpos = pos0 + np.arange(T)
    xn = _rmsnorm(x, g_norm_mix)
    z = xn @ w_in + b_in
    q_a, kv_a, gate_a, q_b, k_b, v_b, if_b, o_b, merge = jnp.split(z, _split_points(), axis=-1)
    q_a = q_a.reshape(B, T, NSA_GROUPS, NSA_HPG, HEAD_DIM) * HEAD_DIM ** -0.5
    kv_a = kv_a.reshape(B, T, 3, 2, NSA_GROUPS, HEAD_DIM)
    kv_cmp, kv_slc, kv_win = kv_a[:, :, 0], kv_a[:, :, 1], kv_a[:, :, 2]
    win_keys = jnp.concatenate([win_past, kv_win], axis=1)
    o_cmp, o_slc, o_win = _nsa(q_a, jnp.concatenate([cmp_past, kv_cmp], axis=1),
                               jnp.concatenate([slc_past, kv_slc], axis=1),
                               win_keys, win_kpos0, qpos, cmp_pos, w_cmp1, b_cmp1, w_cmp2)
    ga = jax.nn.sigmoid(gate_a.reshape(B, T, 3, NSA_GROUPS, NSA_HPG))[..., None]
    h_a = (ga[:, :, 0] * o_cmp + ga[:, :, 1] * o_slc + ga[:, :, 2] * o_win).reshape(B, T, -1)
    q_b = q_b.reshape(B, T, MLSTM_HEADS, MLSTM_DQK)
    k_b = k_b.reshape(B, T, MLSTM_HEADS, MLSTM_DQK) * MLSTM_DQK ** -0.5
    v_b = v_b.reshape(B, T, MLSTM_HEADS, MLSTM_DV)
    if_b = if_b.reshape(B, T, 2, MLSTM_HEADS).astype(jnp.float32)
    h, C, n, m = _mlstm(q_b, k_b, v_b, if_b[:, :, 0], jax.nn.log_sigmoid(if_b[:, :, 1]), C0, n0, m0)
    h = h * lax.rsqrt(jnp.mean(h * h, axis=-1, keepdims=True) + RMS_EPS) \
        * g_mlstm_norm.reshape(MLSTM_HEADS, MLSTM_DV).astype(jnp.float32)
    h_b = (h.astype(x.dtype) * jax.nn.sigmoid(o_b.reshape(B, T, MLSTM_HEADS, MLSTM_DV))).reshape(B, T, -1)
    gm = jax.nn.sigmoid(merge.reshape(B, T, 2, D_MODEL))
    mix = gm[:, :, 0] * (h_a @ w_branch_a) + gm[:, :, 1] * (h_b @ w_branch_b)
    x = x + mix @ w_out
    x = x + _moe(_rmsnorm(x, g_norm_ffn), w_rg, b_rg, w_re, b_re, w_eg, w_eu, w_ed)
    return x, kv_cmp, kv_slc, win_keys[:, -win_buf:], C, n, m


def setup_inputs(seed: int = 0) -> dict:
    key = jax.random.key(seed)
    ks = iter(jax.random.split(key, 40))

    def nrm(shape, scale):
        return scale * jax.random.normal(next(ks), shape, jnp.float32)

    def gain(shape):
        return 1.0 + nrm(shape, 0.02)

    L = DEPTH
    n_pages = PAST_LEN // PAGE_SIZE
    n_pool = (DEC_BATCH * n_pages * 5) // 4
    win_buf = min(WINDOW, PAST_LEN)
    kvs = (2, NSA_GROUPS, HEAD_DIM)
    x_prompt = nrm((BATCH, SEQ, D_MODEL), 1.0)
    x_sample = nrm((DEC_BATCH, DEC_SEQ, D_MODEL), 1.0)
    cache_cmp_kv = nrm((L, n_pool, PAGE_SIZE) + kvs, 1.0)
    cache_slc_kv = nrm((L, n_pool, PAGE_SIZE) + kvs, 1.0)
    cache_win_kv = nrm((L, DEC_BATCH, win_buf) + kvs, 1.0)
    state_mlstm_C = nrm((L, DEC_BATCH, MLSTM_HEADS, MLSTM_DQK, MLSTM_DV), 0.1)
    state_mlstm_n = nrm((L, DEC_BATCH, MLSTM_HEADS, MLSTM_DQK), 0.1)
    state_mlstm_m = nrm((L, DEC_BATCH, MLSTM_HEADS), 1.0)
    page_table = jax.random.permutation(next(ks), n_pool)[:DEC_BATCH * n_pages] \
        .reshape(DEC_BATCH, n_pages).astype(jnp.int32)
    f_off = sum(_IN_SIZES[:6]) + MLSTM_HEADS
    b_in = nrm((L, D_IN), 0.02).at[:, f_off:f_off + MLSTM_HEADS].add(jnp.linspace(3.0, 6.0, MLSTM_HEADS))
    return {
        "x_prompt": x_prompt, "x_sample": x_sample,
        "cache_cmp_kv": cache_cmp_kv, "cache_slc_kv": cache_slc_kv, "cache_win_kv": cache_win_kv,
        "state_mlstm_C": state_mlstm_C, "state_mlstm_n": state_mlstm_n, "state_mlstm_m": state_mlstm_m,
        "page_table": page_table,
        "g_norm_mix": gain((L, D_MODEL)),
        "w_in": nrm((L, D_MODEL, D_IN), D_MODEL ** -0.5),
        "b_in": b_in,
        "cmp_pos": nrm((L, CMP_BLOCK, 2, HEAD_DIM), 0.02),
        "w_cmp1": nrm((L, CMP_BLOCK, 2, HEAD_DIM, CMP_HIDDEN), (CMP_BLOCK * HEAD_DIM) ** -0.5),
        "b_cmp1": nrm((L, 2, CMP_HIDDEN), 0.02),
        "w_cmp2": nrm((L, 2, CMP_HIDDEN, HEAD_DIM), CMP_HIDDEN ** -0.5),
        "g_mlstm_norm": gain((L, MLSTM_HEADS * MLSTM_DV)),
        "w_branch_a": nrm((L, NSA_HEADS * HEAD_DIM, D_MODEL), (NSA_HEADS * HEAD_DIM) ** -0.5),
        "w_branch_b": nrm((L, MLSTM_HEADS * MLSTM_DV, D_MODEL), (MLSTM_HEADS * MLSTM_DV) ** -0.5),
        "w_out": nrm((L, D_MODEL, D_MODEL), D_MODEL ** -0.5),
        "g_norm_ffn": gain((L, D_MODEL)),
        "w_router_group": nrm((L, D_MODEL, N_GROUPS), D_MODEL ** -0.5),
        "b_router_group": nrm((L, N_GROUPS), 0.01),
        "w_router_expert": nrm((L, D_MODEL, N_EXPERTS), D_MODEL ** -0.5),
        "b_router_expert": nrm((L, N_EXPERTS), 0.01),
        "w_exp_gate": nrm((L, N_EXPERTS, D_MODEL, D_EXPERT), D_MODEL ** -0.5),
        "w_exp_up": nrm((L, N_EXPERTS, D_MODEL, D_EXPERT), D_MODEL ** -0.5),
        "w_exp_down": nrm((L, N_EXPERTS, D_EXPERT, D_MODEL), D_EXPERT ** -0.5),
        "g_norm_final": gain((D_MODEL,)),
    }


def reference(x_prompt, x_sample, cache_cmp_kv, cache_slc_kv, cache_win_kv, state_mlstm_C, state_mlstm_n,
              state_mlstm_m, page_table, g_norm_mix, w_in, b_in, cmp_pos, w_cmp1, b_cmp1, w_cmp2, g_mlstm_norm,
              w_branch_a, w_branch_b, w_out, g_norm_ffn, w_router_group, b_router_group, w_router_expert,
              b_router_expert, w_exp_gate, w_exp_up, w_exp_down, g_norm_final):
    B = x_prompt.shape[0]
    DB = x_sample.shape[0]
    n_pages = page_table.shape[1]
    past = n_pages * PAGE_SIZE
    win_buf = cache_win_kv.shape[2]
    kvs = (2, NSA_GROUPS, HEAD_DIM)
    xp, xs = x_prompt, x_sample
    pc_l, sc_l, ps_l, ss_l, pw_l, sw_l, pC_l, sC_l, pn_l, sn_l, pm_l, sm_l = ([] for _ in range(12))
    for layer in range(DEPTH):
        p = (g_norm_mix[layer], w_in[layer], b_in[layer], cmp_pos[layer], w_cmp1[layer], b_cmp1[layer],
             w_cmp2[layer], g_mlstm_norm[layer], w_branch_a[layer], w_branch_b[layer], w_out[layer],
             g_norm_ffn[layer], w_router_group[layer], b_router_group[layer], w_router_expert[layer],
             b_router_expert[layer], w_exp_gate[layer], w_exp_up[layer], w_exp_down[layer])
        empty = jnp.zeros((B, 0) + kvs, xp.dtype)
        win0 = jnp.zeros((B, WINDOW) + kvs, xp.dtype)
        C0 = jnp.zeros((B, MLSTM_HEADS, MLSTM_DQK, MLSTM_DV), jnp.float32)
        n0 = jnp.zeros((B, MLSTM_HEADS, MLSTM_DQK), jnp.float32)
        m0 = jnp.zeros((B, MLSTM_HEADS), jnp.float32)
        xp, pc, ps, pw, pC, pn, pm = _layer(xp, 0, empty, empty, win0, -WINDOW, C0, n0, m0, win_buf, p)
        cmp_past = cache_cmp_kv[layer][page_table].reshape((DB, past) + kvs)
        slc_past = cache_slc_kv[layer][page_table].reshape((DB, past) + kvs)
        xs, sc, ss, sw, sC, sn, sm = _layer(xs, past, cmp_past, slc_past, cache_win_kv[layer], past - win_buf,
                                            state_mlstm_C[layer], state_mlstm_n[layer], state_mlstm_m[layer],
                                            win_buf, p)
        for lst, val in zip((pc_l, sc_l, ps_l, ss_l, pw_l, sw_l, pC_l, sC_l, pn_l, sn_l, pm_l, sm_l),
                            (pc, sc, ps, ss, pw, sw, pC, sC, pn, sn, pm, sm)):
            lst.append(val)
    y_prompt = _rmsnorm(xp, g_norm_final)
    y_sample = _rmsnorm(xs, g_norm_final)
    return (y_prompt, y_sample,
            jnp.stack(pc_l), jnp.stack(sc_l), jnp.stack(ps_l), jnp.stack(ss_l), jnp.stack(pw_l), jnp.stack(sw_l),
            jnp.stack(pC_l), jnp.stack(sC_l), jnp.stack(pn_l), jnp.stack(sn_l), jnp.stack(pm_l), jnp.stack(sm_l))
```

```python
import functools
import math

import numpy as np
import jax
import jax.numpy as jnp
from jax import lax
from jax.experimental import pallas as pl
from jax.experimental.pallas import tpu as pltpu

f32 = jnp.float32
bf16 = jnp.bfloat16
i32 = jnp.int32

HEAD_DIM = 128
PAGE_SIZE = 128
CMP_BLOCK = 32
CMP_STRIDE = 16
SLC_BLOCK = 64
N_SELECT = 16
WINDOW = 512
Q_BLOCK = 128
TOP_K_WITHIN = 2
RMS_EPS = 1e-6
NEG_INF = -1e30
FORCE_BONUS = 1e4

ROW_ALIGN = 256
MOE_ROWS = 128
MLSTM_CHUNK = 256
SAMPLE_PAD = 16
SAMPLE_TOK_PAD = 8
MIB = 1024 * 1024


def _cp(sem, vmem_mib=None):
    kw = dict(dimension_semantics=sem)
    if vmem_mib is not None:
        kw["vmem_limit_bytes"] = int(vmem_mib * MIB)
    return pltpu.CompilerParams(**kw)


def _pick(n, cands):
    for c in cands:
        if n % c == 0:
            return c
    return n


def _sigmoid(x):
    return 1.0 / (1.0 + jnp.exp(-x))


def _log_sigmoid(x):
    return jnp.minimum(x, 0.0) - jnp.log(1.0 + jnp.exp(-jnp.abs(x)))


def _dot(a, b):
    return jnp.dot(a, b, preferred_element_type=f32)


def _dot_nt(a, b):
    return lax.dot_general(a, b, (((1,), (1,)), ((), ())), preferred_element_type=f32)


def _dot_tn(a, b):
    return lax.dot_general(a, b, (((0,), (0,)), ((), ())), preferred_element_type=f32)


def _dot_hilo(a, b_bf16):
    hi = a.astype(bf16)
    lo = (a - hi.astype(f32)).astype(bf16)
    return _dot(hi, b_bf16) + _dot(lo, b_bf16)


def _masked_softmax(s, mask):
    s = jnp.where(mask, s, NEG_INF)
    p = jnp.where(mask, jnp.exp(s - jnp.max(s, axis=-1, keepdims=True)), 0.0)
    return p / jnp.maximum(jnp.sum(p, axis=-1, keepdims=True), 1.0)


def _rmsnorm_body(x_ref, g_ref, o_ref):
    x = x_ref[...]
    y = x * lax.rsqrt(jnp.mean(x * x, axis=-1, keepdims=True) + RMS_EPS)
    o_ref[...] = (y * g_ref[...]).astype(o_ref.dtype)


def _rmsnorm(x, g, out_dtype):
    M, D = x.shape
    tm = _pick(M, (256, 128, 8))
    return pl.pallas_call(
        _rmsnorm_body, grid=(M // tm,),
        in_specs=[pl.BlockSpec((tm, D), lambda i: (i, 0)), pl.BlockSpec((1, D), lambda i: (0, 0))],
        out_specs=pl.BlockSpec((tm, D), lambda i: (i, 0)),
        out_shape=jax.ShapeDtypeStruct((M, D), out_dtype),
        compiler_params=_cp(("parallel",), 40), name="rmsnorm")(x, g.reshape(1, D).astype(f32))


def _mm_bias_body(x_ref, w_ref, b_ref, o_ref):
    o_ref[...] = (_dot(x_ref[...], w_ref[...]) + b_ref[...]).astype(o_ref.dtype)


def _mm_bias(x, w, b, out_dtype, name):
    M, K = x.shape
    N = w.shape[1]
    tm = _pick(M, (1088, 1024, 768, 512, 256))
    tn = _pick(N, (512, 256, 128))
    return pl.pallas_call(
        _mm_bias_body, grid=(M // tm, N // tn),
        in_specs=[pl.BlockSpec((tm, K), lambda i, j: (i, 0)),
                  pl.BlockSpec((K, tn), lambda i, j: (0, j)),
                  pl.BlockSpec((1, tn), lambda i, j: (0, j))],
        out_specs=pl.BlockSpec((tm, tn), lambda i, j: (i, j)),
        out_shape=jax.ShapeDtypeStruct((M, N), out_dtype),
        compiler_params=_cp(("parallel", "parallel"), 48), name=name)(x, w, b.reshape(1, N).astype(f32))


def _merge_body(ha_ref, wa_ref, hb_ref, wb_ref, ga_ref, gb_ref, o_ref):
    a = _dot(ha_ref[...], wa_ref[...])
    b = _dot(hb_ref[...], wb_ref[...])
    o_ref[...] = (_sigmoid(ga_ref[...].astype(f32)) * a + _sigmoid(gb_ref[...].astype(f32)) * b).astype(o_ref.dtype)


def _merge(h_a, w_a, h_b, w_b, zc, gate_col0):
    M, Ka = h_a.shape
    Kb = h_b.shape[1]
    D = w_a.shape[1]
    tm = _pick(M, (544, 512, 256))
    tn = _pick(D, (512, 256, 128))
    c0 = gate_col0 // tn
    nd = D // tn
    return pl.pallas_call(
        _merge_body, grid=(M // tm, nd),
        in_specs=[pl.BlockSpec((tm, Ka), lambda i, j: (i, 0)),
                  pl.BlockSpec((Ka, tn), lambda i, j: (0, j)),
                  pl.BlockSpec((tm, Kb), lambda i, j: (i, 0)),
                  pl.BlockSpec((Kb, tn), lambda i, j: (0, j)),
                  pl.BlockSpec((tm, tn), lambda i, j: (i, c0 + j)),
                  pl.BlockSpec((tm, tn), lambda i, j: (i, c0 + nd + j))],
        out_specs=pl.BlockSpec((tm, tn), lambda i, j: (i, j)),
        out_shape=jax.ShapeDtypeStruct((M, D), bf16),
        compiler_params=_cp(("parallel", "parallel"), 48), name="merge")(h_a, w_a, h_b, w_b, zc, zc)


def _mm_res_body(x_ref, w_ref, r_ref, o_ref):
    o_ref[...] = r_ref[...] + _dot(x_ref[...], w_ref[...])


def _mm_residual(x, w, res):
    M, K = x.shape
    N = w.shape[1]
    tm = _pick(M, (1088, 1024, 768, 512, 256))
    tn = _pick(N, (512, 256, 128))
    return pl.pallas_call(
        _mm_res_body, grid=(M // tm, N // tn),
        in_specs=[pl.BlockSpec((tm, K), lambda i, j: (i, 0)),
                  pl.BlockSpec((K, tn), lambda i, j: (0, j)),
                  pl.BlockSpec((tm, tn), lambda i, j: (i, j))],
        out_specs=pl.BlockSpec((tm, tn), lambda i, j: (i, j)),
        out_shape=jax.ShapeDtypeStruct((M, N), f32),
        compiler_params=_cp(("parallel", "parallel"), 48), name="out_proj")(x, w, res)


def _compress_body(x_ref, w1_ref, pb_ref, w2_ref, o_ref, xcat_ref, *, nch):
    for l in range(CMP_STRIDE):
        xcat_ref[:, l * HEAD_DIM:(l + 1) * HEAD_DIM] = x_ref[pl.ds(l, nch, stride=CMP_STRIDE), :].astype(bf16)
    parts = _dot(xcat_ref[...], w1_ref[0])
    F = parts.shape[1] // 2
    h = parts[:, :F] + pltpu.roll(parts[:, F:], nch - 1, 0) + pb_ref[0]
    h = jax.nn.gelu(h, approximate=True)
    o_ref[0, 0, 0] = _dot(h.astype(bf16), w2_ref[0])


def _compress_prompt(zA, w1c, pb, w2, B, T, G, qblocks):
    nch = T // CMP_STRIDE
    F2 = w1c.shape[2]
    return pl.pallas_call(
        functools.partial(_compress_body, nch=nch), grid=(B, 2, G),
        in_specs=[pl.BlockSpec((T, HEAD_DIM), lambda b, c, g: (b, qblocks + c * G + g)),
                  pl.BlockSpec((1, CMP_STRIDE * HEAD_DIM, F2), lambda b, c, g: (c, 0, 0)),
                  pl.BlockSpec((1, 1, F2 // 2), lambda b, c, g: (c, 0, 0)),
                  pl.BlockSpec((1, F2 // 2, HEAD_DIM), lambda b, c, g: (c, 0, 0))],
        out_specs=pl.BlockSpec((1, 1, 1, nch, HEAD_DIM), lambda b, c, g: (b, c, g, 0, 0)),
        out_shape=jax.ShapeDtypeStruct((B, 2, G, nch, HEAD_DIM), f32),
        scratch_shapes=[pltpu.VMEM((nch, CMP_STRIDE * HEAD_DIM), bf16)],
        compiler_params=_cp(("parallel", "parallel", "parallel"), 32), name="compress_prompt")(zA, w1c, pb, w2)


def _select_blocks(imp, cur, nblk, n_sel):
    R, W = imp.shape
    blk = lax.broadcasted_iota(i32, (R, W), 1)
    valid = blk <= cur
    forced = (blk == 0) | (blk == cur) | (blk == cur - 1)
    v = jnp.where(valid, imp + jnp.where(forced, FORCE_BONUS, 0.0), NEG_INF)
    cnt = jnp.zeros((R, W), f32)
    for i in range(nblk):
        ci = v[:, i:i + 1]
        tie = jnp.where(blk > i, 1.0, 0.0)
        cnt = cnt + jnp.where(ci > v, 1.0, jnp.where(ci == v, tie, 0.0))
    return jnp.where(valid, jnp.where(cnt < n_sel, 1.0, 0.0), 0.0)


def _expand_sel(sel_bf16, kt):
    W = sel_bf16.shape[1]
    jrow = lax.broadcasted_iota(i32, (W, 2 * SLC_BLOCK), 0)
    scol = lax.broadcasted_iota(i32, (W, 2 * SLC_BLOCK), 1)
    e = jnp.where(jrow == 2 * kt + (scol >> 6), 1.0, 0.0).astype(bf16)
    return _dot(sel_bf16, e)


def _nsa_prompt_body(q_ref, kc_ref, vc_ref, ks_ref, vs_ref, kw_ref, vw_ref, gate_ref, slope_ref, ov_ref, o_ref,
                     m_ref, l_ref, acc_ref, *, T, hg, ncp, nslc, n_sel, window, wk):
    qb = pl.program_id(2)
    HQ = hg * Q_BLOCK
    q = q_ref[...] * (HEAD_DIM ** -0.5)
    qs = jnp.concatenate([q[:, h * HEAD_DIM:(h + 1) * HEAD_DIM] for h in range(hg)], axis=0).astype(bf16)
    row = lax.broadcasted_iota(i32, (HQ, 1), 0)
    qpos = qb * Q_BLOCK + (row & (Q_BLOCK - 1))
    slope = slope_ref[0][:, 0:1]

    kc = kc_ref[0, 0, 0].astype(bf16)
    vc = vc_ref[0, 0, 0].astype(bf16)
    s = _dot_nt(qs, kc)
    c_end = lax.broadcasted_iota(i32, (1, ncp), 1) * CMP_STRIDE + (CMP_BLOCK - 1)
    dist = qpos - c_end
    p = _masked_softmax(s - slope * dist.astype(f32), dist >= 0)
    o_cmp = _dot(p.astype(bf16), vc)
    psum = p[0:Q_BLOCK]
    for h in range(1, hg):
        psum = psum + p[h * Q_BLOCK:(h + 1) * Q_BLOCK]
    imp = _dot_hilo(psum, ov_ref[...])
    cur = (qb * Q_BLOCK + lax.broadcasted_iota(i32, (Q_BLOCK, 1), 0)) >> 6
    sel = _select_blocks(imp, cur, nslc, n_sel).astype(bf16)

    m_ref[...] = jnp.full(m_ref.shape, NEG_INF, f32)
    l_ref[...] = jnp.zeros(l_ref.shape, f32)
    acc_ref[...] = jnp.zeros(acc_ref.shape, f32)
    lane = lax.broadcasted_iota(i32, (1, 2 * SLC_BLOCK), 1)

    def tile(kt, carry):
        k0 = pl.multiple_of(kt * (2 * SLC_BLOCK), 2 * SLC_BLOCK)
        k = ks_ref[pl.ds(k0, 2 * SLC_BLOCK), :].astype(bf16)
        v = vs_ref[pl.ds(k0, 2 * SLC_BLOCK), :].astype(bf16)
        d = qpos - (k0 + lane)
        mk = _expand_sel(sel, kt)
        mk = jnp.concatenate([mk] * hg, axis=0)
        mask = jnp.where(d >= 0, mk, 0.0) > 0.5
        sc = jnp.where(mask, _dot_nt(qs, k) - slope * d.astype(f32), NEG_INF)
        m_old = m_ref[...]
        m_new = jnp.maximum(m_old, jnp.max(sc, axis=-1, keepdims=True))
        alpha = jnp.exp(m_old - m_new)
        pt = jnp.where(mask, jnp.exp(sc - m_new), 0.0)
        l_ref[...] = alpha * l_ref[...] + jnp.sum(pt, axis=-1, keepdims=True)
        acc_ref[...] = alpha * acc_ref[...] + _dot(pt.astype(bf16), v)
        m_ref[...] = m_new
        return carry

    lax.fori_loop(0, qb + 1, tile, 0)
    o_slc = acc_ref[...] / jnp.maximum(l_ref[...], 1.0)

    start = pl.multiple_of(jnp.clip(qb * Q_BLOCK - window, 0, T - wk), Q_BLOCK)
    kw = kw_ref[pl.ds(start, wk), :].astype(bf16)
    vw = vw_ref[pl.ds(start, wk), :].astype(bf16)
    dw = qpos - (start + lax.broadcasted_iota(i32, (1, wk), 1))
    maskw = jnp.where(dw >= 0, jnp.where(dw < window, 1.0, 0.0), 0.0) > 0.5
    pw = _masked_softmax(_dot_nt(qs, kw) - slope * dw.astype(f32), maskw)
    o_win = _dot(pw.astype(bf16), vw)

    ga = _sigmoid(gate_ref[0])
    outs = []
    for h in range(hg):
        sl = slice(h * Q_BLOCK, (h + 1) * Q_BLOCK)
        outs.append(ga[:, h:h + 1] * o_cmp[sl] + ga[:, hg + h:hg + h + 1] * o_slc[sl]
                    + ga[:, 2 * hg + h:2 * hg + h + 1] * o_win[sl])
    o_ref[...] = jnp.concatenate(outs, axis=1).astype(o_ref.dtype)


def _nsa_prompt(zA, kvc, gates, slopes, ov, B, T, G, hg, qblocks):
    NQ = T // Q_BLOCK
    ncp = kvc.shape[3]
    nslc = -(-T // SLC_BLOCK)
    wk = min(WINDOW + Q_BLOCK, T)
    HQ = hg * Q_BLOCK

    def kvspec(br, c):
        return pl.BlockSpec((T, HEAD_DIM), lambda b, g, i: (b, qblocks + (br * 2 + c) * G + g))

    body = functools.partial(_nsa_prompt_body, T=T, hg=hg, ncp=ncp, nslc=nslc, n_sel=min(N_SELECT, nslc),
                             window=WINDOW, wk=wk)
    return pl.pallas_call(
        body, grid=(B, G, NQ),
        in_specs=[pl.BlockSpec((Q_BLOCK, hg * HEAD_DIM), lambda b, g, i: (b * NQ + i, g)),
                  pl.BlockSpec((1, 1, 1, ncp, HEAD_DIM), lambda b, g, i: (b, 0, g, 0, 0)),
                  pl.BlockSpec((1, 1, 1, ncp, HEAD_DIM), lambda b, g, i: (b, 1, g, 0, 0)),
                  kvspec(1, 0), kvspec(1, 1), kvspec(2, 0), kvspec(2, 1),
                  pl.BlockSpec((1, Q_BLOCK, 128), lambda b, g, i: (g, b * NQ + i, 0)),
                  pl.BlockSpec((1, HQ, 128), lambda b, g, i: (g, 0, 0)),
                  pl.BlockSpec((ncp, 128), lambda b, g, i: (0, 0))],
        out_specs=pl.BlockSpec((Q_BLOCK, hg * HEAD_DIM), lambda b, g, i: (b * NQ + i, g)),
        out_shape=jax.ShapeDtypeStruct((B * T, G * hg * HEAD_DIM), bf16),
        scratch_shapes=[pltpu.VMEM((HQ, 1), f32), pltpu.VMEM((HQ, 1), f32), pltpu.VMEM((HQ, HEAD_DIM), f32)],
        compiler_params=_cp(("parallel", "parallel", "arbitrary"), 48), name="nsa_prompt")(
            zA, kvc, kvc, zA, zA, zA, zA, gates, slopes, ov)


def _nsa_sample_cmp_body(pt_ref, *refs, NP, G, hg, past, nslc, n_sel):
    page_refs = refs[:2 * G]
    w1_ref, pb_ref, w2_ref, q_ref, slope_ref, ov_ref, o_ref, sel_ref, xcat_ref = refs[2 * G:]
    p = pl.program_id(1)
    cpp = PAGE_SIZE // CMP_STRIDE
    nch = NP * cpp
    tp = SAMPLE_TOK_PAD
    base = pl.multiple_of(p * cpp, cpp)
    for c in range(2):
        for g in range(G):
            for l in range(CMP_STRIDE):
                xcat_ref[c, pl.ds(g * nch + base, cpp), l * HEAD_DIM:(l + 1) * HEAD_DIM] = (
                    page_refs[c * G + g][pl.ds(l, cpp, stride=CMP_STRIDE), :])

    @pl.when(p == NP - 1)
    def _():
        kv = []
        for c in range(2):
            parts = _dot(xcat_ref[c].astype(bf16), w1_ref[c])
            F = parts.shape[1] // 2
            h = parts[:, :F] + pltpu.roll(parts[:, F:], G * nch - 1, 0) + pb_ref[c]
            h = jax.nn.gelu(h, approximate=True)
            kv.append(_dot(h.astype(bf16), w2_ref[c]))
        row = lax.broadcasted_iota(i32, (hg * tp, 1), 0)
        qpos = past + (row & (tp - 1))
        c_end = lax.broadcasted_iota(i32, (1, nch), 1) * CMP_STRIDE + (CMP_BLOCK - 1)
        dist = qpos - c_end
        cur = (past + lax.broadcasted_iota(i32, (tp, 1), 0)) >> 6
        for g in range(G):
            kc = kv[0][g * nch:(g + 1) * nch].astype(bf16)
            vc = kv[1][g * nch:(g + 1) * nch].astype(bf16)
            qg = (q_ref[0, g] * (HEAD_DIM ** -0.5)).astype(bf16)
            slope = slope_ref[g][:, 0:1]
            pr = _masked_softmax(_dot_nt(qg, kc) - slope * dist.astype(f32), dist >= 0)
            o_ref[0, g] = _dot(pr.astype(bf16), vc)
            psum = pr[0:tp]
            for h in range(1, hg):
                psum = psum + pr[h * tp:(h + 1) * tp]
            imp = _dot_hilo(psum, ov_ref[...])
            sel_ref[0, g] = _select_blocks(imp, cur, nslc, n_sel)


def _nsa_sample_cmp(pt, cache, w1c, pb, w2, q_s, slopes_s, ov_s, DB, NP, G, hg, past, nslc):
    tp = SAMPLE_TOK_PAD
    nch = NP * (PAGE_SIZE // CMP_STRIDE)
    W = cache.shape[2]
    body = functools.partial(_nsa_sample_cmp_body, NP=NP, G=G, hg=hg, past=past, nslc=nslc,
                             n_sel=min(N_SELECT, nslc))
    gs = pltpu.PrefetchScalarGridSpec(
        num_scalar_prefetch=1, grid=(DB, NP),
        in_specs=[pl.BlockSpec((None, PAGE_SIZE, HEAD_DIM), lambda b, p, pt, cg=cg: (pt[b * NP + p], 0, cg))
                  for cg in range(2 * G)] + [
                  pl.BlockSpec(w1c.shape, lambda b, p, pt: (0, 0, 0)),
                  pl.BlockSpec(pb.shape, lambda b, p, pt: (0, 0, 0)),
                  pl.BlockSpec(w2.shape, lambda b, p, pt: (0, 0, 0)),
                  pl.BlockSpec((1, G, hg * tp, HEAD_DIM), lambda b, p, pt: (b, 0, 0, 0)),
                  pl.BlockSpec(slopes_s.shape, lambda b, p, pt: (0, 0, 0)),
                  pl.BlockSpec(ov_s.shape, lambda b, p, pt: (0, 0))],
        out_specs=[pl.BlockSpec((1, G, hg * tp, HEAD_DIM), lambda b, p, pt: (b, 0, 0, 0)),
                   pl.BlockSpec((1, G, tp, 128), lambda b, p, pt: (b, 0, 0, 0))],
        scratch_shapes=[pltpu.VMEM((2, G * nch, CMP_STRIDE * HEAD_DIM), f32)])
    return pl.pallas_call(
        body, grid_spec=gs,
        out_shape=[jax.ShapeDtypeStruct((DB, G, hg * tp, HEAD_DIM), f32),
                   jax.ShapeDtypeStruct((DB, G, tp, 128), f32)],
        compiler_params=_cp(("parallel", "arbitrary"), 48), name="nsa_sample_cmp")(
            pt, *([cache] * (2 * G)), w1c, pb, w2, q_s, slopes_s, ov_s)


def _nsa_sample_sw_body(pt_ref, page_ref, q_ref, sel_ref, kvn_ref, win_ref, gate_ref, slope_ref, ocmp_ref, o_ref,
                        m_ref, l_ref, acc_ref, *, NP, G, hg, past, ts, window):
    p = pl.program_id(1)
    tp = SAMPLE_TOK_PAD
    R = hg * tp
    row = lax.broadcasted_iota(i32, (R, 1), 0)
    qpos = past + (row & (tp - 1))
    scale = HEAD_DIM ** -0.5

    @pl.when(p == 0)
    def _():
        m_ref[...] = jnp.full(m_ref.shape, NEG_INF, f32)
        l_ref[...] = jnp.zeros(l_ref.shape, f32)
        acc_ref[...] = jnp.zeros(acc_ref.shape, f32)

    def update(g, sc, mask, pv):
        m_old = m_ref[g]
        m_new = jnp.maximum(m_old, jnp.max(sc, axis=-1, keepdims=True))
        alpha = jnp.exp(m_old - m_new)
        pt = jnp.where(mask, jnp.exp(sc - m_new), 0.0)
        l_ref[g] = alpha * l_ref[g] + jnp.sum(pt, axis=-1, keepdims=True)
        acc_ref[g] = alpha * acc_ref[g] + pv(pt)
        m_ref[g] = m_new

    @pl.when(p < NP)
    def _():
        lane = lax.broadcasted_iota(i32, (1, PAGE_SIZE), 1)
        d = qpos - (p * PAGE_SIZE + lane)
        for g in range(G):
            k = page_ref[0, :, g * HEAD_DIM:(g + 1) * HEAD_DIM].astype(bf16)
            v = page_ref[0, :, (G + g) * HEAD_DIM:(G + g + 1) * HEAD_DIM].astype(bf16)
            qg = (q_ref[0, g] * scale).astype(bf16)
            slope = slope_ref[g][:, 0:1]
            sel = jnp.concatenate([sel_ref[0, g]] * hg, axis=0).astype(bf16)
            mask = jnp.where(d >= 0, _expand_sel(sel, p), 0.0) > 0.5
            sc = jnp.where(mask, _dot_nt(qg, k) - slope * d.astype(f32), NEG_INF)
            update(g, sc, mask, lambda pt: _dot(pt.astype(bf16), v))

    @pl.when(p == NP)
    def _():
        kvn = kvn_ref[0]
        nb = past // SLC_BLOCK
        for g in range(G):
            qf = q_ref[0, g] * scale
            qg = qf.astype(bf16)
            slope = slope_ref[g][:, 0:1]

            def new_cols(br, c):
                o = ((br * 2 + c) * G + g) * HEAD_DIM
                return kvn[:, o:o + HEAD_DIM]

            sel_nb = jnp.concatenate([sel_ref[0, g]] * hg, axis=0)[:, nb:nb + 1]
            kn, vn = new_cols(1, 0), new_cols(1, 1)
            for j in range(ts):
                dj = qpos - (past + j)
                mj = jnp.where(dj >= 0, sel_nb, 0.0) > 0.5
                sj = jnp.sum(qf * kn[j:j + 1, :], axis=-1, keepdims=True) - slope * dj.astype(f32)
                sj = jnp.where(mj, sj, NEG_INF)
                update(g, sj, mj, lambda pt, j=j: pt * vn[j:j + 1, :])
            o_slc = acc_ref[g] / jnp.maximum(l_ref[g], 1.0)

            wb = win_ref.shape[1]
            kw = win_ref[0, :, g * HEAD_DIM:(g + 1) * HEAD_DIM].astype(bf16)
            vw = win_ref[0, :, (G + g) * HEAD_DIM:(G + g + 1) * HEAD_DIM].astype(bf16)
            dw = qpos - (past - wb + lax.broadcasted_iota(i32, (1, wb), 1))
            maskw = jnp.where(dw >= 0, jnp.where(dw < window, 1.0, 0.0), 0.0) > 0.5
            sw = jnp.where(maskw, _dot_nt(qg, kw) - slope * dw.astype(f32), NEG_INF)
            kn, vn = new_cols(2, 0), new_cols(2, 1)
            mx = jnp.max(sw, axis=-1, keepdims=True)
            snew = []
            for j in range(ts):
                dj = qpos - (past + j)
                mj = jnp.where(dj >= 0, jnp.where(dj < window, 1.0, 0.0), 0.0) > 0.5
                sj = jnp.sum(qf * kn[j:j + 1, :], axis=-1, keepdims=True) - slope * dj.astype(f32)
                sj = jnp.where(mj, sj, NEG_INF)
                snew.append((sj, mj))
                mx = jnp.maximum(mx, sj)
            pw = jnp.where(maskw, jnp.exp(sw - mx), 0.0)
            den = jnp.sum(pw, axis=-1, keepdims=True)
            o_win = _dot(pw.astype(bf16), vw)
            for j, (sj, mj) in enumerate(snew):
                pj = jnp.where(mj, jnp.exp(sj - mx), 0.0)
                den = den + pj
                o_win = o_win + pj * vn[j:j + 1, :]
            o_win = o_win / jnp.maximum(den, 1.0)

            ga = _sigmoid(gate_ref[0, g])
            o_ref[0, g] = ga[:, 0:1] * ocmp_ref[0, g] + ga[:, 1:2] * o_slc + ga[:, 2:3] * o_win


def _nsa_sample_sw(pt, cache, q_s, sel, kvn, win, gates_s, slopes_s, o_cmp, DB, NP, G, hg, past, ts):
    tp = SAMPLE_TOK_PAD
    R = hg * tp
    W = cache.shape[2]
    body = functools.partial(_nsa_sample_sw_body, NP=NP, G=G, hg=hg, past=past, ts=ts, window=WINDOW)
    blk4 = lambda b, p, pt: (b, 0, 0, 0)
    gs = pltpu.PrefetchScalarGridSpec(
        num_scalar_prefetch=1, grid=(DB, NP + 1),
        in_specs=[pl.BlockSpec((1, PAGE_SIZE, W), lambda b, p, pt: (pt[b * NP + jnp.minimum(p, NP - 1)], 0, 0)),
                  pl.BlockSpec((1, G, R, HEAD_DIM), blk4),
                  pl.BlockSpec((1, G, tp, 128), blk4),
                  pl.BlockSpec((1, tp, kvn.shape[2]), lambda b, p, pt: (b, 0, 0)),
                  pl.BlockSpec((1, win.shape[1], win.shape[2]), lambda b, p, pt: (b, 0, 0)),
                  pl.BlockSpec((1, G, R, 128), blk4),
                  pl.BlockSpec(slopes_s.shape, lambda b, p, pt: (0, 0, 0)),
                  pl.BlockSpec((1, G, R, HEAD_DIM), blk4)],
        out_specs=pl.BlockSpec((1, G, R, HEAD_DIM), blk4),
        scratch_shapes=[pltpu.VMEM((G, R, 1), f32), pltpu.VMEM((G, R, 1), f32), pltpu.VMEM((G, R, HEAD_DIM), f32)])
    return pl.pallas_call(
        body, grid_spec=gs, out_shape=jax.ShapeDtypeStruct((DB, G, R, HEAD_DIM), f32),
        compiler_params=_cp(("parallel", "arbitrary"), 32), name="nsa_sample_sw")(
            pt, cache, q_s, sel, kvn, win, gates_s, slopes_s, o_cmp)


def _mlstm_body(q_ref, k_ref, v_ref, ob_ref, gc_ref, gr_ref, c0_ref, n0_ref, m0_ref, gn_ref,
                h_ref, c_ref, n_ref, mo_ref, cs_ref, ns_ref, ms_ref, *, L, dqk):
    c = pl.program_id(2)
    nc = pl.num_programs(2)

    @pl.when(c == 0)
    def _():
        cs_ref[...] = c0_ref[0, 0]
        ns_ref[...] = n0_ref[0, 0]
        ms_ref[...] = m0_ref[0, 0]

    q = q_ref[...]
    k = k_ref[...] * (dqk ** -0.5)
    v = v_ref[...]
    gc = gc_ref[0, 0]
    gr = gr_ref[0, 0]
    i_col, f_col = gc[:, 0:1], _log_sigmoid(gc[:, 1:2])
    i_row, f_row = gr[0:1, :], _log_sigmoid(gr[1:2, :])
    r = lax.broadcasted_iota(i32, (L, L), 0)
    s = lax.broadcasted_iota(i32, (L, L), 1)
    causal = r >= s
    b_col = jnp.sum(jnp.where(causal, f_row, 0.0), axis=1, keepdims=True)
    b_row = jnp.sum(jnp.where(r <= s, f_col, 0.0), axis=0, keepdims=True)
    m_prev = ms_ref[...]
    g = b_col + m_prev
    dlog = jnp.where(causal, b_col - b_row + i_row, NEG_INF)
    m_t = jnp.maximum(g, jnp.max(dlog, axis=1, keepdims=True))
    w = jnp.exp(dlog - m_t)
    gw = jnp.exp(g - m_t)
    qb_, kb_, vb_ = q.astype(bf16), k.astype(bf16), v.astype(bf16)
    qk = _dot_nt(qb_, kb_) * w
    num = gw * _dot(qb_, cs_ref[...].astype(bf16)) + _dot(qk.astype(bf16), vb_)
    den = gw * jnp.sum(q * ns_ref[...], axis=1, keepdims=True) + jnp.sum(qk, axis=1, keepdims=True)
    hh = num / jnp.maximum(jnp.abs(den), jnp.exp(-m_t))
    hn = hh * lax.rsqrt(jnp.mean(hh * hh, axis=1, keepdims=True) + RMS_EPS) * gn_ref[0]
    h_ref[...] = (hn * _sigmoid(ob_ref[...].astype(f32))).astype(h_ref.dtype)

    b_last = b_col[L - 1:L, :]
    ws = b_last - b_col + i_col
    m_new = jnp.maximum(b_last + m_prev, jnp.max(ws, axis=0, keepdims=True))
    sw = jnp.exp(ws - m_new)
    cw = jnp.exp(b_last + m_prev - m_new)
    ksw = k * sw
    cs_ref[...] = cw * cs_ref[...] + _dot_tn(ksw.astype(bf16), vb_)
    ns_ref[...] = cw * ns_ref[...] + jnp.sum(ksw, axis=0, keepdims=True)
    ms_ref[...] = m_new

    @pl.when(c == nc - 1)
    def _():
        c_ref[0, 0] = cs_ref[...]
        n_ref[0, 0] = ns_ref[...]
        mo_ref[0, 0] = ms_ref[...]


def _mlstm(z, q_blk0, k_blk0, v_blk0, ob, ob_blk0, gcol, grow, c0, n0, m0, gnorm, NS, NC, L, H, dqk, dv):
    R = NS * NC * L
    st = lambda b, h, c: (b, h, 0, 0)
    return pl.pallas_call(
        functools.partial(_mlstm_body, L=L, dqk=dqk), grid=(NS, H, NC),
        in_specs=[pl.BlockSpec((L, dqk), lambda b, h, c: (b * NC + c, q_blk0 + h)),
                  pl.BlockSpec((L, dqk), lambda b, h, c: (b * NC + c, k_blk0 + h)),
                  pl.BlockSpec((L, dv), lambda b, h, c: (b * NC + c, v_blk0 + h)),
                  pl.BlockSpec((L, dv), lambda b, h, c: (b * NC + c, ob_blk0 + h)),
                  pl.BlockSpec((1, 1, L, 2), lambda b, h, c: (h, b * NC + c, 0, 0)),
                  pl.BlockSpec((1, 1, 2, L), lambda b, h, c: (h, b * NC + c, 0, 0)),
                  pl.BlockSpec((1, 1, dqk, dv), st),
                  pl.BlockSpec((1, 1, 1, dqk), st),
                  pl.BlockSpec((1, 1, 1, 1), st),
                  pl.BlockSpec((1, 1, dv), lambda b, h, c: (h, 0, 0))],
        out_specs=[pl.BlockSpec((L, dv), lambda b, h, c: (b * NC + c, h)),
                   pl.BlockSpec((1, 1, dqk, dv), st),
                   pl.BlockSpec((1, 1, 1, dqk), st),
                   pl.BlockSpec((1, 1, 1, 1), st)],
        out_shape=[jax.ShapeDtypeStruct((R, H * dv), bf16),
                   jax.ShapeDtypeStruct((NS, H, dqk, dv), f32),
                   jax.ShapeDtypeStruct((NS, H, 1, dqk), f32),
                   jax.ShapeDtypeStruct((NS, H, 1, 1), f32)],
        scratch_shapes=[pltpu.VMEM((dqk, dv), f32), pltpu.VMEM((1, dqk), f32), pltpu.VMEM((1, 1), f32)],
        compiler_params=_cp(("parallel", "parallel", "arbitrary"), 32), name="mlstm")(
            z, z, z, ob, gcol, grow, c0, n0, m0, gnorm)


def _router_body(x_ref, g_ref, w_ref, b_ref, xn_ref, e_ref, wt_ref, *, n_groups, epg):
    x = x_ref[...]
    xn = x * lax.rsqrt(jnp.mean(x * x, axis=-1, keepdims=True) + RMS_EPS) * g_ref[...]
    xn_ref[...] = xn.astype(xn_ref.dtype)
    logits = _dot(xn.astype(bf16), w_ref[...]) + b_ref[...]
    R, W = logits.shape
    lane = lax.broadcasted_iota(i32, (R, W), 1)
    big = jnp.int32(W)

    def first_argmax(vals, mask):
        mx = jnp.max(jnp.where(mask, vals, -jnp.inf), axis=-1, keepdims=True)
        idx = jnp.min(jnp.where(mask & (vals == mx), lane, big), axis=-1, keepdims=True)
        return mx, idx

    gmask = lane < n_groups
    gprob = _masked_softmax_plain(logits, gmask)
    g_w, grp = first_argmax(gprob, gmask)
    lo = n_groups + grp * epg
    emask = (lane >= lo) & (lane < lo + epg)
    eprob = _masked_softmax_plain(logits, emask)
    p1, i1 = first_argmax(eprob, emask)
    p2, i2 = first_argmax(eprob, emask & (lane != i1))
    tot = p1 + p2
    e_ref[...] = jnp.where(lane == 0, i1 - n_groups, jnp.where(lane == 1, i2 - n_groups, 0))
    wt_ref[...] = jnp.where(lane == 0, g_w * p1 / tot, jnp.where(lane == 1, g_w * p2 / tot, 0.0))


def _masked_softmax_plain(x, mask):
    mx = jnp.max(jnp.where(mask, x, -jnp.inf), axis=-1, keepdims=True)
    e = jnp.where(mask, jnp.exp(x - mx), 0.0)
    return e / jnp.sum(e, axis=-1, keepdims=True)


def _router(x, g, w_r, b_r, n_groups, epg):
    M, D = x.shape
    tm = _pick(M, (256, 128, 8))
    row = lambda i: (i, 0)
    return pl.pallas_call(
        functools.partial(_router_body, n_groups=n_groups, epg=epg), grid=(M // tm,),
        in_specs=[pl.BlockSpec((tm, D), row), pl.BlockSpec((1, D), lambda i: (0, 0)),
                  pl.BlockSpec((D, 128), lambda i: (0, 0)), pl.BlockSpec((1, 128), lambda i: (0, 0))],
        out_specs=[pl.BlockSpec((tm, D), row), pl.BlockSpec((tm, 128), row), pl.BlockSpec((tm, 128), row)],
        out_shape=[jax.ShapeDtypeStruct((M, D), bf16), jax.ShapeDtypeStruct((M, 128), i32),
                   jax.ShapeDtypeStruct((M, 128), f32)],
        compiler_params=_cp(("parallel",), 40), name="router")(x, g.reshape(1, D).astype(f32), w_r, b_r)


def _gather_rows_body(idx_ref, nv_ref, x_hbm, o_hbm, sem, *, rb):
    base = pl.program_id(0) * rb

    @pl.when(base < nv_ref[0])
    def _():
        def issue(r, carry):
            pltpu.make_async_copy(x_hbm.at[pl.ds(idx_ref[base + r], 1)], o_hbm.at[pl.ds(base + r, 1)], sem).start()
            return carry

        def wait(r, carry):
            pltpu.make_async_copy(x_hbm.at[pl.ds(0, 1)], o_hbm.at[pl.ds(0, 1)], sem).wait()
            return carry

        lax.fori_loop(0, rb, issue, 0)
        lax.fori_loop(0, rb, wait, 0)


def _gather_rows(x, idx, n_valid, rb, name):
    R = idx.shape[0]
    gs = pltpu.PrefetchScalarGridSpec(
        num_scalar_prefetch=2, grid=(R // rb,),
        in_specs=[pl.BlockSpec(memory_space=pl.ANY)],
        out_specs=pl.BlockSpec(memory_space=pl.ANY),
        scratch_shapes=[pltpu.SemaphoreType.DMA(())])
    return pl.pallas_call(
        functools.partial(_gather_rows_body, rb=rb), grid_spec=gs,
        out_shape=jax.ShapeDtypeStruct((R,) + x.shape[1:], x.dtype),
        compiler_params=_cp(("arbitrary",)), name=name)(idx, n_valid, x)


def _moe_up_body(be_ref, nu_ref, x_ref, wg_ref, wu_ref, h_ref, wgb_ref, wub_ref):
    b = pl.program_id(1)
    changed = (b == 0) | (be_ref[b] != be_ref[jnp.maximum(b - 1, 0)])

    @pl.when(changed)
    def _():
        wgb_ref[...] = wg_ref[0].astype(bf16)
        wub_ref[...] = wu_ref[0].astype(bf16)

    @pl.when(b < nu_ref[0])
    def _():
        x = x_ref[...]
        a = _dot(x, wgb_ref[...])
        u = _dot(x, wub_ref[...])
        h_ref[...] = (a * _sigmoid(a) * u).astype(h_ref.dtype)

    @pl.when(b >= nu_ref[0])
    def _():
        h_ref[...] = jnp.zeros(h_ref.shape, h_ref.dtype)


def _moe_up(blk_expert, n_used, xs, w_g, w_u):
    R, D = xs.shape
    F = w_g.shape[2]
    tf = _pick(F, (256, 128))
    nblk = R // MOE_ROWS
    gs = pltpu.PrefetchScalarGridSpec(
        num_scalar_prefetch=2, grid=(F // tf, nblk),
        in_specs=[pl.BlockSpec((MOE_ROWS, D), lambda f, b, be, nu: (jnp.minimum(b, nu[0] - 1), 0)),
                  pl.BlockSpec((1, D, tf), lambda f, b, be, nu: (be[b], 0, f)),
                  pl.BlockSpec((1, D, tf), lambda f, b, be, nu: (be[b], 0, f))],
        out_specs=pl.BlockSpec((MOE_ROWS, tf), lambda f, b, be, nu: (b, f)),
        scratch_shapes=[pltpu.VMEM((D, tf), bf16), pltpu.VMEM((D, tf), bf16)])
    return pl.pallas_call(
        _moe_up_body, grid_spec=gs, out_shape=jax.ShapeDtypeStruct((R, F), bf16),
        compiler_params=_cp(("arbitrary", "arbitrary"), 40), name="moe_up")(blk_expert, n_used, xs, w_g, w_u)


def _moe_down_body(be_ref, nu_ref, h_ref, wd_ref, rw_ref, y_ref, wdb_ref):
    b = pl.program_id(1)
    changed = (b == 0) | (be_ref[b] != be_ref[jnp.maximum(b - 1, 0)])

    @pl.when(changed)
    def _():
        wdb_ref[...] = wd_ref[0].astype(bf16)

    @pl.when(b < nu_ref[0])
    def _():
        y_ref[...] = _dot(h_ref[...], wdb_ref[...]) * rw_ref[...]

    @pl.when(b >= nu_ref[0])
    def _():
        y_ref[...] = jnp.zeros(y_ref.shape, y_ref.dtype)


def _moe_down(blk_expert, n_used, h, w_d, row_w):
    R, F = h.shape
    D = w_d.shape[2]
    td = _pick(D, (1024, 512, 256, 128))
    nblk = R // MOE_ROWS
    gs = pltpu.PrefetchScalarGridSpec(
        num_scalar_prefetch=2, grid=(D // td, nblk),
        in_specs=[pl.BlockSpec((MOE_ROWS, F), lambda d, b, be, nu: (jnp.minimum(b, nu[0] - 1), 0)),
                  pl.BlockSpec((1, F, td), lambda d, b, be, nu: (be[b], 0, d)),
                  pl.BlockSpec((MOE_ROWS, 1), lambda d, b, be, nu: (b, 0))],
        out_specs=pl.BlockSpec((MOE_ROWS, td), lambda d, b, be, nu: (b, d)),
        scratch_shapes=[pltpu.VMEM((F, td), bf16)])
    return pl.pallas_call(
        _moe_down_body, grid_spec=gs, out_shape=jax.ShapeDtypeStruct((R, D), f32),
        compiler_params=_cp(("arbitrary", "arbitrary"), 40), name="moe_down")(blk_expert, n_used, h, w_d, row_w)


def _final_body(x_ref, y_ref, g_ref, o_ref):
    D = x_ref.shape[1]
    x = x_ref[...] + y_ref[:, :D] + y_ref[:, D:]
    y = x * lax.rsqrt(jnp.mean(x * x, axis=-1, keepdims=True) + RMS_EPS)
    o_ref[...] = y * g_ref[...]


def _final(x, ycomb, g):
    M, D = x.shape
    tm = _pick(M, (256, 128, 8))
    return pl.pallas_call(
        _final_body, grid=(M // tm,),
        in_specs=[pl.BlockSpec((tm, D), lambda i: (i, 0)), pl.BlockSpec((tm, 2 * D), lambda i: (i, 0)),
                  pl.BlockSpec((1, D), lambda i: (0, 0))],
        out_specs=pl.BlockSpec((tm, D), lambda i: (i, 0)),
        out_shape=jax.ShapeDtypeStruct((M, D), f32),
        compiler_params=_cp(("parallel",), 48), name="final_norm")(x, ycomb, g.reshape(1, D).astype(f32))


def _moe(x2, g_ffn, w_rg, b_rg, w_re, b_re, w_eg, w_eu, w_ed):
    M, D = x2.shape
    n_groups = w_rg.shape[1]
    E = w_re.shape[1]
    epg = E // n_groups
    w_r = jnp.zeros((D, 128), f32).at[:, :n_groups].set(w_rg).at[:, n_groups:n_groups + E].set(w_re).astype(bf16)
    b_r = jnp.zeros((1, 128), f32).at[0, :n_groups].set(b_rg).at[0, n_groups:n_groups + E].set(b_re)
    xn, e_out, w_out = _router(x2, g_ffn, w_r, b_r, n_groups, epg)

    A = M * TOP_K_WITHIN
    e_flat = e_out[:, :TOP_K_WITHIN].reshape(A)
    w_flat = w_out[:, :TOP_K_WITHIN].reshape(A)
    order = jnp.argsort(e_flat)
    e_sorted = e_flat[order]
    counts = jnp.bincount(e_flat, length=E).astype(i32)
    padded = (counts + MOE_ROWS - 1) // MOE_ROWS * MOE_ROWS
    pad_end = jnp.cumsum(padded)
    pad_start = pad_end - padded
    cnt_start = jnp.cumsum(counts) - counts
    dest_sorted = (pad_start[e_sorted] + jnp.arange(A, dtype=i32) - cnt_start[e_sorted]).astype(i32)
    nblk = -(-(A + E * (MOE_ROWS - 1)) // MOE_ROWS)
    R = nblk * MOE_ROWS
    row_tok = jnp.zeros((R,), i32).at[dest_sorted].set((order // TOP_K_WITHIN).astype(i32))
    row_w = jnp.zeros((R,), f32).at[dest_sorted].set(w_flat[order])
    dest = jnp.zeros((A,), i32).at[order].set(dest_sorted)
    n_used = (pad_end[-1] // MOE_ROWS).astype(i32)
    blk = jnp.arange(nblk, dtype=i32)
    blk_expert = jnp.minimum(jnp.searchsorted(pad_end, blk * MOE_ROWS, side="right"), E - 1).astype(i32)
    blk_expert = jnp.where(blk < n_used, blk_expert, blk_expert[jnp.maximum(n_used - 1, 0)])
    n_used1 = n_used.reshape(1)

    xn_u = lax.bitcast_convert_type(xn.reshape(M, D // 2, 2), jnp.uint32)
    xs_u = _gather_rows(xn_u, row_tok, jnp.full((1,), R, i32), MOE_ROWS, "moe_gather")
    xs = lax.bitcast_convert_type(xs_u, bf16).reshape(R, D)
    h = _moe_up(blk_expert, n_used1, xs, w_eg, w_eu)
    y_rows = _moe_down(blk_expert, n_used1, h, w_ed, row_w.reshape(R, 1))
    y_comb = _gather_rows(y_rows, dest, jnp.full((1,), A, i32), _pick(A, (256, 128, 8)), "moe_combine")
    return y_comb.reshape(M, TOP_K_WITHIN * D)


def _alibi_slopes(G, hg):
    h = np.arange(1, G * hg + 1, dtype=np.float32)
    return (2.0 ** (-8.0 * h / (G * hg))).astype(np.float32).reshape(G, hg)


def _slope_rows(G, hg, rows_per_head):
    s = _alibi_slopes(G, hg)
    return jnp.asarray(np.broadcast_to(np.repeat(s, rows_per_head, axis=1)[:, :, None],
                                       (G, hg * rows_per_head, 128)).copy())


def _overlap(n_rows, n_cmp, n_slc):
    cs = np.arange(n_cmp)[:, None] * CMP_STRIDE
    ss = np.arange(n_slc)[None, :] * SLC_BLOCK
    ov = np.clip(np.minimum(cs + CMP_BLOCK, ss + SLC_BLOCK) - np.maximum(cs, ss), 0, None)
    out = np.zeros((n_rows, 128), np.float32)
    out[:n_cmp, :n_slc] = ov.astype(np.float32) / np.float32(CMP_BLOCK)
    return jnp.asarray(out, bf16)


def kernel(x_prompt, x_sample, cache_cmp_kv, cache_slc_kv, cache_win_kv, state_mlstm_C, state_mlstm_n,
           state_mlstm_m, page_table, g_norm_mix, w_in, b_in, cmp_pos, w_cmp1, b_cmp1, w_cmp2, g_mlstm_norm,
           w_branch_a, w_branch_b, w_out, g_norm_ffn, w_router_group, b_router_group, w_router_expert,
           b_router_expert, w_exp_gate, w_exp_up, w_exp_down, g_norm_final):
    B, T, D = x_prompt.shape
    DB, TS, _ = x_sample.shape
    depth = w_in.shape[0]
    assert depth == 1, "single layer only"
    G, hd = cache_cmp_kv.shape[4], cache_cmp_kv.shape[5]
    assert hd == HEAD_DIM and cache_cmp_kv.shape[2] == PAGE_SIZE and SLC_BLOCK == 64
    NH = w_branch_a.shape[1] // HEAD_DIM
    hg = NH // G
    H, dqk, dv = state_mlstm_C.shape[2:]
    NP = page_table.shape[1]
    past = NP * PAGE_SIZE
    win_buf = cache_win_kv.shape[2]
    F_cmp = w_cmp1.shape[-1]
    tp, sp = SAMPLE_TOK_PAD, SAMPLE_PAD
    assert T % Q_BLOCK == 0 and T >= win_buf and TS <= tp and TS & (TS - 1) == 0
    n_cmp_s = (past + TS - CMP_BLOCK) // CMP_STRIDE + 1
    assert n_cmp_s == past // CMP_STRIDE - 1, "sample compression must not reach the new tokens"
    nslc_s = -(-(past + TS) // SLC_BLOCK)
    assert nslc_s == past // SLC_BLOCK + 1 and nslc_s <= 128 and -(-T // SLC_BLOCK) <= 128

    n_prompt, n_sample = B * T, DB * TS
    M0 = n_prompt + n_sample
    M = -(-M0 // ROW_ALIGN) * ROW_ALIGN
    xa = jnp.concatenate([x_prompt.reshape(n_prompt, D), x_sample.reshape(n_sample, D),
                          jnp.zeros((M - M0, D), f32)], axis=0)

    sizes = (NH * HEAD_DIM, 6 * G * HEAD_DIM, 3 * NH, H * dqk, H * dqk, H * dv, 2 * H, H * dv, 2 * D)
    offs = np.concatenate([[0], np.cumsum(sizes)])
    seg = lambda a, i, j=None: a[..., int(offs[i]):int(offs[(i if j is None else j) + 1])]
    w0, b0 = w_in[0], b_in[0]
    ngate = sizes[2] + sizes[6]
    assert ngate <= 128
    w_g = jnp.concatenate([seg(w0, 2), seg(w0, 6), jnp.zeros((D, 128 - ngate), f32)], axis=1).astype(bf16)
    b_g = jnp.concatenate([seg(b0, 2), seg(b0, 6), jnp.zeros((128 - ngate,), f32)])
    xn = _rmsnorm(xa, g_norm_mix[0], bf16)
    zA = _mm_bias(xn, seg(w0, 0, 1).astype(bf16), seg(b0, 0, 1), f32, "in_proj_nsa")
    zB = _mm_bias(xn, seg(w0, 3, 5).astype(bf16), seg(b0, 3, 5), f32, "in_proj_mlstm")
    zC = _mm_bias(xn, seg(w0, 7, 8).astype(bf16), seg(b0, 7, 8), bf16, "in_proj_gates")
    zG = _mm_bias(xn, w_g, b_g, f32, "in_proj_small")
    qblocks = NH

    w1c = (w_cmp1[0].reshape(2, CMP_STRIDE, 2, HEAD_DIM, F_cmp).transpose(2, 1, 3, 0, 4)
           .reshape(2, CMP_STRIDE * HEAD_DIM, 2 * F_cmp).astype(bf16))
    pos_bias = (b_cmp1[0] + jnp.einsum("lcd,lcdf->cf", cmp_pos[0], w_cmp1[0],
                                       precision=lax.Precision.HIGHEST)).reshape(2, 1, F_cmp)
    w2c = w_cmp2[0].astype(bf16)
    kvc = _compress_prompt(zA, w1c, pos_bias, w2c, B, T, G, qblocks)
    ga_all = zG[:, :3 * NH].reshape(M, 3, G, hg)
    gates_p = jnp.pad(ga_all.transpose(2, 0, 1, 3).reshape(G, M, 3 * hg), ((0, 0), (0, 0), (0, 128 - 3 * hg)))
    nch_p = T // CMP_STRIDE
    ov_p = _overlap(nch_p, nch_p - 1, -(-T // SLC_BLOCK))
    h_a_p = _nsa_prompt(zA, kvc, gates_p, _slope_rows(G, hg, Q_BLOCK), ov_p, B, T, G, hg, qblocks)

    zA_s = zA[n_prompt:M0]
    q_s = zA_s[:, :NH * HEAD_DIM].reshape(DB, TS, G, hg, HEAD_DIM).transpose(0, 2, 3, 1, 4)
    q_s = jnp.pad(q_s, ((0, 0), (0, 0), (0, 0), (0, tp - TS), (0, 0))).reshape(DB, G, hg * tp, HEAD_DIM)
    kvn = jnp.pad(zA_s[:, NH * HEAD_DIM:].reshape(DB, TS, 6 * G * HEAD_DIM), ((0, 0), (0, tp - TS), (0, 0)))
    gates_s = ga_all[n_prompt:M0].reshape(DB, TS, 3, G, hg).transpose(0, 3, 4, 1, 2)
    gates_s = jnp.pad(gates_s, ((0, 0), (0, 0), (0, 0), (0, tp - TS), (0, 125))).reshape(DB, G, hg * tp, 128)
    slopes_s = _slope_rows(G, hg, tp)
    pt_flat = page_table.reshape(-1).astype(i32)
    page_w = 2 * G * HEAD_DIM
    nch_s = past // CMP_STRIDE
    ov_s = _overlap(nch_s, n_cmp_s, nslc_s)
    o_cmp_s, sel_s = _nsa_sample_cmp(pt_flat, cache_cmp_kv[0].reshape(-1, PAGE_SIZE, page_w), w1c, pos_bias, w2c,
                                     q_s, slopes_s, ov_s, DB, NP, G, hg, past, nslc_s)
    h_a_s = _nsa_sample_sw(pt_flat, cache_slc_kv[0].reshape(-1, PAGE_SIZE, page_w), q_s, sel_s, kvn,
                           cache_win_kv[0].reshape(DB, win_buf, page_w), gates_s, slopes_s, o_cmp_s,
                           DB, NP, G, hg, past, TS)
    h_a_s = (h_a_s.reshape(DB, G, hg, tp, HEAD_DIM)[:, :, :, :TS].transpose(0, 3, 1, 2, 4)
             .reshape(n_sample, NH * HEAD_DIM).astype(bf16))
    h_a = jnp.concatenate([h_a_p, h_a_s, jnp.zeros((M - M0, NH * HEAD_DIM), bf16)], axis=0)

    if_all = zG[:, 3 * NH:3 * NH + 2 * H].reshape(M, 2, H)
    gnorm = g_mlstm_norm[0].reshape(H, 1, dv).astype(f32)
    L = math.gcd(T, MLSTM_CHUNK)
    NC = T // L
    if_p = if_all[:n_prompt].reshape(B * NC, L, 2, H)
    h_b_p, C_p, n_p, m_p = _mlstm(
        zB, 0, H, (2 * H * dqk) // dv, zC, 0, if_p.transpose(3, 0, 1, 2), if_p.transpose(3, 0, 2, 1),
        jnp.zeros((B, H, dqk, dv), f32), jnp.zeros((B, H, 1, dqk), f32), jnp.zeros((B, H, 1, 1), f32),
        gnorm, B, NC, L, H, dqk, dv)
    padrows = lambda a: jnp.pad(a.reshape(DB, TS, -1), ((0, 0), (0, sp - TS), (0, 0))).reshape(DB * sp, -1)
    if_s = if_all[n_prompt:M0].reshape(DB, TS, 2, H)
    pad_gate = jnp.broadcast_to(jnp.asarray([NEG_INF, 1e4], f32)[None, None, :, None], (DB, sp - TS, 2, H))
    if_s = jnp.concatenate([if_s, pad_gate], axis=1)
    h_b_s, C_s, n_s, m_s = _mlstm(
        padrows(zB[n_prompt:M0]), 0, H, (2 * H * dqk) // dv, padrows(zC[n_prompt:M0, :H * dv]), 0,
        if_s.transpose(3, 0, 1, 2), if_s.transpose(3, 0, 2, 1),
        state_mlstm_C[0], state_mlstm_n[0].reshape(DB, H, 1, dqk), state_mlstm_m[0].reshape(DB, H, 1, 1),
        gnorm, DB, 1, sp, H, dqk, dv)
    h_b = jnp.concatenate([h_b_p, h_b_s.reshape(DB, sp, H * dv)[:, :TS].reshape(n_sample, H * dv),
                           jnp.zeros((M - M0, H * dv), bf16)], axis=0)

    mix = _merge(h_a, w_branch_a[0].astype(bf16), h_b, w_branch_b[0].astype(bf16), zC, H * dv)
    x2 = _mm_residual(mix, w_out[0].astype(bf16), xa)
    y_comb = _moe(x2, g_norm_ffn[0], w_router_group[0], b_router_group[0], w_router_expert[0],
                  b_router_expert[0], w_exp_gate[0], w_exp_up[0], w_exp_down[0])
    y = _final(x2, y_comb, g_norm_final)

    kv_shape = (2, G, HEAD_DIM)
    kv_p = zA[:n_prompt, NH * HEAD_DIM:].reshape(B, T, 3, *kv_shape)
    kv_s = zA_s[:, NH * HEAD_DIM:].reshape(DB, TS, 3, *kv_shape)
    win_s = jnp.concatenate([cache_win_kv[0], kv_s[:, :, 2]], axis=1)[:, -win_buf:]
    return (y[:n_prompt].reshape(B, T, D), y[n_prompt:M0].reshape(DB, TS, D),
            kv_p[:, :, 0][None], kv_s[:, :, 0][None], kv_p[:, :, 1][None], kv_s[:, :, 1][None],
            kv_p[:, -win_buf:, 2][None], win_s[None],
            C_p[None], C_s[None], n_p.reshape(1, B, H, dqk), n_s.reshape(1, DB, H, dqk),
            m_p.reshape(1, B, H), m_s.reshape(1, DB, H))
```

```python
import functools
import math

import numpy as np
import jax
import jax.numpy as jnp
from jax import lax
from jax.experimental import pallas as pl
from jax.experimental.pallas import tpu as pltpu

f32 = jnp.float32
bf16 = jnp.bfloat16
i32 = jnp.int32

HEAD_DIM = 128
PAGE_SIZE = 128
CMP_BLOCK = 32
CMP_STRIDE = 16
SLC_BLOCK = 64
N_SELECT = 16
WINDOW = 512
Q_BLOCK = 128
TOP_K_WITHIN = 2
RMS_EPS = 1e-6
NEG_INF = -1e30
FORCE_BONUS = 1e4

ROW_ALIGN = 256
MOE_ROWS = 128
MLSTM_CHUNK = 256
SAMPLE_PAD = 16
SAMPLE_TOK_PAD = 8
SAMPLE_NEW_ROWS = 16
SAMPLE_NEW_PAD = 128
NSA_KEY_TILE = 512
MIB = 1024 * 1024


def _cp(sem, vmem_mib=None):
    kw = dict(dimension_semantics=sem)
    if vmem_mib is not None:
        kw["vmem_limit_bytes"] = int(vmem_mib * MIB)
    return pltpu.CompilerParams(**kw)


def _pick(n, cands):
    for c in cands:
        if n % c == 0:
            return c
    return n


def _sigmoid(x):
    return 1.0 / (1.0 + jnp.exp(-x))


def _log_sigmoid(x):
    return jnp.minimum(x, 0.0) - jnp.log(1.0 + jnp.exp(-jnp.abs(x)))


def _dot(a, b):
    return jnp.dot(a, b, preferred_element_type=f32)


def _dot_nt(a, b):
    return lax.dot_general(a, b, (((1,), (1,)), ((), ())), preferred_element_type=f32)


def _dot_tn(a, b):
    return lax.dot_general(a, b, (((0,), (0,)), ((), ())), preferred_element_type=f32)


def _dot_hilo(a, b_bf16):
    hi = a.astype(bf16)
    lo = (a - hi.astype(f32)).astype(bf16)
    return _dot(hi, b_bf16) + _dot(lo, b_bf16)


def _masked_softmax(s, mask):
    s = jnp.where(mask, s, NEG_INF)
    p = jnp.where(mask, jnp.exp(s - jnp.max(s, axis=-1, keepdims=True)), 0.0)
    return p / jnp.maximum(jnp.sum(p, axis=-1, keepdims=True), 1.0)


def _rmsnorm_body(x_ref, g_ref, o_ref):
    x = x_ref[...]
    y = x * lax.rsqrt(jnp.mean(x * x, axis=-1, keepdims=True) + RMS_EPS)
    o_ref[...] = (y * g_ref[...]).astype(o_ref.dtype)


def _rmsnorm(x, g, out_dtype):
    M, D = x.shape
    tm = _pick(M, (256, 128, 8))
    return pl.pallas_call(
        _rmsnorm_body, grid=(M // tm,),
        in_specs=[pl.BlockSpec((tm, D), lambda i: (i, 0)), pl.BlockSpec((1, D), lambda i: (0, 0))],
        out_specs=pl.BlockSpec((tm, D), lambda i: (i, 0)),
        out_shape=jax.ShapeDtypeStruct((M, D), out_dtype),
        compiler_params=_cp(("parallel",), 40), name="rmsnorm")(x, g.reshape(1, D).astype(f32))


def _mm_bias_body(x_ref, w_ref, b_ref, o_ref):
    o_ref[...] = (_dot(x_ref[...], w_ref[...]) + b_ref[...]).astype(o_ref.dtype)


def _mm_bias(x, w, b, out_dtype, name):
    M, K = x.shape
    N = w.shape[1]
    tm = _pick(M, (1088, 1024, 768, 512, 256))
    tn = _pick(N, (512, 256, 128))
    return pl.pallas_call(
        _mm_bias_body, grid=(M // tm, N // tn),
        in_specs=[pl.BlockSpec((tm, K), lambda i, j: (i, 0)),
                  pl.BlockSpec((K, tn), lambda i, j: (0, j)),
                  pl.BlockSpec((1, tn), lambda i, j: (0, j))],
        out_specs=pl.BlockSpec((tm, tn), lambda i, j: (i, j)),
        out_shape=jax.ShapeDtypeStruct((M, N), out_dtype),
        compiler_params=_cp(("parallel", "parallel"), 48), name=name)(x, w, b.reshape(1, N).astype(f32))


def _merge_body(ha_ref, wa_ref, hb_ref, wb_ref, ga_ref, gb_ref, o_ref):
    a = _dot(ha_ref[...], wa_ref[...])
    b = _dot(hb_ref[...], wb_ref[...])
    o_ref[...] = (_sigmoid(ga_ref[...].astype(f32)) * a + _sigmoid(gb_ref[...].astype(f32)) * b).astype(o_ref.dtype)


def _merge(h_a, w_a, h_b, w_b, zc, gate_col0):
    M, Ka = h_a.shape
    Kb = h_b.shape[1]
    D = w_a.shape[1]
    tm = _pick(M, (544, 512, 256))
    tn = _pick(D, (512, 256, 128))
    c0 = gate_col0 // tn
    nd = D // tn
    return pl.pallas_call(
        _merge_body, grid=(M // tm, nd),
        in_specs=[pl.BlockSpec((tm, Ka), lambda i, j: (i, 0)),
                  pl.BlockSpec((Ka, tn), lambda i, j: (0, j)),
                  pl.BlockSpec((tm, Kb), lambda i, j: (i, 0)),
                  pl.BlockSpec((Kb, tn), lambda i, j: (0, j)),
                  pl.BlockSpec((tm, tn), lambda i, j: (i, c0 + j)),
                  pl.BlockSpec((tm, tn), lambda i, j: (i, c0 + nd + j))],
        out_specs=pl.BlockSpec((tm, tn), lambda i, j: (i, j)),
        out_shape=jax.ShapeDtypeStruct((M, D), bf16),
        compiler_params=_cp(("parallel", "parallel"), 48), name="merge")(h_a, w_a, h_b, w_b, zc, zc)


def _mm_res_body(x_ref, w_ref, r_ref, o_ref):
    o_ref[...] = r_ref[...] + _dot(x_ref[...], w_ref[...])


def _mm_residual(x, w, res):
    M, K = x.shape
    N = w.shape[1]
    tm = _pick(M, (1088, 1024, 768, 512, 256))
    tn = _pick(N, (512, 256, 128))
    return pl.pallas_call(
        _mm_res_body, grid=(M // tm, N // tn),
        in_specs=[pl.BlockSpec((tm, K), lambda i, j: (i, 0)),
                  pl.BlockSpec((K, tn), lambda i, j: (0, j)),
                  pl.BlockSpec((tm, tn), lambda i, j: (i, j))],
        out_specs=pl.BlockSpec((tm, tn), lambda i, j: (i, j)),
        out_shape=jax.ShapeDtypeStruct((M, N), f32),
        compiler_params=_cp(("parallel", "parallel"), 48), name="out_proj")(x, w, res)


def _compress_body(x_ref, w1_ref, pb_ref, w2_ref, o_ref, xcat_ref, *, nch):
    for l in range(CMP_STRIDE):
        xcat_ref[:, l * HEAD_DIM:(l + 1) * HEAD_DIM] = x_ref[pl.ds(l, nch, stride=CMP_STRIDE), :].astype(bf16)
    parts = _dot(xcat_ref[...], w1_ref[0])
    F = parts.shape[1] // 2
    h = parts[:, :F] + pltpu.roll(parts[:, F:], nch - 1, 0) + pb_ref[0]
    h = jax.nn.gelu(h, approximate=True)
    o_ref[0, 0, 0] = _dot(h.astype(bf16), w2_ref[0])


def _compress_prompt(zA, w1c, pb, w2, B, T, G, qblocks):
    nch = T // CMP_STRIDE
    F2 = w1c.shape[2]
    return pl.pallas_call(
        functools.partial(_compress_body, nch=nch), grid=(B, 2, G),
        in_specs=[pl.BlockSpec((T, HEAD_DIM), lambda b, c, g: (b, qblocks + c * G + g)),
                  pl.BlockSpec((1, CMP_STRIDE * HEAD_DIM, F2), lambda b, c, g: (c, 0, 0)),
                  pl.BlockSpec((1, 1, F2 // 2), lambda b, c, g: (c, 0, 0)),
                  pl.BlockSpec((1, F2 // 2, HEAD_DIM), lambda b, c, g: (c, 0, 0))],
        out_specs=pl.BlockSpec((1, 1, 1, nch, HEAD_DIM), lambda b, c, g: (b, c, g, 0, 0)),
        out_shape=jax.ShapeDtypeStruct((B, 2, G, nch, HEAD_DIM), f32),
        scratch_shapes=[pltpu.VMEM((nch, CMP_STRIDE * HEAD_DIM), bf16)],
        compiler_params=_cp(("parallel", "parallel", "parallel"), 32), name="compress_prompt")(zA, w1c, pb, w2)


def _select_blocks_t(imp_t, cur, nblk, n_sel):
    Bk, W = imp_t.shape
    blk = lax.broadcasted_iota(i32, (Bk, W), 0)
    valid = blk <= cur
    forced = (blk == 0) | (blk == cur) | (blk == cur - 1)
    v = jnp.where(valid, imp_t + jnp.where(forced, FORCE_BONUS, 0.0), NEG_INF)
    cnt = jnp.zeros((Bk, W), f32)
    for i in range(nblk):
        ci = v[i:i + 1, :]
        tie = jnp.where(blk > i, 1.0, 0.0)
        cnt = cnt + jnp.where(ci > v, 1.0, jnp.where(ci == v, tie, 0.0))
    return jnp.where(valid, jnp.where(cnt < n_sel, 1.0, 0.0), 0.0)


def _nsa_prompt_body(slope_ref, q_ref, kc_ref, vc_ref, ks_ref, vs_ref, kw_ref, vw_ref, gate_ref, ovt_ref, o_ref,
                     m_ref, l_ref, acc_ref, *, T, hg, ncp, nslc, n_sel, window, wk, kt):
    g = pl.program_id(1)
    qb = pl.program_id(2)
    QB = Q_BLOCK
    q = q_ref[...] * (HEAD_DIM ** -0.5)
    qs = jnp.concatenate([q[:, h * HEAD_DIM:(h + 1) * HEAD_DIM] for h in range(hg)], axis=0).astype(bf16)
    slopes = [slope_ref[g * hg + h] for h in range(hg)]
    heads = lambda a: [a[h * QB:(h + 1) * QB] for h in range(hg)]
    qpos = lambda w: qb * QB + lax.broadcasted_iota(i32, (QB, w), 0)

    kc = kc_ref[0, 0, 0].astype(bf16)
    vc = vc_ref[0, 0, 0].astype(bf16)
    dist = qpos(ncp) - (lax.broadcasted_iota(i32, (QB, ncp), 1) * CMP_STRIDE + (CMP_BLOCK - 1))
    distf, valid = dist.astype(f32), dist >= 0
    ps = [_masked_softmax(sh - slopes[h] * distf, valid) for h, sh in enumerate(heads(_dot_nt(qs, kc)))]
    o_cmp = _dot(jnp.concatenate(ps, axis=0).astype(bf16), vc)
    psum = ps[0]
    for h in range(1, hg):
        psum = psum + ps[h]
    hi = psum.astype(bf16)
    lo = (psum - hi.astype(f32)).astype(bf16)
    imp_t = _dot_nt(ovt_ref[...], hi) + _dot_nt(ovt_ref[...], lo)
    nbp = imp_t.shape[0]
    cur = (qb * QB + lax.broadcasted_iota(i32, (nbp, QB), 1)) >> 6
    sel_t = _select_blocks_t(imp_t, cur, nslc, n_sel).astype(bf16)

    m_ref[...] = jnp.full(m_ref.shape, NEG_INF, f32)
    l_ref[...] = jnp.zeros(l_ref.shape, f32)
    acc_ref[...] = jnp.zeros(acc_ref.shape, f32)

    def tile(i, carry):
        k0 = pl.multiple_of(i * kt, kt)
        k = ks_ref[pl.ds(k0, kt), :].astype(bf16)
        v = vs_ref[pl.ds(k0, kt), :].astype(bf16)
        kpos = k0 + lax.broadcasted_iota(i32, (QB, kt), 1)
        d = qpos(kt) - kpos
        e = jnp.where(lax.broadcasted_iota(i32, (nbp, kt), 0)
                      == ((k0 + lax.broadcasted_iota(i32, (nbp, kt), 1)) >> 6), 1.0, 0.0).astype(bf16)
        madd = jnp.where(d >= 0, jnp.where(_dot_tn(sel_t, e) > 0.5, 0.0, NEG_INF), NEG_INF)
        df = d.astype(f32)
        sc = jnp.concatenate([sh - slopes[h] * df + madd for h, sh in enumerate(heads(_dot_nt(qs, k)))], axis=0)
        m_old = m_ref[...]
        m_new = jnp.maximum(m_old, jnp.max(sc, axis=-1, keepdims=True))
        alpha = jnp.exp(m_old - m_new)
        pt = jnp.exp(sc - m_new)
        l_ref[...] = alpha * l_ref[...] + jnp.sum(pt, axis=-1, keepdims=True)
        acc_ref[...] = alpha * acc_ref[...] + _dot(pt.astype(bf16), v)
        m_ref[...] = m_new
        return carry

    lax.fori_loop(0, (qb * QB + QB + kt - 1) // kt, tile, 0)
    o_slc = acc_ref[...] / jnp.maximum(l_ref[...], 1.0)

    start = pl.multiple_of(jnp.clip(qb * QB - window, 0, T - wk), QB)
    kw = kw_ref[pl.ds(start, wk), :].astype(bf16)
    vw = vw_ref[pl.ds(start, wk), :].astype(bf16)
    dw = qpos(wk) - (start + lax.broadcasted_iota(i32, (QB, wk), 1))
    maddw = jnp.where(dw >= 0, jnp.where(dw < window, 0.0, NEG_INF), NEG_INF)
    dwf = dw.astype(f32)
    sw = jnp.concatenate([sh - slopes[h] * dwf + maddw for h, sh in enumerate(heads(_dot_nt(qs, kw)))], axis=0)
    ew = jnp.exp(sw - jnp.max(sw, axis=-1, keepdims=True))
    pw = ew / jnp.maximum(jnp.sum(ew, axis=-1, keepdims=True), 1.0)
    o_win = _dot(pw.astype(bf16), vw)

    ga = _sigmoid(gate_ref[0])
    outs = []
    for h in range(hg):
        sl = slice(h * Q_BLOCK, (h + 1) * Q_BLOCK)
        outs.append(ga[:, h:h + 1] * o_cmp[sl] + ga[:, hg + h:hg + h + 1] * o_slc[sl]
                    + ga[:, 2 * hg + h:2 * hg + h + 1] * o_win[sl])
    o_ref[...] = jnp.concatenate(outs, axis=1).astype(o_ref.dtype)


def _nsa_prompt(zA, kvc, gates, slopes, ov_t, B, T, G, hg, qblocks):
    NQ = T // Q_BLOCK
    ncp = kvc.shape[3]
    nslc = -(-T // SLC_BLOCK)
    wk = min(WINDOW + Q_BLOCK, T)
    kt = math.gcd(T, NSA_KEY_TILE)
    HQ = hg * Q_BLOCK

    def kvspec(br, c):
        return pl.BlockSpec((T, HEAD_DIM), lambda b, g, i: (b, qblocks + (br * 2 + c) * G + g))

    body = functools.partial(_nsa_prompt_body, T=T, hg=hg, ncp=ncp, nslc=nslc, n_sel=min(N_SELECT, nslc),
                             window=WINDOW, wk=wk, kt=kt)
    return pl.pallas_call(
        body, grid=(B, G, NQ),
        in_specs=[pl.BlockSpec(memory_space=pltpu.SMEM),
                  pl.BlockSpec((Q_BLOCK, hg * HEAD_DIM), lambda b, g, i: (b * NQ + i, g)),
                  pl.BlockSpec((1, 1, 1, ncp, HEAD_DIM), lambda b, g, i: (b, 0, g, 0, 0)),
                  pl.BlockSpec((1, 1, 1, ncp, HEAD_DIM), lambda b, g, i: (b, 1, g, 0, 0)),
                  kvspec(1, 0), kvspec(1, 1), kvspec(2, 0), kvspec(2, 1),
                  pl.BlockSpec((1, Q_BLOCK, 128), lambda b, g, i: (g, b * NQ + i, 0)),
                  pl.BlockSpec(ov_t.shape, lambda b, g, i: (0, 0))],
        out_specs=pl.BlockSpec((Q_BLOCK, hg * HEAD_DIM), lambda b, g, i: (b * NQ + i, g)),
        out_shape=jax.ShapeDtypeStruct((B * T, G * hg * HEAD_DIM), bf16),
        scratch_shapes=[pltpu.VMEM((HQ, 1), f32), pltpu.VMEM((HQ, 1), f32), pltpu.VMEM((HQ, HEAD_DIM), f32)],
        compiler_params=_cp(("parallel", "parallel", "arbitrary"), 48), name="nsa_prompt")(
            slopes, zA, kvc, kvc, zA, zA, zA, zA, gates, ov_t)


def _softmax_rows(sc, mask):
    sc = jnp.where(mask, sc, NEG_INF)
    e = jnp.where(mask, jnp.exp(sc - jnp.max(sc, axis=0, keepdims=True)), 0.0)
    return e / jnp.maximum(jnp.sum(e, axis=0, keepdims=True), 1.0)


def _nsa_sample_cmp_body(pt_ref, *refs, NP, G, hg, past, nslc, n_sel):
    pages = refs[:NP]
    (w1k_ref, w1v_ref, pbk_ref, pbv_ref, w2k_ref, w2v_ref, q_ref, slope_ref, ovr_ref, rsum_ref,
     o_ref, sel_ref, xk_ref, xv_ref) = refs[NP:]
    tp = SAMPLE_TOK_PAD
    cpp = PAGE_SIZE // CMP_STRIDE
    NR = NP * cpp * G
    lg, lq = G.bit_length() - 1, (hg * tp).bit_length() - 1
    low = lax.broadcasted_iota(i32, (2 * G, HEAD_DIM), 0) < G
    for k in range(NP):
        for j in range(cpp // 2):
            for l in range(CMP_STRIDE):
                ra = ((2 * j) * CMP_STRIDE + l) * 2 * G
                rb = ((2 * j + 1) * CMP_STRIDE + l) * 2 * G
                a = pages[k][ra:ra + 2 * G, :]
                b = pages[k][rb:rb + 2 * G, :]
                r0 = (k * (cpp // 2) + j) * 2 * G
                xk_ref[r0:r0 + 2 * G, l * HEAD_DIM:(l + 1) * HEAD_DIM] = jnp.where(low, a, pltpu.roll(b, G, 0))
                xv_ref[r0:r0 + 2 * G, l * HEAD_DIM:(l + 1) * HEAD_DIM] = jnp.where(low, pltpu.roll(a, G, 0), b)

    def compress(x_ref, w1_ref, pb_ref, w2_ref):
        parts = _dot(x_ref[...].astype(bf16), w1_ref[...])
        F = parts.shape[1] // 2
        h = parts[:, :F] + pltpu.roll(parts[:, F:], NR - G, 0) + pb_ref[...]
        return _dot(jax.nn.gelu(h, approximate=True).astype(bf16), w2_ref[...]).astype(bf16)

    kc = compress(xk_ref, w1k_ref, pbk_ref, w2k_ref)
    vc = compress(xv_ref, w1v_ref, pbv_ref, w2v_ref)
    qs = (q_ref[0] * (HEAD_DIM ** -0.5)).astype(bf16)
    row = lax.broadcasted_iota(i32, (NR, G * hg * tp), 0)
    lane = lax.broadcasted_iota(i32, (NR, G * hg * tp), 1)
    dist = past + (lane & (tp - 1)) - ((row >> lg) * CMP_STRIDE + (CMP_BLOCK - 1))
    valid = ((row & (G - 1)) == (lane >> lq)) & (dist >= 0)
    p = _softmax_rows(_dot_nt(kc, qs) - slope_ref[...] * dist.astype(f32), valid)
    o_ref[0] = _dot_tn(p.astype(bf16), vc)
    hi = p.astype(bf16)
    lo = (p - hi.astype(f32)).astype(bf16)
    imp = _dot_tn(ovr_ref[...], hi) + _dot_tn(ovr_ref[...], lo)
    imp = _dot_hilo(imp, rsum_ref[...])
    bp = imp.shape[0]
    cur = (past + (lax.broadcasted_iota(i32, (bp, G * hg * tp), 1) & (tp - 1))) >> 6
    sel_ref[0] = _select_blocks_t(imp, cur, nslc, n_sel)


def _page_specs(NP, rows):
    return [pl.BlockSpec((rows, HEAD_DIM), lambda b, pt, k=k: (pt[b * NP + k], 0)) for k in range(NP)]


def _nsa_sample_cmp(pt, pages, w1c, pb, w2, q_s, slope_l, ovr, rsum, DB, NP, G, hg, past, nslc):
    R = q_s.shape[1]
    NR = NP * (PAGE_SIZE // CMP_STRIDE) * G
    bp = ovr.shape[1]
    const2 = lambda b, pt: (0, 0)
    body = functools.partial(_nsa_sample_cmp_body, NP=NP, G=G, hg=hg, past=past, nslc=nslc,
                             n_sel=min(N_SELECT, nslc))
    consts = [w1c[0], w1c[1], pb[0], pb[1], w2[0], w2[1]]
    gs = pltpu.PrefetchScalarGridSpec(
        num_scalar_prefetch=1, grid=(DB,),
        in_specs=_page_specs(NP, PAGE_SIZE * 2 * G) + [pl.BlockSpec(c.shape, const2) for c in consts] + [
            pl.BlockSpec((1, R, HEAD_DIM), lambda b, pt: (b, 0, 0)),
            pl.BlockSpec(slope_l.shape, const2), pl.BlockSpec(ovr.shape, const2), pl.BlockSpec(rsum.shape, const2)],
        out_specs=[pl.BlockSpec((1, R, HEAD_DIM), lambda b, pt: (b, 0, 0)),
                   pl.BlockSpec((1, bp, R), lambda b, pt: (b, 0, 0))],
        scratch_shapes=[pltpu.VMEM((NR, CMP_STRIDE * HEAD_DIM), f32), pltpu.VMEM((NR, CMP_STRIDE * HEAD_DIM), f32)])
    return pl.pallas_call(
        body, grid_spec=gs,
        out_shape=[jax.ShapeDtypeStruct((DB, R, HEAD_DIM), f32), jax.ShapeDtypeStruct((DB, bp, R), f32)],
        compiler_params=_cp(("parallel",), 52), name="nsa_sample_cmp")(
            pt, *([pages] * NP), *consts, q_s, slope_l, ovr, rsum)


def _nsa_sample_attn_body(pt_ref, *refs, NP, G, hg, past, ts, window):
    pages = refs[:NP]
    (win_ref, q_ref, sel_ref, kvn_ref, gate_ref, slope_ref, ocmp_ref, o_ref,
     ks_ref, vs_ref, kw_ref, vw_ref) = refs[NP:]
    tp, npad = SAMPLE_TOK_PAD, SAMPLE_NEW_PAD
    R = G * hg * tp
    lq = (hg * tp).bit_length() - 1
    wb = win_ref.shape[0] // (2 * G)
    kvn = kvn_ref[0]
    nrow = kvn.shape[0]

    def fill(k_ref, v_ref, srcs, rows, br):
        for g in range(G):
            cols = slice(g * HEAD_DIM, (g + 1) * HEAD_DIM)
            for i, src in enumerate(srcs):
                k_ref[i * rows:(i + 1) * rows, cols] = src[pl.ds(g, rows, stride=2 * G), :].astype(bf16)
                v_ref[i * rows:(i + 1) * rows, cols] = src[pl.ds(G + g, rows, stride=2 * G), :].astype(bf16)
            n0 = len(srcs) * rows
            ok, ov = ((br * 2) * G + g) * HEAD_DIM, ((br * 2 + 1) * G + g) * HEAD_DIM
            k_ref[n0:n0 + nrow, cols] = kvn[:, ok:ok + HEAD_DIM].astype(bf16)
            v_ref[n0:n0 + nrow, cols] = kvn[:, ov:ov + HEAD_DIM].astype(bf16)
            k_ref[n0 + nrow:n0 + npad, cols] = jnp.zeros((npad - nrow, HEAD_DIM), bf16)
            v_ref[n0 + nrow:n0 + npad, cols] = jnp.zeros((npad - nrow, HEAD_DIM), bf16)

    fill(ks_ref, vs_ref, pages, PAGE_SIZE, 1)
    fill(kw_ref, vw_ref, [win_ref], wb, 2)

    qs = q_ref[0] * (HEAD_DIM ** -0.5)
    rowg = lax.broadcasted_iota(i32, (R, HEAD_DIM), 0) >> lq
    q_bd = jnp.concatenate([jnp.where(rowg == g, qs, 0.0) for g in range(G)], axis=1).astype(bf16)
    slope = slope_ref[...]

    def attend(k_ref, v_ref, kpos0, mask_fn):
        n = k_ref.shape[0]
        kpos = kpos0 + lax.broadcasted_iota(i32, (n, R), 0)
        d = past + (lax.broadcasted_iota(i32, (n, R), 1) & (tp - 1)) - kpos
        p = _softmax_rows(_dot_nt(k_ref[...], q_bd) - slope * d.astype(f32), mask_fn(kpos, d))
        o_all = _dot_tn(p.astype(bf16), v_ref[...])
        o = jnp.where(rowg == 0, o_all[:, :HEAD_DIM], 0.0)
        for g in range(1, G):
            o = o + jnp.where(rowg == g, o_all[:, g * HEAD_DIM:(g + 1) * HEAD_DIM], 0.0)
        return o

    sel = sel_ref[0]
    nblk = past // SLC_BLOCK
    selk = jnp.concatenate([jnp.broadcast_to(sel[j:j + 1, :], (SLC_BLOCK, R)) for j in range(nblk)]
                           + [jnp.broadcast_to(sel[nblk:nblk + 1, :], (npad, R))], axis=0)
    o_slc = attend(ks_ref, vs_ref, 0,
                   lambda kpos, d: (selk > 0.5) & (d >= 0) & (kpos < past + ts))
    o_win = attend(kw_ref, vw_ref, past - wb,
                   lambda kpos, d: (d >= 0) & (d < window) & (kpos < past + ts))
    ga = _sigmoid(gate_ref[0])
    o_ref[0] = ga[:, 0:1] * ocmp_ref[0] + ga[:, 1:2] * o_slc + ga[:, 2:3] * o_win


def _nsa_sample_attn(pt, pages, win, q_s, sel, kvn, gates_s, slope_l, o_cmp, DB, NP, G, hg, past, ts):
    R = q_s.shape[1]
    wrows = win.shape[0] // DB
    wb = wrows // (2 * G)
    body = functools.partial(_nsa_sample_attn_body, NP=NP, G=G, hg=hg, past=past, ts=ts, window=WINDOW)
    seq3 = lambda b, pt: (b, 0, 0)
    gs = pltpu.PrefetchScalarGridSpec(
        num_scalar_prefetch=1, grid=(DB,),
        in_specs=_page_specs(NP, PAGE_SIZE * 2 * G) + [
            pl.BlockSpec((wrows, HEAD_DIM), lambda b, pt: (b, 0)),
            pl.BlockSpec((1, R, HEAD_DIM), seq3),
            pl.BlockSpec((1, sel.shape[1], R), seq3),
            pl.BlockSpec((1,) + kvn.shape[1:], seq3),
            pl.BlockSpec((1, R, 128), seq3),
            pl.BlockSpec(slope_l.shape, lambda b, pt: (0, 0)),
            pl.BlockSpec((1, R, HEAD_DIM), seq3)],
        out_specs=pl.BlockSpec((1, R, HEAD_DIM), seq3),
        scratch_shapes=[pltpu.VMEM((past + SAMPLE_NEW_PAD, G * HEAD_DIM), bf16),
                        pltpu.VMEM((past + SAMPLE_NEW_PAD, G * HEAD_DIM), bf16),
                        pltpu.VMEM((wb + SAMPLE_NEW_PAD, G * HEAD_DIM), bf16),
                        pltpu.VMEM((wb + SAMPLE_NEW_PAD, G * HEAD_DIM), bf16)])
    return pl.pallas_call(
        body, grid_spec=gs, out_shape=jax.ShapeDtypeStruct((DB, R, HEAD_DIM), f32),
        compiler_params=_cp(("parallel",), 48), name="nsa_sample_attn")(
            pt, *([pages] * NP), win, q_s, sel, kvn, gates_s, slope_l, o_cmp)


def _mlstm_body(q_ref, k_ref, v_ref, ob_ref, gc_ref, gr_ref, c0_ref, n0_ref, m0_ref, gn_ref,
                h_ref, c_ref, n_ref, mo_ref, cs_ref, ns_ref, ms_ref, *, L, dqk):
    c = pl.program_id(2)
    nc = pl.num_programs(2)

    @pl.when(c == 0)
    def _():
        cs_ref[...] = c0_ref[0, 0]
        ns_ref[...] = n0_ref[0, 0]
        ms_ref[...] = m0_ref[0, 0]

    q = q_ref[...]
    k = k_ref[...] * (dqk ** -0.5)
    v = v_ref[...]
    gc = gc_ref[0, 0]
    gr = gr_ref[0, 0]
    i_col, f_col = gc[:, 0:1], _log_sigmoid(gc[:, 1:2])
    i_row, f_row = gr[0:1, :], _log_sigmoid(gr[1:2, :])
    r = lax.broadcasted_iota(i32, (L, L), 0)
    s = lax.broadcasted_iota(i32, (L, L), 1)
    causal = r >= s
    b_col = jnp.sum(jnp.where(causal, f_row, 0.0), axis=1, keepdims=True)
    b_row = jnp.sum(jnp.where(r <= s, f_col, 0.0), axis=0, keepdims=True)
    m_prev = ms_ref[...]
    g = b_col + m_prev
    dlog = jnp.where(causal, b_col - b_row + i_row, NEG_INF)
    m_t = jnp.maximum(g, jnp.max(dlog, axis=1, keepdims=True))
    w = jnp.exp(dlog - m_t)
    gw = jnp.exp(g - m_t)
    qb_, kb_, vb_ = q.astype(bf16), k.astype(bf16), v.astype(bf16)
    qk = _dot_nt(qb_, kb_) * w
    num = gw * _dot(qb_, cs_ref[...].astype(bf16)) + _dot(qk.astype(bf16), vb_)
    den = gw * jnp.sum(q * ns_ref[...], axis=1, keepdims=True) + jnp.sum(qk, axis=1, keepdims=True)
    hh = num / jnp.maximum(jnp.abs(den), jnp.exp(-m_t))
    hn = hh * lax.rsqrt(jnp.mean(hh * hh, axis=1, keepdims=True) + RMS_EPS) * gn_ref[0]
    h_ref[...] = (hn * _sigmoid(ob_ref[...].astype(f32))).astype(h_ref.dtype)

    b_last = b_col[L - 1:L, :]
    ws = b_last - b_col + i_col
    m_new = jnp.maximum(b_last + m_prev, jnp.max(ws, axis=0, keepdims=True))
    sw = jnp.exp(ws - m_new)
    cw = jnp.exp(b_last + m_prev - m_new)
    ksw = k * sw
    cs_ref[...] = cw * cs_ref[...] + _dot_tn(ksw.astype(bf16), vb_)
    ns_ref[...] = cw * ns_ref[...] + jnp.sum(ksw, axis=0, keepdims=True)
    ms_ref[...] = m_new

    @pl.when(c == nc - 1)
    def _():
        c_ref[0, 0] = cs_ref[...]
        n_ref[0, 0] = ns_ref[...]
        mo_ref[0, 0] = ms_ref[...]


def _mlstm(z, q_blk0, k_blk0, v_blk0, ob, ob_blk0, gcol, grow, c0, n0, m0, gnorm, NS, NC, L, H, dqk, dv):
    R = NS * NC * L
    st = lambda b, h, c: (b, h, 0, 0)
    return pl.pallas_call(
        functools.partial(_mlstm_body, L=L, dqk=dqk), grid=(NS, H, NC),
        in_specs=[pl.BlockSpec((L, dqk), lambda b, h, c: (b * NC + c, q_blk0 + h)),
                  pl.BlockSpec((L, dqk), lambda b, h, c: (b * NC + c, k_blk0 + h)),
                  pl.BlockSpec((L, dv), lambda b, h, c: (b * NC + c, v_blk0 + h)),
                  pl.BlockSpec((L, dv), lambda b, h, c: (b * NC + c, ob_blk0 + h)),
                  pl.BlockSpec((1, 1, L, 2), lambda b, h, c: (h, b * NC + c, 0, 0)),
                  pl.BlockSpec((1, 1, 2, L), lambda b, h, c: (h, b * NC + c, 0, 0)),
                  pl.BlockSpec((1, 1, dqk, dv), st),
                  pl.BlockSpec((1, 1, 1, dqk), st),
                  pl.BlockSpec((1, 1, 1, 1), st),
                  pl.BlockSpec((1, 1, dv), lambda b, h, c: (h, 0, 0))],
        out_specs=[pl.BlockSpec((L, dv), lambda b, h, c: (b * NC + c, h)),
                   pl.BlockSpec((1, 1, dqk, dv), st),
                   pl.BlockSpec((1, 1, 1, dqk), st),
                   pl.BlockSpec((1, 1, 1, 1), st)],
        out_shape=[jax.ShapeDtypeStruct((R, H * dv), bf16),
                   jax.ShapeDtypeStruct((NS, H, dqk, dv), f32),
                   jax.ShapeDtypeStruct((NS, H, 1, dqk), f32),
                   jax.ShapeDtypeStruct((NS, H, 1, 1), f32)],
        scratch_shapes=[pltpu.VMEM((dqk, dv), f32), pltpu.VMEM((1, dqk), f32), pltpu.VMEM((1, 1), f32)],
        compiler_params=_cp(("parallel", "parallel", "arbitrary"), 32), name="mlstm")(
            z, z, z, ob, gcol, grow, c0, n0, m0, gnorm)


def _router_body(x_ref, g_ref, w_ref, b_ref, xn_ref, e_ref, wt_ref, *, n_groups, epg):
    x = x_ref[...]
    xn = x * lax.rsqrt(jnp.mean(x * x, axis=-1, keepdims=True) + RMS_EPS) * g_ref[...]
    xn_ref[...] = xn.reshape(xn_ref.shape)
    logits = _dot(xn.astype(bf16), w_ref[...]) + b_ref[...]
    R, W = logits.shape
    lane = lax.broadcasted_iota(i32, (R, W), 1)
    big = jnp.int32(W)

    def first_argmax(vals, mask):
        mx = jnp.max(jnp.where(mask, vals, -jnp.inf), axis=-1, keepdims=True)
        idx = jnp.min(jnp.where(mask & (vals == mx), lane, big), axis=-1, keepdims=True)
        return mx, idx

    gmask = lane < n_groups
    gprob = _masked_softmax_plain(logits, gmask)
    g_w, grp = first_argmax(gprob, gmask)
    lo = n_groups + grp * epg
    emask = (lane >= lo) & (lane < lo + epg)
    eprob = _masked_softmax_plain(logits, emask)
    p1, i1 = first_argmax(eprob, emask)
    p2, i2 = first_argmax(eprob, emask & (lane != i1))
    tot = p1 + p2
    e_ref[...] = jnp.where(lane == 0, i1 - n_groups, jnp.where(lane == 1, i2 - n_groups, 0))
    wt_ref[...] = jnp.where(lane == 0, g_w * p1 / tot, jnp.where(lane == 1, g_w * p2 / tot, 0.0))


def _masked_softmax_plain(x, mask):
    mx = jnp.max(jnp.where(mask, x, -jnp.inf), axis=-1, keepdims=True)
    e = jnp.where(mask, jnp.exp(x - mx), 0.0)
    return e / jnp.sum(e, axis=-1, keepdims=True)


def _router(x, g, w_r, b_r, n_groups, epg):
    M, D = x.shape
    tm = _pick(M, (256, 128, 8))
    row = lambda i: (i, 0)
    return pl.pallas_call(
        functools.partial(_router_body, n_groups=n_groups, epg=epg), grid=(M // tm,),
        in_specs=[pl.BlockSpec((tm, D), row), pl.BlockSpec((1, D), lambda i: (0, 0)),
                  pl.BlockSpec((D, 128), lambda i: (0, 0)), pl.BlockSpec((1, 128), lambda i: (0, 0))],
        out_specs=[pl.BlockSpec((tm, 1, D), lambda i: (i, 0, 0)), pl.BlockSpec((tm, 128), row),
                   pl.BlockSpec((tm, 128), row)],
        out_shape=[jax.ShapeDtypeStruct((M, 1, D), f32), jax.ShapeDtypeStruct((M, 128), i32),
                   jax.ShapeDtypeStruct((M, 128), f32)],
        compiler_params=_cp(("parallel",), 40), name="router")(x, g.reshape(1, D).astype(f32), w_r, b_r)


def _gather_rows_body(idx_ref, x_hbm, o_hbm, sem, *, rb):
    base = pl.program_id(0) * rb

    def issue(r, carry):
        pltpu.make_async_copy(x_hbm.at[idx_ref[base + r]], o_hbm.at[base + r], sem).start()
        return carry

    lax.fori_loop(0, rb, issue, 0)
    pltpu.make_async_copy(o_hbm.at[pl.ds(base, rb)], o_hbm.at[pl.ds(base, rb)], sem).wait()


def _gather_rows(x, idx, rb, name):
    R = idx.shape[0]
    gs = pltpu.PrefetchScalarGridSpec(
        num_scalar_prefetch=1, grid=(R // rb,),
        in_specs=[pl.BlockSpec(memory_space=pl.ANY)],
        out_specs=pl.BlockSpec(memory_space=pl.ANY),
        scratch_shapes=[pltpu.SemaphoreType.DMA(())])
    return pl.pallas_call(
        functools.partial(_gather_rows_body, rb=rb), grid_spec=gs,
        out_shape=jax.ShapeDtypeStruct((R,) + x.shape[1:], x.dtype),
        compiler_params=_cp(("arbitrary",)), name=name)(idx, x)


def _moe_up_body(be_ref, nu_ref, x_ref, wg_ref, wu_ref, h_ref, wgb_ref, wub_ref):
    b = pl.program_id(1)
    changed = (b == 0) | (be_ref[b] != be_ref[jnp.maximum(b - 1, 0)])

    @pl.when(changed)
    def _():
        wgb_ref[...] = wg_ref[0].astype(bf16)
        wub_ref[...] = wu_ref[0].astype(bf16)

    @pl.when(b < nu_ref[0])
    def _():
        x = x_ref[...].reshape(x_ref.shape[0], x_ref.shape[2]).astype(bf16)
        a = _dot(x, wgb_ref[...])
        u = _dot(x, wub_ref[...])
        h_ref[...] = (a * _sigmoid(a) * u).astype(h_ref.dtype)

    @pl.when(b >= nu_ref[0])
    def _():
        h_ref[...] = jnp.zeros(h_ref.shape, h_ref.dtype)


def _moe_up(blk_expert, n_used, xs, w_g, w_u):
    R, _, D = xs.shape
    F = w_g.shape[2]
    tf = _pick(F, (512, 256, 128))
    nblk = R // MOE_ROWS
    gs = pltpu.PrefetchScalarGridSpec(
        num_scalar_prefetch=2, grid=(F // tf, nblk),
        in_specs=[pl.BlockSpec((MOE_ROWS, 1, D), lambda f, b, be, nu: (jnp.minimum(b, nu[0] - 1), 0, 0)),
                  pl.BlockSpec((1, D, tf), lambda f, b, be, nu: (be[b], 0, f)),
                  pl.BlockSpec((1, D, tf), lambda f, b, be, nu: (be[b], 0, f))],
        out_specs=pl.BlockSpec((MOE_ROWS, tf), lambda f, b, be, nu: (b, f)),
        scratch_shapes=[pltpu.VMEM((D, tf), bf16), pltpu.VMEM((D, tf), bf16)])
    return pl.pallas_call(
        _moe_up_body, grid_spec=gs, out_shape=jax.ShapeDtypeStruct((R, F), bf16),
        compiler_params=_cp(("arbitrary", "arbitrary"), 52), name="moe_up")(blk_expert, n_used, xs, w_g, w_u)


def _moe_down_body(be_ref, nu_ref, h_ref, wd_ref, rw_ref, y_ref, wdb_ref):
    b = pl.program_id(1)
    changed = (b == 0) | (be_ref[b] != be_ref[jnp.maximum(b - 1, 0)])

    @pl.when(changed)
    def _():
        wdb_ref[...] = wd_ref[0].astype(bf16)

    @pl.when(b < nu_ref[0])
    def _():
        y_ref[...] = (_dot(h_ref[...], wdb_ref[...]) * rw_ref[...]).reshape(y_ref.shape)

    @pl.when(b >= nu_ref[0])
    def _():
        y_ref[...] = jnp.zeros(y_ref.shape, y_ref.dtype)


def _moe_down(blk_expert, n_used, h, w_d, row_w):
    R, F = h.shape
    D = w_d.shape[2]
    td = _pick(D, (1024, 512, 256, 128))
    nblk = R // MOE_ROWS
    gs = pltpu.PrefetchScalarGridSpec(
        num_scalar_prefetch=2, grid=(D // td, nblk),
        in_specs=[pl.BlockSpec((MOE_ROWS, F), lambda d, b, be, nu: (jnp.minimum(b, nu[0] - 1), 0)),
                  pl.BlockSpec((1, F, td), lambda d, b, be, nu: (be[b], 0, d)),
                  pl.BlockSpec((MOE_ROWS, 1), lambda d, b, be, nu: (b, 0))],
        out_specs=pl.BlockSpec((MOE_ROWS, 1, td), lambda d, b, be, nu: (b, 0, d)),
        scratch_shapes=[pltpu.VMEM((F, td), bf16)])
    return pl.pallas_call(
        _moe_down_body, grid_spec=gs, out_shape=jax.ShapeDtypeStruct((R, 1, D), f32),
        compiler_params=_cp(("arbitrary", "arbitrary"), 40), name="moe_down")(blk_expert, n_used, h, w_d, row_w)


def _final_body(dest_ref, x_ref, y_hbm, g_ref, o_ref, ybuf_ref, y2d_ref, sem, *, tm, topk):
    base = pl.program_id(0) * (tm * topk)

    def issue(t, carry):
        for k in range(topk):
            pltpu.make_async_copy(y_hbm.at[dest_ref[base + t * topk + k]], ybuf_ref.at[k * tm + t], sem).start()
        return carry

    lax.fori_loop(0, tm, issue, 0)
    pltpu.make_async_copy(ybuf_ref, ybuf_ref, sem).wait()
    y2d_ref[...] = ybuf_ref[...].reshape(y2d_ref.shape)
    x = x_ref[...]
    for k in range(topk):
        x = x + y2d_ref[k * tm:(k + 1) * tm, :]
    y = x * lax.rsqrt(jnp.mean(x * x, axis=-1, keepdims=True) + RMS_EPS)
    o_ref[...] = y * g_ref[...]


def _final(x, y_rows, dest, g):
    M, D = x.shape
    tm = _pick(M, (128, 8))
    topk = dest.shape[0] // M
    gs = pltpu.PrefetchScalarGridSpec(
        num_scalar_prefetch=1, grid=(M // tm,),
        in_specs=[pl.BlockSpec((tm, D), lambda i, d: (i, 0)), pl.BlockSpec(memory_space=pl.ANY),
                  pl.BlockSpec((1, D), lambda i, d: (0, 0))],
        out_specs=pl.BlockSpec((tm, D), lambda i, d: (i, 0)),
        scratch_shapes=[pltpu.VMEM((topk * tm, 1, D), f32), pltpu.VMEM((topk * tm, D), f32),
                        pltpu.SemaphoreType.DMA(())])
    return pl.pallas_call(
        functools.partial(_final_body, tm=tm, topk=topk), grid_spec=gs,
        out_shape=jax.ShapeDtypeStruct((M, D), f32),
        compiler_params=_cp(("arbitrary",), 40), name="final_norm")(dest, x, y_rows, g.reshape(1, D).astype(f32))


def _moe(x2, g_ffn, w_rg, b_rg, w_re, b_re, w_eg, w_eu, w_ed):
    M, D = x2.shape
    n_groups = w_rg.shape[1]
    E = w_re.shape[1]
    epg = E // n_groups
    w_r = jnp.zeros((D, 128), f32).at[:, :n_groups].set(w_rg).at[:, n_groups:n_groups + E].set(w_re).astype(bf16)
    b_r = jnp.zeros((1, 128), f32).at[0, :n_groups].set(b_rg).at[0, n_groups:n_groups + E].set(b_re)
    xn, e_out, w_out = _router(x2, g_ffn, w_r, b_r, n_groups, epg)

    A = M * TOP_K_WITHIN
    e_flat = e_out[:, :TOP_K_WITHIN].reshape(A)
    w_flat = w_out[:, :TOP_K_WITHIN].reshape(A)
    order = jnp.argsort(e_flat)
    e_sorted = e_flat[order]
    counts = jnp.bincount(e_flat, length=E).astype(i32)
    padded = (counts + MOE_ROWS - 1) // MOE_ROWS * MOE_ROWS
    pad_end = jnp.cumsum(padded)
    pad_start = pad_end - padded
    cnt_start = jnp.cumsum(counts) - counts
    dest_sorted = (pad_start[e_sorted] + jnp.arange(A, dtype=i32) - cnt_start[e_sorted]).astype(i32)
    nblk = -(-(A + E * (MOE_ROWS - 1)) // MOE_ROWS)
    R = nblk * MOE_ROWS
    row_tok = jnp.zeros((R,), i32).at[dest_sorted].set((order // TOP_K_WITHIN).astype(i32))
    row_w = jnp.zeros((R,), f32).at[dest_sorted].set(w_flat[order])
    dest = jnp.zeros((A,), i32).at[order].set(dest_sorted)
    n_used = (pad_end[-1] // MOE_ROWS).astype(i32)
    blk = jnp.arange(nblk, dtype=i32)
    blk_expert = jnp.minimum(jnp.searchsorted(pad_end, blk * MOE_ROWS, side="right"), E - 1).astype(i32)
    blk_expert = jnp.where(blk < n_used, blk_expert, blk_expert[jnp.maximum(n_used - 1, 0)])
    n_used1 = n_used.reshape(1)

    xs = _gather_rows(xn, row_tok, _pick(R, (512, 256, 128)), "moe_gather")
    h = _moe_up(blk_expert, n_used1, xs, w_eg, w_eu)
    y_rows = _moe_down(blk_expert, n_used1, h, w_ed, row_w.reshape(R, 1))
    return y_rows, dest


def _alibi_slopes(G, hg):
    h = np.arange(1, G * hg + 1, dtype=np.float32)
    return (2.0 ** (-8.0 * h / (G * hg))).astype(np.float32).reshape(G, hg)


def _overlap(n_cmp, n_slc):
    cs = np.arange(n_cmp)[:, None] * CMP_STRIDE
    ss = np.arange(n_slc)[None, :] * SLC_BLOCK
    ov = np.clip(np.minimum(cs + CMP_BLOCK, ss + SLC_BLOCK) - np.maximum(cs, ss), 0, None)
    return ov.astype(np.float32) / np.float32(CMP_BLOCK)


def _pad2(a, rows, cols):
    out = np.zeros((rows, cols), np.float32)
    out[:a.shape[0], :a.shape[1]] = a
    return out


def kernel(x_prompt, x_sample, cache_cmp_kv, cache_slc_kv, cache_win_kv, state_mlstm_C, state_mlstm_n,
           state_mlstm_m, page_table, g_norm_mix, w_in, b_in, cmp_pos, w_cmp1, b_cmp1, w_cmp2, g_mlstm_norm,
           w_branch_a, w_branch_b, w_out, g_norm_ffn, w_router_group, b_router_group, w_router_expert,
           b_router_expert, w_exp_gate, w_exp_up, w_exp_down, g_norm_final):
    B, T, D = x_prompt.shape
    DB, TS, _ = x_sample.shape
    depth = w_in.shape[0]
    assert depth == 1, "single layer only"
    G, hd = cache_cmp_kv.shape[4], cache_cmp_kv.shape[5]
    assert hd == HEAD_DIM and cache_cmp_kv.shape[2] == PAGE_SIZE and SLC_BLOCK == 64
    NH = w_branch_a.shape[1] // HEAD_DIM
    hg = NH // G
    H, dqk, dv = state_mlstm_C.shape[2:]
    NP = page_table.shape[1]
    past = NP * PAGE_SIZE
    win_buf = cache_win_kv.shape[2]
    F_cmp = w_cmp1.shape[-1]
    tp, sp = SAMPLE_TOK_PAD, SAMPLE_PAD
    assert T % Q_BLOCK == 0 and T >= win_buf and TS <= tp and TS & (TS - 1) == 0
    assert 2 * G == 8 and (hg * tp) & (hg * tp - 1) == 0, "sample kernels: one token's k/v rows fill one sublane tile"
    assert TS <= SAMPLE_NEW_ROWS and past % SLC_BLOCK == 0
    n_cmp_s = (past + TS - CMP_BLOCK) // CMP_STRIDE + 1
    assert n_cmp_s == past // CMP_STRIDE - 1, "sample compression must not reach the new tokens"
    nslc_s = -(-(past + TS) // SLC_BLOCK)
    assert nslc_s == past // SLC_BLOCK + 1 and nslc_s <= 128 and -(-T // SLC_BLOCK) <= 128

    n_prompt, n_sample = B * T, DB * TS
    M0 = n_prompt + n_sample
    M = -(-M0 // ROW_ALIGN) * ROW_ALIGN
    xa = jnp.concatenate([x_prompt.reshape(n_prompt, D), x_sample.reshape(n_sample, D),
                          jnp.zeros((M - M0, D), f32)], axis=0)

    sizes = (NH * HEAD_DIM, 6 * G * HEAD_DIM, 3 * NH, H * dqk, H * dqk, H * dv, 2 * H, H * dv, 2 * D)
    offs = np.concatenate([[0], np.cumsum(sizes)])
    seg = lambda a, i, j=None: a[..., int(offs[i]):int(offs[(i if j is None else j) + 1])]
    w0, b0 = w_in[0], b_in[0]
    ngate = sizes[2] + sizes[6]
    assert ngate <= 128
    w_g = jnp.concatenate([seg(w0, 2), seg(w0, 6), jnp.zeros((D, 128 - ngate), f32)], axis=1).astype(bf16)
    b_g = jnp.concatenate([seg(b0, 2), seg(b0, 6), jnp.zeros((128 - ngate,), f32)])
    xn = _rmsnorm(xa, g_norm_mix[0], bf16)
    zA = _mm_bias(xn, seg(w0, 0, 1).astype(bf16), seg(b0, 0, 1), f32, "in_proj_nsa")
    zB = _mm_bias(xn, seg(w0, 3, 5).astype(bf16), seg(b0, 3, 5), f32, "in_proj_mlstm")
    zC = _mm_bias(xn, seg(w0, 7, 8).astype(bf16), seg(b0, 7, 8), bf16, "in_proj_gates")
    zG = _mm_bias(xn, w_g, b_g, f32, "in_proj_small")
    qblocks = NH

    w1c = (w_cmp1[0].reshape(2, CMP_STRIDE, 2, HEAD_DIM, F_cmp).transpose(2, 1, 3, 0, 4)
           .reshape(2, CMP_STRIDE * HEAD_DIM, 2 * F_cmp).astype(bf16))
    pos_bias = (b_cmp1[0] + jnp.einsum("lcd,lcdf->cf", cmp_pos[0], w_cmp1[0],
                                       precision=lax.Precision.HIGHEST)).reshape(2, 1, F_cmp)
    w2c = w_cmp2[0].astype(bf16)
    kvc = _compress_prompt(zA, w1c, pos_bias, w2c, B, T, G, qblocks)
    ga_all = zG[:, :3 * NH].reshape(M, 3, G, hg)
    gates_p = jnp.pad(ga_all.transpose(2, 0, 1, 3).reshape(G, M, 3 * hg), ((0, 0), (0, 0), (0, 128 - 3 * hg)))
    slopes = _alibi_slopes(G, hg)
    nch_p, nslc_p = T // CMP_STRIDE, -(-T // SLC_BLOCK)
    ov_t = jnp.asarray(_pad2(_overlap(nch_p - 1, nslc_p).T, -(-nslc_p // 8) * 8, nch_p), bf16)
    h_a_p = _nsa_prompt(zA, kvc, gates_p, jnp.asarray(slopes.reshape(-1)), ov_t, B, T, G, hg, qblocks)

    R_s = G * hg * tp
    zA_s = zA[n_prompt:M0]
    q_s = zA_s[:, :NH * HEAD_DIM].reshape(DB, TS, G, hg, HEAD_DIM).transpose(0, 2, 3, 1, 4)
    q_s = jnp.pad(q_s, ((0, 0), (0, 0), (0, 0), (0, tp - TS), (0, 0))).reshape(DB, R_s, HEAD_DIM)
    kvn = jnp.pad(zA_s[:, NH * HEAD_DIM:].reshape(DB, TS, 6 * G * HEAD_DIM),
                  ((0, 0), (0, SAMPLE_NEW_ROWS - TS), (0, 0)))
    gates_s = ga_all[n_prompt:M0].reshape(DB, TS, 3, G, hg).transpose(0, 3, 4, 1, 2)
    gates_s = jnp.pad(gates_s, ((0, 0), (0, 0), (0, 0), (0, tp - TS), (0, 125))).reshape(DB, R_s, 128)
    slope_l = jnp.asarray(np.repeat(slopes.reshape(-1), tp)[None, :])
    lane_g, lane_t = np.arange(R_s) // (hg * tp), np.arange(R_s) % tp
    rsum = jnp.asarray((lane_g[:, None] == lane_g[None, :]) & (lane_t[:, None] == lane_t[None, :]), bf16)
    nch_s = past // CMP_STRIDE
    ovr = jnp.asarray(np.repeat(_pad2(_overlap(n_cmp_s, nslc_s), nch_s, -(-nslc_s // 8) * 8), G, axis=0), bf16)
    pt_flat = page_table.reshape(-1).astype(i32)
    as_rows = lambda c: c.reshape(-1, HEAD_DIM)
    o_cmp_s, sel_s = _nsa_sample_cmp(pt_flat, as_rows(cache_cmp_kv), w1c, pos_bias, w2c, q_s, slope_l, ovr, rsum,
                                     DB, NP, G, hg, past, nslc_s)
    h_a_s = _nsa_sample_attn(pt_flat, as_rows(cache_slc_kv), as_rows(cache_win_kv), q_s, sel_s, kvn, gates_s,
                             slope_l, o_cmp_s, DB, NP, G, hg, past, TS)
    h_a_s = (h_a_s.reshape(DB, G, hg, tp, HEAD_DIM)[:, :, :, :TS].transpose(0, 3, 1, 2, 4)
             .reshape(n_sample, NH * HEAD_DIM).astype(bf16))
    h_a = jnp.concatenate([h_a_p, h_a_s, jnp.zeros((M - M0, NH * HEAD_DIM), bf16)], axis=0)

    if_all = zG[:, 3 * NH:3 * NH + 2 * H].reshape(M, 2, H)
    gnorm = g_mlstm_norm[0].reshape(H, 1, dv).astype(f32)
    L = math.gcd(T, MLSTM_CHUNK)
    NC = T // L
    if_p = if_all[:n_prompt].reshape(B * NC, L, 2, H)
    h_b_p, C_p, n_p, m_p = _mlstm(
        zB, 0, H, (2 * H * dqk) // dv, zC, 0, if_p.transpose(3, 0, 1, 2), if_p.transpose(3, 0, 2, 1),
        jnp.zeros((B, H, dqk, dv), f32), jnp.zeros((B, H, 1, dqk), f32), jnp.zeros((B, H, 1, 1), f32),
        gnorm, B, NC, L, H, dqk, dv)
    padrows = lambda a: jnp.pad(a.reshape(DB, TS, -1), ((0, 0), (0, sp - TS), (0, 0))).reshape(DB * sp, -1)
    if_s = if_all[n_prompt:M0].reshape(DB, TS, 2, H)
    pad_gate = jnp.broadcast_to(jnp.asarray([NEG_INF, 1e4], f32)[None, None, :, None], (DB, sp - TS, 2, H))
    if_s = jnp.concatenate([if_s, pad_gate], axis=1)
    h_b_s, C_s, n_s, m_s = _mlstm(
        padrows(zB[n_prompt:M0]), 0, H, (2 * H * dqk) // dv, padrows(zC[n_prompt:M0, :H * dv]), 0,
        if_s.transpose(3, 0, 1, 2), if_s.transpose(3, 0, 2, 1),
        state_mlstm_C[0], state_mlstm_n[0].reshape(DB, H, 1, dqk), state_mlstm_m[0].reshape(DB, H, 1, 1),
        gnorm, DB, 1, sp, H, dqk, dv)
    h_b = jnp.concatenate([h_b_p, h_b_s.reshape(DB, sp, H * dv)[:, :TS].reshape(n_sample, H * dv),
                           jnp.zeros((M - M0, H * dv), bf16)], axis=0)

    mix = _merge(h_a, w_branch_a[0].astype(bf16), h_b, w_branch_b[0].astype(bf16), zC, H * dv)
    x2 = _mm_residual(mix, w_out[0].astype(bf16), xa)
    y_rows, dest = _moe(x2, g_norm_ffn[0], w_router_group[0], b_router_group[0], w_router_expert[0],
                        b_router_expert[0], w_exp_gate[0], w_exp_up[0], w_exp_down[0])
    y = _final(x2, y_rows, dest, g_norm_final)

    kv_shape = (2, G, HEAD_DIM)
    kv_p = zA[:n_prompt, NH * HEAD_DIM:].reshape(B, T, 3, *kv_shape)
    kv_s = zA_s[:, NH * HEAD_DIM:].reshape(DB, TS, 3, *kv_shape)
    win_s = jnp.concatenate([cache_win_kv[0], kv_s[:, :, 2]], axis=1)[:, -win_buf:]
    return (y[:n_prompt].reshape(B, T, D), y[n_prompt:M0].reshape(DB, TS, D),
            kv_p[:, :, 0][None], kv_s[:, :, 0][None], kv_p[:, :, 1][None], kv_s[:, :, 1][None],
            kv_p[:, -win_buf:, 2][None], win_s[None],
            C_p[None], C_s[None], n_p.reshape(1, B, H, dqk), n_s.reshape(1, DB, H, dqk),
            m_p.reshape(1, B, H), m_s.reshape(1, DB, H))
```

```python
import functools
import math

import numpy as np
import jax
import jax.numpy as jnp
from jax import lax
from jax.experimental import pallas as pl
from jax.experimental.pallas import tpu as pltpu

f32 = jnp.float32
bf16 = jnp.bfloat16
i32 = jnp.int32

HEAD_DIM = 128
PAGE_SIZE = 128
CMP_BLOCK = 32
CMP_STRIDE = 16
SLC_BLOCK = 64
N_SELECT = 16
WINDOW = 512
Q_BLOCK = 128
TOP_K_WITHIN = 2
RMS_EPS = 1e-6
NEG_INF = -1e30
FORCE_BONUS = 1e4

ROW_ALIGN = 256
MOE_ROWS = 256
MLSTM_CHUNK = 256
SAMPLE_PAD = 16
SAMPLE_TOK_PAD = 8
SAMPLE_NEW_ROWS = 16
SAMPLE_NEW_PAD = 128
NSA_KEY_TILE = 512
MIB = 1024 * 1024


def _cp(sem, vmem_mib=None):
    kw = dict(dimension_semantics=sem)
    if vmem_mib is not None:
        kw["vmem_limit_bytes"] = int(vmem_mib * MIB)
    return pltpu.CompilerParams(**kw)


def _pick(n, cands):
    for c in cands:
        if n % c == 0:
            return c
    return n


def _sigmoid(x):
    return 1.0 / (1.0 + jnp.exp(-x))


def _log_sigmoid(x):
    return jnp.minimum(x, 0.0) - jnp.log(1.0 + jnp.exp(-jnp.abs(x)))


def _dot(a, b):
    return jnp.dot(a, b, preferred_element_type=f32)


def _dot_nt(a, b):
    return lax.dot_general(a, b, (((1,), (1,)), ((), ())), preferred_element_type=f32)


def _dot_tn(a, b):
    return lax.dot_general(a, b, (((0,), (0,)), ((), ())), preferred_element_type=f32)


def _dot_hilo(a, b_bf16):
    hi = a.astype(bf16)
    lo = (a - hi.astype(f32)).astype(bf16)
    return _dot(hi, b_bf16) + _dot(lo, b_bf16)


def _masked_softmax(s, mask):
    s = jnp.where(mask, s, NEG_INF)
    p = jnp.where(mask, jnp.exp(s - jnp.max(s, axis=-1, keepdims=True)), 0.0)
    return p / jnp.maximum(jnp.sum(p, axis=-1, keepdims=True), 1.0)


def _rmsnorm_body(x_ref, g_ref, o_ref):
    x = x_ref[...]
    y = x * lax.rsqrt(jnp.mean(x * x, axis=-1, keepdims=True) + RMS_EPS)
    o_ref[...] = (y * g_ref[...]).astype(o_ref.dtype)


def _rmsnorm(x, g, out_dtype):
    M, D = x.shape
    tm = _pick(M, (256, 128, 8))
    return pl.pallas_call(
        _rmsnorm_body, grid=(M // tm,),
        in_specs=[pl.BlockSpec((tm, D), lambda i: (i, 0)), pl.BlockSpec((1, D), lambda i: (0, 0))],
        out_specs=pl.BlockSpec((tm, D), lambda i: (i, 0)),
        out_shape=jax.ShapeDtypeStruct((M, D), out_dtype),
        compiler_params=_cp(("parallel",), 40), name="rmsnorm")(x, g.reshape(1, D).astype(f32))


def _mm_bias_body(x_ref, w_ref, b_ref, o_ref):
    o_ref[...] = (_dot(x_ref[...], w_ref[...]) + b_ref[...]).astype(o_ref.dtype)


def _mm_bias(x, w, b, out_dtype, name):
    M, K = x.shape
    N = w.shape[1]
    tm = _pick(M, (1088, 1024, 768, 512, 256))
    tn = _pick(N, (512, 256, 128))
    return pl.pallas_call(
        _mm_bias_body, grid=(M // tm, N // tn),
        in_specs=[pl.BlockSpec((tm, K), lambda i, j: (i, 0)),
                  pl.BlockSpec((K, tn), lambda i, j: (0, j)),
                  pl.BlockSpec((1, tn), lambda i, j: (0, j))],
        out_specs=pl.BlockSpec((tm, tn), lambda i, j: (i, j)),
        out_shape=jax.ShapeDtypeStruct((M, N), out_dtype),
        compiler_params=_cp(("parallel", "parallel"), 48), name=name)(x, w, b.reshape(1, N).astype(f32))


def _mm_bias_slabs(x, w, b, ws, name):
    M, K = x.shape
    N = w.shape[1]
    tm = _pick(M, (1088, 1024, 768, 512, 256))
    return pl.pallas_call(
        _mm_bias_body, grid=(M // tm, N // ws),
        in_specs=[pl.BlockSpec((tm, K), lambda i, j: (i, 0)),
                  pl.BlockSpec((K, ws), lambda i, j: (0, j)),
                  pl.BlockSpec((1, ws), lambda i, j: (0, j))],
        out_specs=pl.BlockSpec((None, tm, ws), lambda i, j: (j, i, 0)),
        out_shape=jax.ShapeDtypeStruct((N // ws, M, ws), f32),
        compiler_params=_cp(("parallel", "parallel"), 48), name=name)(x, w, b.reshape(1, N).astype(f32))


def _merge_body(ha_ref, wa_ref, hb_ref, wb_ref, ga_ref, gb_ref, o_ref):
    a = _dot(ha_ref[...], wa_ref[...])
    b = _dot(hb_ref[...], wb_ref[...])
    o_ref[...] = (_sigmoid(ga_ref[...].astype(f32)) * a + _sigmoid(gb_ref[...].astype(f32)) * b).astype(o_ref.dtype)


def _merge(h_a, w_a, h_b, w_b, zc, gate_col0):
    M, Ka = h_a.shape
    Kb = h_b.shape[1]
    D = w_a.shape[1]
    tm = _pick(M, (544, 512, 256))
    tn = _pick(D, (512, 256, 128))
    c0 = gate_col0 // tn
    nd = D // tn
    return pl.pallas_call(
        _merge_body, grid=(M // tm, nd),
        in_specs=[pl.BlockSpec((tm, Ka), lambda i, j: (i, 0)),
                  pl.BlockSpec((Ka, tn), lambda i, j: (0, j)),
                  pl.BlockSpec((tm, Kb), lambda i, j: (i, 0)),
                  pl.BlockSpec((Kb, tn), lambda i, j: (0, j)),
                  pl.BlockSpec((tm, tn), lambda i, j: (i, c0 + j)),
                  pl.BlockSpec((tm, tn), lambda i, j: (i, c0 + nd + j))],
        out_specs=pl.BlockSpec((tm, tn), lambda i, j: (i, j)),
        out_shape=jax.ShapeDtypeStruct((M, D), bf16),
        compiler_params=_cp(("parallel", "parallel"), 48), name="merge")(h_a, w_a, h_b, w_b, zc, zc)


def _mm_res_body(x_ref, w_ref, r_ref, o_ref):
    o_ref[...] = r_ref[...] + _dot(x_ref[...], w_ref[...])


def _mm_residual(x, w, res):
    M, K = x.shape
    N = w.shape[1]
    tm = _pick(M, (1088, 1024, 768, 512, 256))
    tn = _pick(N, (512, 256, 128))
    return pl.pallas_call(
        _mm_res_body, grid=(M // tm, N // tn),
        in_specs=[pl.BlockSpec((tm, K), lambda i, j: (i, 0)),
                  pl.BlockSpec((K, tn), lambda i, j: (0, j)),
                  pl.BlockSpec((tm, tn), lambda i, j: (i, j))],
        out_specs=pl.BlockSpec((tm, tn), lambda i, j: (i, j)),
        out_shape=jax.ShapeDtypeStruct((M, N), f32),
        compiler_params=_cp(("parallel", "parallel"), 48), name="out_proj")(x, w, res)


def _compress_body(x_ref, w1_ref, pb_ref, w2_ref, o_ref, xcat_ref, *, nch):
    for l in range(CMP_STRIDE):
        xcat_ref[:, l * HEAD_DIM:(l + 1) * HEAD_DIM] = x_ref[pl.ds(l, nch, stride=CMP_STRIDE), :].astype(bf16)
    parts = _dot(xcat_ref[...], w1_ref[0])
    F = parts.shape[1] // 2
    h = parts[:, :F] + pltpu.roll(parts[:, F:], nch - 1, 0) + pb_ref[0]
    h = jax.nn.gelu(h, approximate=True)
    o_ref[0, 0, 0] = _dot(h.astype(bf16), w2_ref[0])


def _compress_prompt(zA, w1c, pb, w2, B, T, G, qblocks):
    nch = T // CMP_STRIDE
    F2 = w1c.shape[2]
    return pl.pallas_call(
        functools.partial(_compress_body, nch=nch), grid=(B, 2, G),
        in_specs=[pl.BlockSpec((T, HEAD_DIM), lambda b, c, g: (b, qblocks + c * G + g)),
                  pl.BlockSpec((1, CMP_STRIDE * HEAD_DIM, F2), lambda b, c, g: (c, 0, 0)),
                  pl.BlockSpec((1, 1, F2 // 2), lambda b, c, g: (c, 0, 0)),
                  pl.BlockSpec((1, F2 // 2, HEAD_DIM), lambda b, c, g: (c, 0, 0))],
        out_specs=pl.BlockSpec((1, 1, 1, nch, HEAD_DIM), lambda b, c, g: (b, c, g, 0, 0)),
        out_shape=jax.ShapeDtypeStruct((B, 2, G, nch, HEAD_DIM), f32),
        scratch_shapes=[pltpu.VMEM((nch, CMP_STRIDE * HEAD_DIM), bf16)],
        compiler_params=_cp(("parallel", "parallel", "parallel"), 32), name="compress_prompt")(zA, w1c, pb, w2)


def _select_blocks_t(imp_t, cur, nblk, n_sel):
    Bk, W = imp_t.shape
    blk = lax.broadcasted_iota(i32, (Bk, W), 0)
    valid = blk <= cur
    forced = (blk == 0) | (blk == cur) | (blk == cur - 1)
    v = jnp.where(valid, imp_t + jnp.where(forced, FORCE_BONUS, 0.0), NEG_INF)
    cnt = jnp.zeros((Bk, W), f32)
    for i in range(nblk):
        ci = v[i:i + 1, :]
        tie = jnp.where(blk > i, 1.0, 0.0)
        cnt = cnt + jnp.where(ci > v, 1.0, jnp.where(ci == v, tie, 0.0))
    return jnp.where(valid, jnp.where(cnt < n_sel, 1.0, 0.0), 0.0)


def _nsa_prompt_body(slope_ref, q_ref, kc_ref, vc_ref, ks_ref, vs_ref, kw_ref, vw_ref, gate_ref, ovt_ref, o_ref,
                     m_ref, l_ref, acc_ref, *, T, hg, ncp, nslc, n_sel, window, wk, kt):
    g = pl.program_id(1)
    qb = pl.program_id(2)
    QB = Q_BLOCK
    q = q_ref[...] * (HEAD_DIM ** -0.5)
    qs = jnp.concatenate([q[:, h * HEAD_DIM:(h + 1) * HEAD_DIM] for h in range(hg)], axis=0).astype(bf16)
    slopes = [slope_ref[g * hg + h] for h in range(hg)]
    heads = lambda a: [a[h * QB:(h + 1) * QB] for h in range(hg)]
    qpos = lambda w: qb * QB + lax.broadcasted_iota(i32, (QB, w), 0)

    kc = kc_ref[0, 0, 0].astype(bf16)
    vc = vc_ref[0, 0, 0].astype(bf16)
    dist = qpos(ncp) - (lax.broadcasted_iota(i32, (QB, ncp), 1) * CMP_STRIDE + (CMP_BLOCK - 1))
    distf, valid = dist.astype(f32), dist >= 0
    ps = [_masked_softmax(sh - slopes[h] * distf, valid) for h, sh in enumerate(heads(_dot_nt(qs, kc)))]
    o_cmp = jnp.concatenate([_dot(ph.astype(bf16), vc) for ph in ps], axis=0)
    psum = ps[0]
    for h in range(1, hg):
        psum = psum + ps[h]
    hi = psum.astype(bf16)
    lo = (psum - hi.astype(f32)).astype(bf16)
    imp_t = _dot_nt(ovt_ref[...], hi) + _dot_nt(ovt_ref[...], lo)
    nbp = imp_t.shape[0]
    cur = (qb * QB + lax.broadcasted_iota(i32, (nbp, QB), 1)) >> 6
    sel_t = _select_blocks_t(imp_t, cur, nslc, n_sel).astype(bf16)

    m_ref[...] = jnp.full(m_ref.shape, NEG_INF, f32)
    l_ref[...] = jnp.zeros(l_ref.shape, f32)
    acc_ref[...] = jnp.zeros(acc_ref.shape, f32)

    def tile(i, carry):
        k0 = pl.multiple_of(i * kt, kt)
        k = ks_ref[pl.ds(k0, kt), :].astype(bf16)
        v = vs_ref[pl.ds(k0, kt), :].astype(bf16)
        kpos = k0 + lax.broadcasted_iota(i32, (QB, kt), 1)
        d = qpos(kt) - kpos
        e = jnp.where(lax.broadcasted_iota(i32, (nbp, kt), 0)
                      == ((k0 + lax.broadcasted_iota(i32, (nbp, kt), 1)) >> 6), 1.0, 0.0).astype(bf16)
        madd = jnp.where(d >= 0, jnp.where(_dot_tn(sel_t, e) > 0.5, 0.0, NEG_INF), NEG_INF)
        df = d.astype(f32)
        for h, sh in enumerate(heads(_dot_nt(qs, k))):
            rows = slice(h * QB, (h + 1) * QB)
            sc = sh - slopes[h] * df + madd
            m_old = m_ref[rows, :]
            m_new = jnp.maximum(m_old, jnp.max(sc, axis=-1, keepdims=True))
            alpha = jnp.exp(m_old - m_new)
            pt = jnp.exp(sc - m_new)
            l_ref[rows, :] = alpha * l_ref[rows, :] + jnp.sum(pt, axis=-1, keepdims=True)
            acc_ref[rows, :] = alpha * acc_ref[rows, :] + _dot(pt.astype(bf16), v)
            m_ref[rows, :] = m_new
        return carry

    lax.fori_loop(0, (qb * QB + QB + kt - 1) // kt, tile, 0)
    o_slc = acc_ref[...] / jnp.maximum(l_ref[...], 1.0)

    start = pl.multiple_of(jnp.clip(qb * QB - window, 0, T - wk), QB)
    kw = kw_ref[pl.ds(start, wk), :].astype(bf16)
    vw = vw_ref[pl.ds(start, wk), :].astype(bf16)
    dw = qpos(wk) - (start + lax.broadcasted_iota(i32, (QB, wk), 1))
    maddw = jnp.where(dw >= 0, jnp.where(dw < window, 0.0, NEG_INF), NEG_INF)
    dwf = dw.astype(f32)
    o_win = []
    for h, sh in enumerate(heads(_dot_nt(qs, kw))):
        sw = sh - slopes[h] * dwf + maddw
        ew = jnp.exp(sw - jnp.max(sw, axis=-1, keepdims=True))
        pw = ew / jnp.maximum(jnp.sum(ew, axis=-1, keepdims=True), 1.0)
        o_win.append(_dot(pw.astype(bf16), vw))
    o_win = jnp.concatenate(o_win, axis=0)

    ga = _sigmoid(gate_ref[0])
    outs = []
    for h in range(hg):
        sl = slice(h * Q_BLOCK, (h + 1) * Q_BLOCK)
        outs.append(ga[:, h:h + 1] * o_cmp[sl] + ga[:, hg + h:hg + h + 1] * o_slc[sl]
                    + ga[:, 2 * hg + h:2 * hg + h + 1] * o_win[sl])
    o_ref[...] = jnp.concatenate(outs, axis=1).astype(o_ref.dtype)


def _nsa_prompt(zA, kvc, gates, slopes, ov_t, B, T, G, hg, qblocks):
    NQ = T // Q_BLOCK
    ncp = kvc.shape[3]
    nslc = -(-T // SLC_BLOCK)
    wk = min(WINDOW + Q_BLOCK, T)
    kt = math.gcd(T, NSA_KEY_TILE)
    HQ = hg * Q_BLOCK

    def kvspec(br, c):
        return pl.BlockSpec((T, HEAD_DIM), lambda b, g, i: (b, qblocks + (br * 2 + c) * G + g))

    body = functools.partial(_nsa_prompt_body, T=T, hg=hg, ncp=ncp, nslc=nslc, n_sel=min(N_SELECT, nslc),
                             window=WINDOW, wk=wk, kt=kt)
    return pl.pallas_call(
        body, grid=(B, G, NQ),
        in_specs=[pl.BlockSpec(memory_space=pltpu.SMEM),
                  pl.BlockSpec((Q_BLOCK, hg * HEAD_DIM), lambda b, g, i: (b * NQ + i, g)),
                  pl.BlockSpec((1, 1, 1, ncp, HEAD_DIM), lambda b, g, i: (b, 0, g, 0, 0)),
                  pl.BlockSpec((1, 1, 1, ncp, HEAD_DIM), lambda b, g, i: (b, 1, g, 0, 0)),
                  kvspec(1, 0), kvspec(1, 1), kvspec(2, 0), kvspec(2, 1),
                  pl.BlockSpec((1, Q_BLOCK, 128), lambda b, g, i: (g, b * NQ + i, 0)),
                  pl.BlockSpec(ov_t.shape, lambda b, g, i: (0, 0))],
        out_specs=pl.BlockSpec((Q_BLOCK, hg * HEAD_DIM), lambda b, g, i: (b * NQ + i, g)),
        out_shape=jax.ShapeDtypeStruct((B * T, G * hg * HEAD_DIM), bf16),
        scratch_shapes=[pltpu.VMEM((HQ, 1), f32), pltpu.VMEM((HQ, 1), f32), pltpu.VMEM((HQ, HEAD_DIM), f32)],
        compiler_params=_cp(("parallel", "parallel", "arbitrary"), 48), name="nsa_prompt")(
            slopes, zA, kvc, kvc, zA, zA, zA, zA, gates, ov_t)


def _softmax_rows(sc, mask):
    sc = jnp.where(mask, sc, NEG_INF)
    e = jnp.where(mask, jnp.exp(sc - jnp.max(sc, axis=0, keepdims=True)), 0.0)
    return e / jnp.maximum(jnp.sum(e, axis=0, keepdims=True), 1.0)


def _nsa_sample_cmp_body(pt_ref, *refs, NP, G, hg, past, nslc, n_sel):
    pages = refs[:NP]
    (w1k_ref, w1v_ref, pbk_ref, pbv_ref, w2k_ref, w2v_ref, q_ref, slope_ref, ovr_ref, rsum_ref,
     o_ref, sel_ref, xk_ref, xv_ref) = refs[NP:]
    tp = SAMPLE_TOK_PAD
    cpp = PAGE_SIZE // CMP_STRIDE
    NR = NP * cpp * G
    lg, lq = G.bit_length() - 1, (hg * tp).bit_length() - 1
    low = lax.broadcasted_iota(i32, (2 * G, HEAD_DIM), 0) < G
    for k in range(NP):
        for j in range(cpp // 2):
            for l in range(CMP_STRIDE):
                ra = ((2 * j) * CMP_STRIDE + l) * 2 * G
                rb = ((2 * j + 1) * CMP_STRIDE + l) * 2 * G
                a = pages[k][ra:ra + 2 * G, :]
                b = pages[k][rb:rb + 2 * G, :]
                r0 = (k * (cpp // 2) + j) * 2 * G
                xk_ref[r0:r0 + 2 * G, l * HEAD_DIM:(l + 1) * HEAD_DIM] = jnp.where(low, a, pltpu.roll(b, G, 0))
                xv_ref[r0:r0 + 2 * G, l * HEAD_DIM:(l + 1) * HEAD_DIM] = jnp.where(low, pltpu.roll(a, G, 0), b)

    def compress(x_ref, w1_ref, pb_ref, w2_ref):
        parts = _dot(x_ref[...].astype(bf16), w1_ref[...])
        F = parts.shape[1] // 2
        h = parts[:, :F] + pltpu.roll(parts[:, F:], NR - G, 0) + pb_ref[...]
        return _dot(jax.nn.gelu(h, approximate=True).astype(bf16), w2_ref[...]).astype(bf16)

    kc = compress(xk_ref, w1k_ref, pbk_ref, w2k_ref)
    vc = compress(xv_ref, w1v_ref, pbv_ref, w2v_ref)
    qs = (q_ref[0] * (HEAD_DIM ** -0.5)).astype(bf16)
    row = lax.broadcasted_iota(i32, (NR, G * hg * tp), 0)
    lane = lax.broadcasted_iota(i32, (NR, G * hg * tp), 1)
    dist = past + (lane & (tp - 1)) - ((row >> lg) * CMP_STRIDE + (CMP_BLOCK - 1))
    valid = ((row & (G - 1)) == (lane >> lq)) & (dist >= 0)
    p = _softmax_rows(_dot_nt(kc, qs) - slope_ref[...] * dist.astype(f32), valid)
    o_ref[0] = _dot_tn(p.astype(bf16), vc)
    hi = p.astype(bf16)
    lo = (p - hi.astype(f32)).astype(bf16)
    imp = _dot_tn(ovr_ref[...], hi) + _dot_tn(ovr_ref[...], lo)
    imp = _dot_hilo(imp, rsum_ref[...])
    bp = imp.shape[0]
    cur = (past + (lax.broadcasted_iota(i32, (bp, G * hg * tp), 1) & (tp - 1))) >> 6
    sel_ref[0] = _select_blocks_t(imp, cur, nslc, n_sel)


def _page_specs(NP, rows):
    return [pl.BlockSpec((rows, HEAD_DIM), lambda b, pt, k=k: (pt[b * NP + k], 0)) for k in range(NP)]


def _nsa_sample_cmp(pt, pages, w1c, pb, w2, q_s, slope_l, ovr, rsum, DB, NP, G, hg, past, nslc):
    R = q_s.shape[1]
    NR = NP * (PAGE_SIZE // CMP_STRIDE) * G
    bp = ovr.shape[1]
    const2 = lambda b, pt: (0, 0)
    body = functools.partial(_nsa_sample_cmp_body, NP=NP, G=G, hg=hg, past=past, nslc=nslc,
                             n_sel=min(N_SELECT, nslc))
    consts = [w1c[0], w1c[1], pb[0], pb[1], w2[0], w2[1]]
    gs = pltpu.PrefetchScalarGridSpec(
        num_scalar_prefetch=1, grid=(DB,),
        in_specs=_page_specs(NP, PAGE_SIZE * 2 * G) + [pl.BlockSpec(c.shape, const2) for c in consts] + [
            pl.BlockSpec((1, R, HEAD_DIM), lambda b, pt: (b, 0, 0)),
            pl.BlockSpec(slope_l.shape, const2), pl.BlockSpec(ovr.shape, const2), pl.BlockSpec(rsum.shape, const2)],
        out_specs=[pl.BlockSpec((1, R, HEAD_DIM), lambda b, pt: (b, 0, 0)),
                   pl.BlockSpec((1, bp, R), lambda b, pt: (b, 0, 0))],
        scratch_shapes=[pltpu.VMEM((NR, CMP_STRIDE * HEAD_DIM), f32), pltpu.VMEM((NR, CMP_STRIDE * HEAD_DIM), f32)])
    return pl.pallas_call(
        body, grid_spec=gs,
        out_shape=[jax.ShapeDtypeStruct((DB, R, HEAD_DIM), f32), jax.ShapeDtypeStruct((DB, bp, R), f32)],
        compiler_params=_cp(("parallel",), 52), name="nsa_sample_cmp")(
            pt, *([pages] * NP), *consts, q_s, slope_l, ovr, rsum)


def _nsa_sample_attn_body(pt_ref, *refs, NP, G, hg, past, ts, window):
    pages = refs[:NP]
    (win_ref, q_ref, sel_ref, kvn_ref, gate_ref, slope_ref, ocmp_ref, o_ref,
     ks_ref, vs_ref, kw_ref, vw_ref) = refs[NP:]
    tp, npad = SAMPLE_TOK_PAD, SAMPLE_NEW_PAD
    R = G * hg * tp
    lq = (hg * tp).bit_length() - 1
    wb = win_ref.shape[0] // (2 * G)
    kvn = kvn_ref[0]
    nrow = kvn.shape[0]

    def fill(k_ref, v_ref, srcs, rows, br):
        for g in range(G):
            cols = slice(g * HEAD_DIM, (g + 1) * HEAD_DIM)
            for i, src in enumerate(srcs):
                k_ref[i * rows:(i + 1) * rows, cols] = src[pl.ds(g, rows, stride=2 * G), :].astype(bf16)
                v_ref[i * rows:(i + 1) * rows, cols] = src[pl.ds(G + g, rows, stride=2 * G), :].astype(bf16)
            n0 = len(srcs) * rows
            ok, ov = ((br * 2) * G + g) * HEAD_DIM, ((br * 2 + 1) * G + g) * HEAD_DIM
            k_ref[n0:n0 + nrow, cols] = kvn[:, ok:ok + HEAD_DIM].astype(bf16)
            v_ref[n0:n0 + nrow, cols] = kvn[:, ov:ov + HEAD_DIM].astype(bf16)
            k_ref[n0 + nrow:n0 + npad, cols] = jnp.zeros((npad - nrow, HEAD_DIM), bf16)
            v_ref[n0 + nrow:n0 + npad, cols] = jnp.zeros((npad - nrow, HEAD_DIM), bf16)

    fill(ks_ref, vs_ref, pages, PAGE_SIZE, 1)
    fill(kw_ref, vw_ref, [win_ref], wb, 2)

    qs = q_ref[0] * (HEAD_DIM ** -0.5)
    rowg = lax.broadcasted_iota(i32, (R, HEAD_DIM), 0) >> lq
    q_bd = jnp.concatenate([jnp.where(rowg == g, qs, 0.0) for g in range(G)], axis=1).astype(bf16)
    slope = slope_ref[...]

    def attend(k_ref, v_ref, kpos0, mask_fn):
        n = k_ref.shape[0]
        kpos = kpos0 + lax.broadcasted_iota(i32, (n, R), 0)
        d = past + (lax.broadcasted_iota(i32, (n, R), 1) & (tp - 1)) - kpos
        p = _softmax_rows(_dot_nt(k_ref[...], q_bd) - slope * d.astype(f32), mask_fn(kpos, d))
        o_all = _dot_tn(p.astype(bf16), v_ref[...])
        o = jnp.where(rowg == 0, o_all[:, :HEAD_DIM], 0.0)
        for g in range(1, G):
            o = o + jnp.where(rowg == g, o_all[:, g * HEAD_DIM:(g + 1) * HEAD_DIM], 0.0)
        return o

    sel = sel_ref[0]
    nblk = past // SLC_BLOCK
    selk = jnp.concatenate([jnp.broadcast_to(sel[j:j + 1, :], (SLC_BLOCK, R)) for j in range(nblk)]
                           + [jnp.broadcast_to(sel[nblk:nblk + 1, :], (npad, R))], axis=0)
    o_slc = attend(ks_ref, vs_ref, 0,
                   lambda kpos, d: (selk > 0.5) & (d >= 0) & (kpos < past + ts))
    o_win = attend(kw_ref, vw_ref, past - wb,
                   lambda kpos, d: (d >= 0) & (d < window) & (kpos < past + ts))
    ga = _sigmoid(gate_ref[0])
    o_ref[0] = ga[:, 0:1] * ocmp_ref[0] + ga[:, 1:2] * o_slc + ga[:, 2:3] * o_win


def _nsa_sample_attn(pt, pages, win, q_s, sel, kvn, gates_s, slope_l, o_cmp, DB, NP, G, hg, past, ts):
    R = q_s.shape[1]
    wrows = win.shape[0] // DB
    wb = wrows // (2 * G)
    body = functools.partial(_nsa_sample_attn_body, NP=NP, G=G, hg=hg, past=past, ts=ts, window=WINDOW)
    seq3 = lambda b, pt: (b, 0, 0)
    gs = pltpu.PrefetchScalarGridSpec(
        num_scalar_prefetch=1, grid=(DB,),
        in_specs=_page_specs(NP, PAGE_SIZE * 2 * G) + [
            pl.BlockSpec((wrows, HEAD_DIM), lambda b, pt: (b, 0)),
            pl.BlockSpec((1, R, HEAD_DIM), seq3),
            pl.BlockSpec((1, sel.shape[1], R), seq3),
            pl.BlockSpec((1,) + kvn.shape[1:], seq3),
            pl.BlockSpec((1, R, 128), seq3),
            pl.BlockSpec(slope_l.shape, lambda b, pt: (0, 0)),
            pl.BlockSpec((1, R, HEAD_DIM), seq3)],
        out_specs=pl.BlockSpec((1, R, HEAD_DIM), seq3),
        scratch_shapes=[pltpu.VMEM((past + SAMPLE_NEW_PAD, G * HEAD_DIM), bf16),
                        pltpu.VMEM((past + SAMPLE_NEW_PAD, G * HEAD_DIM), bf16),
                        pltpu.VMEM((wb + SAMPLE_NEW_PAD, G * HEAD_DIM), bf16),
                        pltpu.VMEM((wb + SAMPLE_NEW_PAD, G * HEAD_DIM), bf16)])
    return pl.pallas_call(
        body, grid_spec=gs, out_shape=jax.ShapeDtypeStruct((DB, R, HEAD_DIM), f32),
        compiler_params=_cp(("parallel",), 48), name="nsa_sample_attn")(
            pt, *([pages] * NP), win, q_s, sel, kvn, gates_s, slope_l, o_cmp)


def _mlstm_body(q_ref, k_ref, *refs, L, dqk, nv):
    v_refs = refs[:nv]
    (ob_ref, gc_ref, gr_ref, c0_ref, n0_ref, m0_ref, gn_ref,
     h_ref, c_ref, n_ref, mo_ref, cs_ref, ns_ref, ms_ref) = refs[nv:]
    c = pl.program_id(2)
    nc = pl.num_programs(2)

    @pl.when(c == 0)
    def _():
        cs_ref[...] = c0_ref[0, 0]
        ns_ref[...] = n0_ref[0, 0]
        ms_ref[...] = m0_ref[0, 0]

    q = q_ref[...]
    k = k_ref[...] * (dqk ** -0.5)
    v = jnp.concatenate([r[...] for r in v_refs], axis=1)
    gc = gc_ref[0, 0]
    gr = gr_ref[0, 0]
    i_col, f_col = gc[:, 0:1], _log_sigmoid(gc[:, 1:2])
    i_row, f_row = gr[0:1, :], _log_sigmoid(gr[1:2, :])
    r = lax.broadcasted_iota(i32, (L, L), 0)
    s = lax.broadcasted_iota(i32, (L, L), 1)
    causal = r >= s
    b_col = jnp.sum(jnp.where(causal, f_row, 0.0), axis=1, keepdims=True)
    b_row = jnp.sum(jnp.where(r <= s, f_col, 0.0), axis=0, keepdims=True)
    m_prev = ms_ref[...]
    g = b_col + m_prev
    dlog = jnp.where(causal, b_col - b_row + i_row, NEG_INF)
    m_t = jnp.maximum(g, jnp.max(dlog, axis=1, keepdims=True))
    w = jnp.exp(dlog - m_t)
    gw = jnp.exp(g - m_t)
    qb_, kb_, vb_ = q.astype(bf16), k.astype(bf16), v.astype(bf16)
    qk = _dot_nt(qb_, kb_) * w
    num = gw * _dot(qb_, cs_ref[...].astype(bf16)) + _dot(qk.astype(bf16), vb_)
    den = gw * jnp.sum(q * ns_ref[...], axis=1, keepdims=True) + jnp.sum(qk, axis=1, keepdims=True)
    hh = num / jnp.maximum(jnp.abs(den), jnp.exp(-m_t))
    hn = hh * lax.rsqrt(jnp.mean(hh * hh, axis=1, keepdims=True) + RMS_EPS) * gn_ref[0]
    h_ref[...] = (hn * _sigmoid(ob_ref[...].astype(f32))).astype(h_ref.dtype)

    b_last = b_col[L - 1:L, :]
    ws = b_last - b_col + i_col
    m_new = jnp.maximum(b_last + m_prev, jnp.max(ws, axis=0, keepdims=True))
    sw = jnp.exp(ws - m_new)
    cw = jnp.exp(b_last + m_prev - m_new)
    ksw = k * sw
    cs_ref[...] = cw * cs_ref[...] + _dot_tn(ksw.astype(bf16), vb_)
    ns_ref[...] = cw * ns_ref[...] + jnp.sum(ksw, axis=0, keepdims=True)
    ms_ref[...] = m_new

    @pl.when(c == nc - 1)
    def _():
        c_ref[0, 0] = cs_ref[...]
        n_ref[0, 0] = ns_ref[...]
        mo_ref[0, 0] = ms_ref[...]


def _mlstm(z, ob, gcol, grow, c0, n0, m0, gnorm, NS, NC, L, H, dqk, dv):
    R = NS * NC * L
    nv = dv // dqk
    st = lambda b, h, c: (b, h, 0, 0)
    slab = lambda f: pl.BlockSpec((None, L, dqk), lambda b, h, c: (f(h), b * NC + c, 0))
    return pl.pallas_call(
        functools.partial(_mlstm_body, L=L, dqk=dqk, nv=nv), grid=(NS, H, NC),
        in_specs=[slab(lambda h: h), slab(lambda h: H + h)]
                 + [slab(lambda h, j=j: 2 * H + nv * h + j) for j in range(nv)] + [
                  pl.BlockSpec((L, dv), lambda b, h, c: (b * NC + c, h)),
                  pl.BlockSpec((1, 1, L, 2), lambda b, h, c: (h, b * NC + c, 0, 0)),
                  pl.BlockSpec((1, 1, 2, L), lambda b, h, c: (h, b * NC + c, 0, 0)),
                  pl.BlockSpec((1, 1, dqk, dv), st),
                  pl.BlockSpec((1, 1, 1, dqk), st),
                  pl.BlockSpec((1, 1, 1, 1), st),
                  pl.BlockSpec((1, 1, dv), lambda b, h, c: (h, 0, 0))],
        out_specs=[pl.BlockSpec((L, dv), lambda b, h, c: (b * NC + c, h)),
                   pl.BlockSpec((1, 1, dqk, dv), st),
                   pl.BlockSpec((1, 1, 1, dqk), st),
                   pl.BlockSpec((1, 1, 1, 1), st)],
        out_shape=[jax.ShapeDtypeStruct((R, H * dv), bf16),
                   jax.ShapeDtypeStruct((NS, H, dqk, dv), f32),
                   jax.ShapeDtypeStruct((NS, H, 1, dqk), f32),
                   jax.ShapeDtypeStruct((NS, H, 1, 1), f32)],
        scratch_shapes=[pltpu.VMEM((dqk, dv), f32), pltpu.VMEM((1, dqk), f32), pltpu.VMEM((1, 1), f32)],
        compiler_params=_cp(("parallel", "parallel", "arbitrary"), 32), name="mlstm")(
            *([z] * (2 + nv)), ob, gcol, grow, c0, n0, m0, gnorm)


def _router_body(x_ref, g_ref, w_ref, b_ref, xn_ref, e_ref, wt_ref, *, n_groups, epg):
    x = x_ref[...]
    xn = x * lax.rsqrt(jnp.mean(x * x, axis=-1, keepdims=True) + RMS_EPS) * g_ref[...]
    xn_ref[...] = xn.reshape(xn_ref.shape)
    logits = _dot(xn.astype(bf16), w_ref[...]) + b_ref[...]
    R, W = logits.shape
    lane = lax.broadcasted_iota(i32, (R, W), 1)
    big = jnp.int32(W)

    def first_argmax(vals, mask):
        mx = jnp.max(jnp.where(mask, vals, -jnp.inf), axis=-1, keepdims=True)
        idx = jnp.min(jnp.where(mask & (vals == mx), lane, big), axis=-1, keepdims=True)
        return mx, idx

    gmask = lane < n_groups
    gprob = _masked_softmax_plain(logits, gmask)
    g_w, grp = first_argmax(gprob, gmask)
    lo = n_groups + grp * epg
    emask = (lane >= lo) & (lane < lo + epg)
    eprob = _masked_softmax_plain(logits, emask)
    p1, i1 = first_argmax(eprob, emask)
    p2, i2 = first_argmax(eprob, emask & (lane != i1))
    tot = p1 + p2
    e_ref[...] = jnp.where(lane == 0, i1 - n_groups, jnp.where(lane == 1, i2 - n_groups, 0))
    wt_ref[...] = jnp.where(lane == 0, g_w * p1 / tot, jnp.where(lane == 1, g_w * p2 / tot, 0.0))


def _masked_softmax_plain(x, mask):
    mx = jnp.max(jnp.where(mask, x, -jnp.inf), axis=-1, keepdims=True)
    e = jnp.where(mask, jnp.exp(x - mx), 0.0)
    return e / jnp.sum(e, axis=-1, keepdims=True)


def _router(x, g, w_r, b_r, n_groups, epg):
    M, D = x.shape
    tm = _pick(M, (256, 128, 8))
    row = lambda i: (i, 0)
    return pl.pallas_call(
        functools.partial(_router_body, n_groups=n_groups, epg=epg), grid=(M // tm,),
        in_specs=[pl.BlockSpec((tm, D), row), pl.BlockSpec((1, D), lambda i: (0, 0)),
                  pl.BlockSpec((D, 128), lambda i: (0, 0)), pl.BlockSpec((1, 128), lambda i: (0, 0))],
        out_specs=[pl.BlockSpec((tm, 1, D), lambda i: (i, 0, 0)), pl.BlockSpec((tm, 128), row),
                   pl.BlockSpec((tm, 128), row)],
        out_shape=[jax.ShapeDtypeStruct((M, 1, D), f32), jax.ShapeDtypeStruct((M, 128), i32),
                   jax.ShapeDtypeStruct((M, 128), f32)],
        compiler_params=_cp(("parallel",), 40), name="router")(x, g.reshape(1, D).astype(f32), w_r, b_r)


def _gather_rows_body(idx_ref, nu_ref, x_hbm, o_ref, buf_ref, x2d_ref, sem, *, rb):
    b = pl.program_id(0)
    base = b * rb

    @pl.when(b < nu_ref[0])
    def _():
        def issue(r, carry):
            pltpu.make_async_copy(x_hbm.at[idx_ref[base + r]], buf_ref.at[r], sem).start()
            return carry

        lax.fori_loop(0, rb, issue, 0)
        pltpu.make_async_copy(buf_ref, buf_ref, sem).wait()
        x2d_ref[...] = buf_ref[...].reshape(x2d_ref.shape)
        o_ref[...] = x2d_ref[...].astype(o_ref.dtype)

    @pl.when(b >= nu_ref[0])
    def _():
        o_ref[...] = jnp.zeros(o_ref.shape, o_ref.dtype)


def _gather_rows(x, idx, n_used, rb, name):
    R = idx.shape[0]
    D = x.shape[2]
    gs = pltpu.PrefetchScalarGridSpec(
        num_scalar_prefetch=2, grid=(R // rb,),
        in_specs=[pl.BlockSpec(memory_space=pl.ANY)],
        out_specs=pl.BlockSpec((rb, D), lambda b, idx, nu: (b, 0)),
        scratch_shapes=[pltpu.VMEM((rb, 1, D), x.dtype), pltpu.VMEM((rb, D), x.dtype), pltpu.SemaphoreType.DMA(())])
    return pl.pallas_call(
        functools.partial(_gather_rows_body, rb=rb), grid_spec=gs,
        out_shape=jax.ShapeDtypeStruct((R, D), bf16),
        compiler_params=_cp(("arbitrary",), 32), name=name)(idx, n_used, x)


def _moe_up_body(be_ref, nu_ref, x_ref, wg_ref, wu_ref, h_ref, wgb_ref, wub_ref):
    b = pl.program_id(1)
    changed = (b == 0) | (be_ref[b] != be_ref[jnp.maximum(b - 1, 0)])

    @pl.when(changed)
    def _():
        wgb_ref[...] = wg_ref[0].astype(bf16)
        wub_ref[...] = wu_ref[0].astype(bf16)

    @pl.when(b < nu_ref[0])
    def _():
        x = x_ref[...]
        a = _dot(x, wgb_ref[...])
        u = _dot(x, wub_ref[...])
        h_ref[...] = (a * _sigmoid(a) * u).astype(h_ref.dtype)

    @pl.when(b >= nu_ref[0])
    def _():
        h_ref[...] = jnp.zeros(h_ref.shape, h_ref.dtype)


def _moe_up(blk_expert, n_used, xs, w_g, w_u):
    R, D = xs.shape
    F = w_g.shape[2]
    tf = _pick(F, (512, 256, 128))
    nblk = R // MOE_ROWS
    gs = pltpu.PrefetchScalarGridSpec(
        num_scalar_prefetch=2, grid=(F // tf, nblk),
        in_specs=[pl.BlockSpec((MOE_ROWS, D), lambda f, b, be, nu: (jnp.minimum(b, nu[0] - 1), 0)),
                  pl.BlockSpec((1, D, tf), lambda f, b, be, nu: (be[b], 0, f)),
                  pl.BlockSpec((1, D, tf), lambda f, b, be, nu: (be[b], 0, f))],
        out_specs=pl.BlockSpec((MOE_ROWS, tf), lambda f, b, be, nu: (b, f)),
        scratch_shapes=[pltpu.VMEM((D, tf), bf16), pltpu.VMEM((D, tf), bf16)])
    return pl.pallas_call(
        _moe_up_body, grid_spec=gs, out_shape=jax.ShapeDtypeStruct((R, F), bf16),
        compiler_params=_cp(("arbitrary", "arbitrary"), 52), name="moe_up")(blk_expert, n_used, xs, w_g, w_u)


def _moe_down_body(be_ref, nu_ref, h_ref, wd_ref, rw_ref, y_ref, wdb_ref):
    b = pl.program_id(1)
    changed = (b == 0) | (be_ref[b] != be_ref[jnp.maximum(b - 1, 0)])

    @pl.when(changed)
    def _():
        wdb_ref[...] = wd_ref[0].astype(bf16)

    @pl.when(b < nu_ref[0])
    def _():
        y_ref[...] = (_dot(h_ref[...], wdb_ref[...]) * rw_ref[...]).reshape(y_ref.shape)

    @pl.when(b >= nu_ref[0])
    def _():
        y_ref[...] = jnp.zeros(y_ref.shape, y_ref.dtype)


def _moe_down(blk_expert, n_used, h, w_d, row_w):
    R, F = h.shape
    D = w_d.shape[2]
    td = _pick(D, (2048, 1024, 512, 256, 128))
    nblk = R // MOE_ROWS
    gs = pltpu.PrefetchScalarGridSpec(
        num_scalar_prefetch=2, grid=(D // td, nblk),
        in_specs=[pl.BlockSpec((MOE_ROWS, F), lambda d, b, be, nu: (jnp.minimum(b, nu[0] - 1), 0)),
                  pl.BlockSpec((1, F, td), lambda d, b, be, nu: (be[b], 0, d)),
                  pl.BlockSpec((MOE_ROWS, 1), lambda d, b, be, nu: (b, 0))],
        out_specs=pl.BlockSpec((MOE_ROWS, 1, td), lambda d, b, be, nu: (b, 0, d)),
        scratch_shapes=[pltpu.VMEM((F, td), bf16)])
    return pl.pallas_call(
        _moe_down_body, grid_spec=gs, out_shape=jax.ShapeDtypeStruct((R, 1, D), f32),
        compiler_params=_cp(("arbitrary", "arbitrary"), 40), name="moe_down")(blk_expert, n_used, h, w_d, row_w)


def _final_body(dest_ref, x_ref, y_hbm, g_ref, o_ref, ybuf_ref, y2d_ref, sem, *, tm, topk):
    base = pl.program_id(0) * (tm * topk)

    def issue(t, carry):
        for k in range(topk):
            pltpu.make_async_copy(y_hbm.at[dest_ref[base + t * topk + k]], ybuf_ref.at[k * tm + t], sem).start()
        return carry

    lax.fori_loop(0, tm, issue, 0)
    pltpu.make_async_copy(ybuf_ref, ybuf_ref, sem).wait()
    y2d_ref[...] = ybuf_ref[...].reshape(y2d_ref.shape)
    x = x_ref[...]
    for k in range(topk):
        x = x + y2d_ref[k * tm:(k + 1) * tm, :]
    y = x * lax.rsqrt(jnp.mean(x * x, axis=-1, keepdims=True) + RMS_EPS)
    o_ref[...] = y * g_ref[...]


def _final(x, y_rows, dest, g):
    M, D = x.shape
    tm = _pick(M, (128, 8))
    topk = dest.shape[0] // M
    gs = pltpu.PrefetchScalarGridSpec(
        num_scalar_prefetch=1, grid=(M // tm,),
        in_specs=[pl.BlockSpec((tm, D), lambda i, d: (i, 0)), pl.BlockSpec(memory_space=pl.ANY),
                  pl.BlockSpec((1, D), lambda i, d: (0, 0))],
        out_specs=pl.BlockSpec((tm, D), lambda i, d: (i, 0)),
        scratch_shapes=[pltpu.VMEM((topk * tm, 1, D), f32), pltpu.VMEM((topk * tm, D), f32),
                        pltpu.SemaphoreType.DMA(())])
    return pl.pallas_call(
        functools.partial(_final_body, tm=tm, topk=topk), grid_spec=gs,
        out_shape=jax.ShapeDtypeStruct((M, D), f32),
        compiler_params=_cp(("arbitrary",), 40), name="final_norm")(dest, x, y_rows, g.reshape(1, D).astype(f32))


def _moe(x2, g_ffn, w_rg, b_rg, w_re, b_re, w_eg, w_eu, w_ed):
    M, D = x2.shape
    n_groups = w_rg.shape[1]
    E = w_re.shape[1]
    epg = E // n_groups
    w_r = jnp.zeros((D, 128), f32).at[:, :n_groups].set(w_rg).at[:, n_groups:n_groups + E].set(w_re).astype(bf16)
    b_r = jnp.zeros((1, 128), f32).at[0, :n_groups].set(b_rg).at[0, n_groups:n_groups + E].set(b_re)
    xn, e_out, w_out = _router(x2, g_ffn, w_r, b_r, n_groups, epg)

    A = M * TOP_K_WITHIN
    e_flat = e_out[:, :TOP_K_WITHIN].reshape(A)
    w_flat = w_out[:, :TOP_K_WITHIN].reshape(A)
    order = jnp.argsort(e_flat)
    e_sorted = e_flat[order]
    counts = jnp.bincount(e_flat, length=E).astype(i32)
    padded = (counts + MOE_ROWS - 1) // MOE_ROWS * MOE_ROWS
    pad_end = jnp.cumsum(padded)
    pad_start = pad_end - padded
    cnt_start = jnp.cumsum(counts) - counts
    dest_sorted = (pad_start[e_sorted] + jnp.arange(A, dtype=i32) - cnt_start[e_sorted]).astype(i32)
    nblk = -(-(A + E * (MOE_ROWS - 1)) // MOE_ROWS)
    R = nblk * MOE_ROWS
    row_tok = jnp.zeros((R,), i32).at[dest_sorted].set((order // TOP_K_WITHIN).astype(i32))
    row_w = jnp.zeros((R,), f32).at[dest_sorted].set(w_flat[order])
    dest = jnp.zeros((A,), i32).at[order].set(dest_sorted)
    n_used = (pad_end[-1] // MOE_ROWS).astype(i32)
    blk = jnp.arange(nblk, dtype=i32)
    blk_expert = jnp.minimum(jnp.searchsorted(pad_end, blk * MOE_ROWS, side="right"), E - 1).astype(i32)
    blk_expert = jnp.where(blk < n_used, blk_expert, blk_expert[jnp.maximum(n_used - 1, 0)])
    n_used1 = n_used.reshape(1)

    xs = _gather_rows(xn, row_tok, n_used1, MOE_ROWS, "moe_gather")
    h = _moe_up(blk_expert, n_used1, xs, w_eg, w_eu)
    y_rows = _moe_down(blk_expert, n_used1, h, w_ed, row_w.reshape(R, 1))
    return y_rows, dest


def _alibi_slopes(G, hg):
    h = np.arange(1, G * hg + 1, dtype=np.float32)
    return (2.0 ** (-8.0 * h / (G * hg))).astype(np.float32).reshape(G, hg)


def _overlap(n_cmp, n_slc):
    cs = np.arange(n_cmp)[:, None] * CMP_STRIDE
    ss = np.arange(n_slc)[None, :] * SLC_BLOCK
    ov = np.clip(np.minimum(cs + CMP_BLOCK, ss + SLC_BLOCK) - np.maximum(cs, ss), 0, None)
    return ov.astype(np.float32) / np.float32(CMP_BLOCK)


def _pad2(a, rows, cols):
    out = np.zeros((rows, cols), np.float32)
    out[:a.shape[0], :a.shape[1]] = a
    return out


def kernel(x_prompt, x_sample, cache_cmp_kv, cache_slc_kv, cache_win_kv, state_mlstm_C, state_mlstm_n,
           state_mlstm_m, page_table, g_norm_mix, w_in, b_in, cmp_pos, w_cmp1, b_cmp1, w_cmp2, g_mlstm_norm,
           w_branch_a, w_branch_b, w_out, g_norm_ffn, w_router_group, b_router_group, w_router_expert,
           b_router_expert, w_exp_gate, w_exp_up, w_exp_down, g_norm_final):
    B, T, D = x_prompt.shape
    DB, TS, _ = x_sample.shape
    depth = w_in.shape[0]
    assert depth == 1, "single layer only"
    G, hd = cache_cmp_kv.shape[4], cache_cmp_kv.shape[5]
    assert hd == HEAD_DIM and cache_cmp_kv.shape[2] == PAGE_SIZE and SLC_BLOCK == 64
    NH = w_branch_a.shape[1] // HEAD_DIM
    hg = NH // G
    H, dqk, dv = state_mlstm_C.shape[2:]
    NP = page_table.shape[1]
    past = NP * PAGE_SIZE
    win_buf = cache_win_kv.shape[2]
    F_cmp = w_cmp1.shape[-1]
    tp, sp = SAMPLE_TOK_PAD, SAMPLE_PAD
    assert T % Q_BLOCK == 0 and T >= win_buf and TS <= tp and TS & (TS - 1) == 0
    assert 2 * G == 8 and (hg * tp) & (hg * tp - 1) == 0, "sample kernels: one token's k/v rows fill one sublane tile"
    assert TS <= SAMPLE_NEW_ROWS and past % SLC_BLOCK == 0
    n_cmp_s = (past + TS - CMP_BLOCK) // CMP_STRIDE + 1
    assert n_cmp_s == past // CMP_STRIDE - 1, "sample compression must not reach the new tokens"
    nslc_s = -(-(past + TS) // SLC_BLOCK)
    assert nslc_s == past // SLC_BLOCK + 1 and nslc_s <= 128 and -(-T // SLC_BLOCK) <= 128

    n_prompt, n_sample = B * T, DB * TS
    M0 = n_prompt + n_sample
    M = -(-M0 // ROW_ALIGN) * ROW_ALIGN
    xa = jnp.concatenate([x_prompt.reshape(n_prompt, D), x_sample.reshape(n_sample, D),
                          jnp.zeros((M - M0, D), f32)], axis=0)

    sizes = (NH * HEAD_DIM, 6 * G * HEAD_DIM, 3 * NH, H * dqk, H * dqk, H * dv, 2 * H, H * dv, 2 * D)
    offs = np.concatenate([[0], np.cumsum(sizes)])
    seg = lambda a, i, j=None: a[..., int(offs[i]):int(offs[(i if j is None else j) + 1])]
    w0, b0 = w_in[0], b_in[0]
    ngate = sizes[2] + sizes[6]
    assert ngate <= 128
    w_g = jnp.concatenate([seg(w0, 2), seg(w0, 6), jnp.zeros((D, 128 - ngate), f32)], axis=1).astype(bf16)
    b_g = jnp.concatenate([seg(b0, 2), seg(b0, 6), jnp.zeros((128 - ngate,), f32)])
    xn = _rmsnorm(xa, g_norm_mix[0], bf16)
    zA = _mm_bias(xn, seg(w0, 0, 1).astype(bf16), seg(b0, 0, 1), f32, "in_proj_nsa")
    assert dv % dqk == 0
    zB = _mm_bias_slabs(xn, seg(w0, 3, 5).astype(bf16), seg(b0, 3, 5), dqk, "in_proj_mlstm")
    zC = _mm_bias(xn, seg(w0, 7, 8).astype(bf16), seg(b0, 7, 8), bf16, "in_proj_gates")
    zG = _mm_bias(xn, w_g, b_g, f32, "in_proj_small")
    qblocks = NH

    w1c = (w_cmp1[0].reshape(2, CMP_STRIDE, 2, HEAD_DIM, F_cmp).transpose(2, 1, 3, 0, 4)
           .reshape(2, CMP_STRIDE * HEAD_DIM, 2 * F_cmp).astype(bf16))
    pos_bias = (b_cmp1[0] + jnp.einsum("lcd,lcdf->cf", cmp_pos[0], w_cmp1[0],
                                       precision=lax.Precision.HIGHEST)).reshape(2, 1, F_cmp)
    w2c = w_cmp2[0].astype(bf16)
    kvc = _compress_prompt(zA, w1c, pos_bias, w2c, B, T, G, qblocks)
    ga_all = zG[:, :3 * NH].reshape(M, 3, G, hg)
    gates_p = jnp.pad(ga_all.transpose(2, 0, 1, 3).reshape(G, M, 3 * hg), ((0, 0), (0, 0), (0, 128 - 3 * hg)))
    slopes = _alibi_slopes(G, hg)
    nch_p, nslc_p = T // CMP_STRIDE, -(-T // SLC_BLOCK)
    ov_t = jnp.asarray(_pad2(_overlap(nch_p - 1, nslc_p).T, -(-nslc_p // 8) * 8, nch_p), bf16)
    h_a_p = _nsa_prompt(zA, kvc, gates_p, jnp.asarray(slopes.reshape(-1)), ov_t, B, T, G, hg, qblocks)

    R_s = G * hg * tp
    zA_s = zA[n_prompt:M0]
    q_s = zA_s[:, :NH * HEAD_DIM].reshape(DB, TS, G, hg, HEAD_DIM).transpose(0, 2, 3, 1, 4)
    q_s = jnp.pad(q_s, ((0, 0), (0, 0), (0, 0), (0, tp - TS), (0, 0))).reshape(DB, R_s, HEAD_DIM)
    kvn = jnp.pad(zA_s[:, NH * HEAD_DIM:].reshape(DB, TS, 6 * G * HEAD_DIM),
                  ((0, 0), (0, SAMPLE_NEW_ROWS - TS), (0, 0)))
    gates_s = ga_all[n_prompt:M0].reshape(DB, TS, 3, G, hg).transpose(0, 3, 4, 1, 2)
    gates_s = jnp.pad(gates_s, ((0, 0), (0, 0), (0, 0), (0, tp - TS), (0, 125))).reshape(DB, R_s, 128)
    slope_l = jnp.asarray(np.repeat(slopes.reshape(-1), tp)[None, :])
    lane_g, lane_t = np.arange(R_s) // (hg * tp), np.arange(R_s) % tp
    rsum = jnp.asarray((lane_g[:, None] == lane_g[None, :]) & (lane_t[:, None] == lane_t[None, :]), bf16)
    nch_s = past // CMP_STRIDE
    ovr = jnp.asarray(np.repeat(_pad2(_overlap(n_cmp_s, nslc_s), nch_s, -(-nslc_s // 8) * 8), G, axis=0), bf16)
    pt_flat = page_table.reshape(-1).astype(i32)
    as_rows = lambda c: c.reshape(-1, HEAD_DIM)
    o_cmp_s, sel_s = _nsa_sample_cmp(pt_flat, as_rows(cache_cmp_kv), w1c, pos_bias, w2c, q_s, slope_l, ovr, rsum,
                                     DB, NP, G, hg, past, nslc_s)
    h_a_s = _nsa_sample_attn(pt_flat, as_rows(cache_slc_kv), as_rows(cache_win_kv), q_s, sel_s, kvn, gates_s,
                             slope_l, o_cmp_s, DB, NP, G, hg, past, TS)
    h_a_s = (h_a_s.reshape(DB, G, hg, tp, HEAD_DIM)[:, :, :, :TS].transpose(0, 3, 1, 2, 4)
             .reshape(n_sample, NH * HEAD_DIM).astype(bf16))
    h_a = jnp.concatenate([h_a_p, h_a_s, jnp.zeros((M - M0, NH * HEAD_DIM), bf16)], axis=0)

    if_all = zG[:, 3 * NH:3 * NH + 2 * H].reshape(M, 2, H)
    gnorm = g_mlstm_norm[0].reshape(H, 1, dv).astype(f32)
    L = math.gcd(T, MLSTM_CHUNK)
    NC = T // L
    if_p = if_all[:n_prompt].reshape(B * NC, L, 2, H)
    h_b_p, C_p, n_p, m_p = _mlstm(
        zB, zC, if_p.transpose(3, 0, 1, 2), if_p.transpose(3, 0, 2, 1),
        jnp.zeros((B, H, dqk, dv), f32), jnp.zeros((B, H, 1, dqk), f32), jnp.zeros((B, H, 1, 1), f32),
        gnorm, B, NC, L, H, dqk, dv)
    padrows = lambda a: jnp.pad(a.reshape(DB, TS, -1), ((0, 0), (0, sp - TS), (0, 0))).reshape(DB * sp, -1)
    zB_s = jnp.pad(zB[:, n_prompt:M0].reshape(-1, DB, TS, dqk), ((0, 0), (0, 0), (0, sp - TS), (0, 0)))
    if_s = if_all[n_prompt:M0].reshape(DB, TS, 2, H)
    pad_gate = jnp.broadcast_to(jnp.asarray([NEG_INF, 1e4], f32)[None, None, :, None], (DB, sp - TS, 2, H))
    if_s = jnp.concatenate([if_s, pad_gate], axis=1)
    h_b_s, C_s, n_s, m_s = _mlstm(
        zB_s.reshape(-1, DB * sp, dqk), padrows(zC[n_prompt:M0, :H * dv]),
        if_s.transpose(3, 0, 1, 2), if_s.transpose(3, 0, 2, 1),
        state_mlstm_C[0], state_mlstm_n[0].reshape(DB, H, 1, dqk), state_mlstm_m[0].reshape(DB, H, 1, 1),
        gnorm, DB, 1, sp, H, dqk, dv)
    h_b = jnp.concatenate([h_b_p, h_b_s.reshape(DB, sp, H * dv)[:, :TS].reshape(n_sample, H * dv),
                           jnp.zeros((M - M0, H * dv), bf16)], axis=0)

    mix = _merge(h_a, w_branch_a[0].astype(bf16), h_b, w_branch_b[0].astype(bf16), zC, H * dv)
    x2 = _mm_residual(mix, w_out[0].astype(bf16), xa)
    y_rows, dest = _moe(x2, g_norm_ffn[0], w_router_group[0], b_router_group[0], w_router_expert[0],
                        b_router_expert[0], w_exp_gate[0], w_exp_up[0], w_exp_down[0])
    y = _final(x2, y_rows, dest, g_norm_final)

    kv_shape = (2, G, HEAD_DIM)
    kv_p = zA[:n_prompt, NH * HEAD_DIM:].reshape(B, T, 3, *kv_shape)
    kv_s = zA_s[:, NH * HEAD_DIM:].reshape(DB, TS, 3, *kv_shape)
    win_s = jnp.concatenate([cache_win_kv[0], kv_s[:, :, 2]], axis=1)[:, -win_buf:]
    return (y[:n_prompt].reshape(B, T, D), y[n_prompt:M0].reshape(DB, TS, D),
            kv_p[:, :, 0][None], kv_s[:, :, 0][None], kv_p[:, :, 1][None], kv_s[:, :, 1][None],
            kv_p[:, -win_buf:, 2][None], win_s[None],
            C_p[None], C_s[None], n_p.reshape(1, B, H, dqk), n_s.reshape(1, DB, H, dqk),
            m_p.reshape(1, B, H), m_s.reshape(1, DB, H))
```

```python
import functools
import math

import numpy as np
import jax
import jax.numpy as jnp
from jax import lax
from jax.experimental import pallas as pl
from jax.experimental.pallas import tpu as pltpu

f32 = jnp.float32
bf16 = jnp.bfloat16
i32 = jnp.int32

HEAD_DIM = 128
PAGE_SIZE = 128
CMP_BLOCK = 32
CMP_STRIDE = 16
SLC_BLOCK = 64
N_SELECT = 16
WINDOW = 512
Q_BLOCK = 128
TOP_K_WITHIN = 2
RMS_EPS = 1e-6
NEG_INF = -1e30
FORCE_BONUS = 1e4

ROW_ALIGN = 256
MOE_ROWS = 256
MLSTM_CHUNK = 256
SAMPLE_PAD = 16
SAMPLE_TOK_PAD = 8
SAMPLE_NEW_ROWS = 16
SAMPLE_NEW_PAD = 128
NSA_KEY_TILE = 512
MIB = 1024 * 1024


def _cp(sem, vmem_mib=None):
    kw = dict(dimension_semantics=sem)
    if vmem_mib is not None:
        kw["vmem_limit_bytes"] = int(vmem_mib * MIB)
    return pltpu.CompilerParams(**kw)


def _pick(n, cands):
    for c in cands:
        if n % c == 0:
            return c
    return n


def _sigmoid(x):
    return 1.0 / (1.0 + jnp.exp(-x))


def _log_sigmoid(x):
    return jnp.minimum(x, 0.0) - jnp.log(1.0 + jnp.exp(-jnp.abs(x)))


def _dot(a, b):
    return jnp.dot(a, b, preferred_element_type=f32)


def _dot_nt(a, b):
    return lax.dot_general(a, b, (((1,), (1,)), ((), ())), preferred_element_type=f32)


def _dot_tn(a, b):
    return lax.dot_general(a, b, (((0,), (0,)), ((), ())), preferred_element_type=f32)


def _dot_hilo(a, b_bf16):
    hi = a.astype(bf16)
    lo = (a - hi.astype(f32)).astype(bf16)
    return _dot(hi, b_bf16) + _dot(lo, b_bf16)


def _masked_softmax(s, mask):
    s = jnp.where(mask, s, NEG_INF)
    p = jnp.where(mask, jnp.exp(s - jnp.max(s, axis=-1, keepdims=True)), 0.0)
    return p / jnp.maximum(jnp.sum(p, axis=-1, keepdims=True), 1.0)


def _rmsnorm_body(x_ref, g_ref, o_ref):
    x = x_ref[...]
    y = x * lax.rsqrt(jnp.mean(x * x, axis=-1, keepdims=True) + RMS_EPS)
    o_ref[...] = (y * g_ref[...]).astype(o_ref.dtype)


def _rmsnorm(x, g, out_dtype):
    M, D = x.shape
    tm = _pick(M, (256, 128, 8))
    return pl.pallas_call(
        _rmsnorm_body, grid=(M // tm,),
        in_specs=[pl.BlockSpec((tm, D), lambda i: (i, 0)), pl.BlockSpec((1, D), lambda i: (0, 0))],
        out_specs=pl.BlockSpec((tm, D), lambda i: (i, 0)),
        out_shape=jax.ShapeDtypeStruct((M, D), out_dtype),
        compiler_params=_cp(("parallel",), 40), name="rmsnorm")(x, g.reshape(1, D).astype(f32))


def _mm_bias_body(x_ref, w_ref, b_ref, o_ref):
    o_ref[...] = (_dot(x_ref[...], w_ref[...]) + b_ref[...]).astype(o_ref.dtype)


def _mm_bias(x, w, b, out_dtype, name):
    M, K = x.shape
    N = w.shape[1]
    tm = _pick(M, (1088, 1024, 768, 512, 256))
    tn = _pick(N, (512, 256, 128))
    return pl.pallas_call(
        _mm_bias_body, grid=(M // tm, N // tn),
        in_specs=[pl.BlockSpec((tm, K), lambda i, j: (i, 0)),
                  pl.BlockSpec((K, tn), lambda i, j: (0, j)),
                  pl.BlockSpec((1, tn), lambda i, j: (0, j))],
        out_specs=pl.BlockSpec((tm, tn), lambda i, j: (i, j)),
        out_shape=jax.ShapeDtypeStruct((M, N), out_dtype),
        compiler_params=_cp(("parallel", "parallel"), 48), name=name)(x, w, b.reshape(1, N).astype(f32))


def _mm_bias_slabs(x, w, b, ws, name):
    M, K = x.shape
    N = w.shape[1]
    tm = _pick(M, (1088, 1024, 768, 512, 256))
    return pl.pallas_call(
        _mm_bias_body, grid=(M // tm, N // ws),
        in_specs=[pl.BlockSpec((tm, K), lambda i, j: (i, 0)),
                  pl.BlockSpec((K, ws), lambda i, j: (0, j)),
                  pl.BlockSpec((1, ws), lambda i, j: (0, j))],
        out_specs=pl.BlockSpec((None, tm, ws), lambda i, j: (j, i, 0)),
        out_shape=jax.ShapeDtypeStruct((N // ws, M, ws), f32),
        compiler_params=_cp(("parallel", "parallel"), 48), name=name)(x, w, b.reshape(1, N).astype(f32))


def _merge_body(ha_ref, wa_ref, hb_ref, wb_ref, ga_ref, gb_ref, o_ref):
    a = _dot(ha_ref[...], wa_ref[...])
    b = _dot(hb_ref[...], wb_ref[...])
    o_ref[...] = (_sigmoid(ga_ref[...].astype(f32)) * a + _sigmoid(gb_ref[...].astype(f32)) * b).astype(o_ref.dtype)


def _merge(h_a, w_a, h_b, w_b, zc, gate_col0):
    M, Ka = h_a.shape
    Kb = h_b.shape[1]
    D = w_a.shape[1]
    tm = _pick(M, (544, 512, 256))
    tn = _pick(D, (512, 256, 128))
    c0 = gate_col0 // tn
    nd = D // tn
    return pl.pallas_call(
        _merge_body, grid=(M // tm, nd),
        in_specs=[pl.BlockSpec((tm, Ka), lambda i, j: (i, 0)),
                  pl.BlockSpec((Ka, tn), lambda i, j: (0, j)),
                  pl.BlockSpec((tm, Kb), lambda i, j: (i, 0)),
                  pl.BlockSpec((Kb, tn), lambda i, j: (0, j)),
                  pl.BlockSpec((tm, tn), lambda i, j: (i, c0 + j)),
                  pl.BlockSpec((tm, tn), lambda i, j: (i, c0 + nd + j))],
        out_specs=pl.BlockSpec((tm, tn), lambda i, j: (i, j)),
        out_shape=jax.ShapeDtypeStruct((M, D), bf16),
        compiler_params=_cp(("parallel", "parallel"), 48), name="merge")(h_a, w_a, h_b, w_b, zc, zc)


def _mm_res_body(x_ref, w_ref, r_ref, o_ref):
    o_ref[...] = r_ref[...] + _dot(x_ref[...], w_ref[...])


def _mm_residual(x, w, res):
    M, K = x.shape
    N = w.shape[1]
    tm = _pick(M, (1088, 1024, 768, 512, 256))
    tn = _pick(N, (512, 256, 128))
    return pl.pallas_call(
        _mm_res_body, grid=(M // tm, N // tn),
        in_specs=[pl.BlockSpec((tm, K), lambda i, j: (i, 0)),
                  pl.BlockSpec((K, tn), lambda i, j: (0, j)),
                  pl.BlockSpec((tm, tn), lambda i, j: (i, j))],
        out_specs=pl.BlockSpec((tm, tn), lambda i, j: (i, j)),
        out_shape=jax.ShapeDtypeStruct((M, N), f32),
        compiler_params=_cp(("parallel", "parallel"), 48), name="out_proj")(x, w, res)


def _compress_body(x_ref, w1_ref, pb_ref, w2_ref, o_ref, xcat_ref, *, nch):
    for l in range(CMP_STRIDE):
        xcat_ref[:, l * HEAD_DIM:(l + 1) * HEAD_DIM] = x_ref[pl.ds(l, nch, stride=CMP_STRIDE), :].astype(bf16)
    parts = _dot(xcat_ref[...], w1_ref[0])
    F = parts.shape[1] // 2
    h = parts[:, :F] + pltpu.roll(parts[:, F:], nch - 1, 0) + pb_ref[0]
    h = jax.nn.gelu(h, approximate=True)
    o_ref[0, 0, 0] = _dot(h.astype(bf16), w2_ref[0])


def _compress_prompt(zA, w1c, pb, w2, B, T, G, qblocks):
    nch = T // CMP_STRIDE
    F2 = w1c.shape[2]
    return pl.pallas_call(
        functools.partial(_compress_body, nch=nch), grid=(B, 2, G),
        in_specs=[pl.BlockSpec((T, HEAD_DIM), lambda b, c, g: (b, qblocks + c * G + g)),
                  pl.BlockSpec((1, CMP_STRIDE * HEAD_DIM, F2), lambda b, c, g: (c, 0, 0)),
                  pl.BlockSpec((1, 1, F2 // 2), lambda b, c, g: (c, 0, 0)),
                  pl.BlockSpec((1, F2 // 2, HEAD_DIM), lambda b, c, g: (c, 0, 0))],
        out_specs=pl.BlockSpec((1, 1, 1, nch, HEAD_DIM), lambda b, c, g: (b, c, g, 0, 0)),
        out_shape=jax.ShapeDtypeStruct((B, 2, G, nch, HEAD_DIM), f32),
        scratch_shapes=[pltpu.VMEM((nch, CMP_STRIDE * HEAD_DIM), bf16)],
        compiler_params=_cp(("parallel", "parallel", "parallel"), 32), name="compress_prompt")(zA, w1c, pb, w2)


def _select_blocks_t(imp_t, cur, nblk, n_sel):
    Bk, W = imp_t.shape
    blk = lax.broadcasted_iota(i32, (Bk, W), 0)
    valid = blk <= cur
    forced = (blk == 0) | (blk == cur) | (blk == cur - 1)
    v = jnp.where(valid, imp_t + jnp.where(forced, FORCE_BONUS, 0.0), NEG_INF)
    cnt = jnp.zeros((Bk, W), f32)
    for i in range(nblk):
        ci = v[i:i + 1, :]
        tie = jnp.where(blk > i, 1.0, 0.0)
        cnt = cnt + jnp.where(ci > v, 1.0, jnp.where(ci == v, tie, 0.0))
    return jnp.where(valid, jnp.where(cnt < n_sel, 1.0, 0.0), 0.0)


def _nsa_prompt_body(slope_ref, q_ref, kc_ref, vc_ref, ks_ref, vs_ref, kw_ref, vw_ref, gate_ref, ovt_ref, o_ref,
                     m_ref, l_ref, acc_ref, *, T, hg, ncp, nslc, n_sel, window, wk, kt):
    g = pl.program_id(1)
    qb = pl.program_id(2)
    QB = Q_BLOCK
    q = q_ref[...] * (HEAD_DIM ** -0.5)
    qs = jnp.concatenate([q[:, h * HEAD_DIM:(h + 1) * HEAD_DIM] for h in range(hg)], axis=0).astype(bf16)
    slopes = [slope_ref[g * hg + h] for h in range(hg)]
    heads = lambda a: [a[h * QB:(h + 1) * QB] for h in range(hg)]
    qpos = lambda w: qb * QB + lax.broadcasted_iota(i32, (QB, w), 0)

    kc = kc_ref[0, 0, 0].astype(bf16)
    vc = vc_ref[0, 0, 0].astype(bf16)
    dist = qpos(ncp) - (lax.broadcasted_iota(i32, (QB, ncp), 1) * CMP_STRIDE + (CMP_BLOCK - 1))
    distf, valid = dist.astype(f32), dist >= 0
    ps = [_masked_softmax(sh - slopes[h] * distf, valid) for h, sh in enumerate(heads(_dot_nt(qs, kc)))]
    o_cmp = jnp.concatenate([_dot(ph.astype(bf16), vc) for ph in ps], axis=0)
    psum = ps[0]
    for h in range(1, hg):
        psum = psum + ps[h]
    hi = psum.astype(bf16)
    lo = (psum - hi.astype(f32)).astype(bf16)
    imp_t = _dot_nt(ovt_ref[...], hi) + _dot_nt(ovt_ref[...], lo)
    nbp = imp_t.shape[0]
    cur = (qb * QB + lax.broadcasted_iota(i32, (nbp, QB), 1)) >> 6
    sel_t = _select_blocks_t(imp_t, cur, nslc, n_sel).astype(bf16)

    m_ref[...] = jnp.full(m_ref.shape, NEG_INF, f32)
    l_ref[...] = jnp.zeros(l_ref.shape, f32)
    acc_ref[...] = jnp.zeros(acc_ref.shape, f32)

    def tile(i, carry):
        k0 = pl.multiple_of(i * kt, kt)
        k = ks_ref[pl.ds(k0, kt), :].astype(bf16)
        v = vs_ref[pl.ds(k0, kt), :].astype(bf16)
        kpos = k0 + lax.broadcasted_iota(i32, (QB, kt), 1)
        d = qpos(kt) - kpos
        e = jnp.where(lax.broadcasted_iota(i32, (nbp, kt), 0)
                      == ((k0 + lax.broadcasted_iota(i32, (nbp, kt), 1)) >> 6), 1.0, 0.0).astype(bf16)
        madd = jnp.where(d >= 0, jnp.where(_dot_tn(sel_t, e) > 0.5, 0.0, NEG_INF), NEG_INF)
        df = d.astype(f32)
        for h, sh in enumerate(heads(_dot_nt(qs, k))):
            rows = slice(h * QB, (h + 1) * QB)
            sc = sh - slopes[h] * df + madd
            m_old = m_ref[rows, :]
            m_new = jnp.maximum(m_old, jnp.max(sc, axis=-1, keepdims=True))
            alpha = jnp.exp(m_old - m_new)
            pt = jnp.exp(sc - m_new)
            l_ref[rows, :] = alpha * l_ref[rows, :] + jnp.sum(pt, axis=-1, keepdims=True)
            acc_ref[rows, :] = alpha * acc_ref[rows, :] + _dot(pt.astype(bf16), v)
            m_ref[rows, :] = m_new
        return carry

    lax.fori_loop(0, (qb * QB + QB + kt - 1) // kt, tile, 0)
    o_slc = acc_ref[...] / jnp.maximum(l_ref[...], 1.0)

    start = pl.multiple_of(jnp.clip(qb * QB - window, 0, T - wk), QB)
    kw = kw_ref[pl.ds(start, wk), :].astype(bf16)
    vw = vw_ref[pl.ds(start, wk), :].astype(bf16)
    dw = qpos(wk) - (start + lax.broadcasted_iota(i32, (QB, wk), 1))
    maddw = jnp.where(dw >= 0, jnp.where(dw < window, 0.0, NEG_INF), NEG_INF)
    dwf = dw.astype(f32)
    o_win = []
    for h, sh in enumerate(heads(_dot_nt(qs, kw))):
        sw = sh - slopes[h] * dwf + maddw
        ew = jnp.exp(sw - jnp.max(sw, axis=-1, keepdims=True))
        pw = ew / jnp.maximum(jnp.sum(ew, axis=-1, keepdims=True), 1.0)
        o_win.append(_dot(pw.astype(bf16), vw))
    o_win = jnp.concatenate(o_win, axis=0)

    ga = _sigmoid(gate_ref[0])
    outs = []
    for h in range(hg):
        sl = slice(h * Q_BLOCK, (h + 1) * Q_BLOCK)
        outs.append(ga[:, h:h + 1] * o_cmp[sl] + ga[:, hg + h:hg + h + 1] * o_slc[sl]
                    + ga[:, 2 * hg + h:2 * hg + h + 1] * o_win[sl])
    o_ref[...] = jnp.concatenate(outs, axis=1).astype(o_ref.dtype)


def _nsa_prompt(zA, kvc, gates, slopes, ov_t, B, T, G, hg, qblocks):
    NQ = T // Q_BLOCK
    ncp = kvc.shape[3]
    nslc = -(-T // SLC_BLOCK)
    wk = min(WINDOW + Q_BLOCK, T)
    kt = math.gcd(T, NSA_KEY_TILE)
    HQ = hg * Q_BLOCK

    def kvspec(br, c):
        return pl.BlockSpec((T, HEAD_DIM), lambda b, g, i: (b, qblocks + (br * 2 + c) * G + g))

    body = functools.partial(_nsa_prompt_body, T=T, hg=hg, ncp=ncp, nslc=nslc, n_sel=min(N_SELECT, nslc),
                             window=WINDOW, wk=wk, kt=kt)
    return pl.pallas_call(
        body, grid=(B, G, NQ),
        in_specs=[pl.BlockSpec(memory_space=pltpu.SMEM),
                  pl.BlockSpec((Q_BLOCK, hg * HEAD_DIM), lambda b, g, i: (b * NQ + i, g)),
                  pl.BlockSpec((1, 1, 1, ncp, HEAD_DIM), lambda b, g, i: (b, 0, g, 0, 0)),
                  pl.BlockSpec((1, 1, 1, ncp, HEAD_DIM), lambda b, g, i: (b, 1, g, 0, 0)),
                  kvspec(1, 0), kvspec(1, 1), kvspec(2, 0), kvspec(2, 1),
                  pl.BlockSpec((1, Q_BLOCK, 128), lambda b, g, i: (g, b * NQ + i, 0)),
                  pl.BlockSpec(ov_t.shape, lambda b, g, i: (0, 0))],
        out_specs=pl.BlockSpec((Q_BLOCK, hg * HEAD_DIM), lambda b, g, i: (b * NQ + i, g)),
        out_shape=jax.ShapeDtypeStruct((B * T, G * hg * HEAD_DIM), bf16),
        scratch_shapes=[pltpu.VMEM((HQ, 1), f32), pltpu.VMEM((HQ, 1), f32), pltpu.VMEM((HQ, HEAD_DIM), f32)],
        compiler_params=_cp(("parallel", "parallel", "arbitrary"), 48), name="nsa_prompt")(
            slopes, zA, kvc, kvc, zA, zA, zA, zA, gates, ov_t)


def _softmax_rows(sc, mask):
    sc = jnp.where(mask, sc, NEG_INF)
    e = jnp.where(mask, jnp.exp(sc - jnp.max(sc, axis=0, keepdims=True)), 0.0)
    return e / jnp.maximum(jnp.sum(e, axis=0, keepdims=True), 1.0)


def _nsa_sample_cmp_body(pt_ref, *refs, NP, G, hg, past, nslc, n_sel):
    pages = refs[:NP]
    (w1k_ref, w1v_ref, pbk_ref, pbv_ref, w2k_ref, w2v_ref, q_ref, slope_ref, ovr_ref, rsum_ref,
     o_ref, sel_ref, xk_ref, xv_ref) = refs[NP:]
    tp = SAMPLE_TOK_PAD
    cpp = PAGE_SIZE // CMP_STRIDE
    NR = NP * cpp * G
    lg, lq = G.bit_length() - 1, (hg * tp).bit_length() - 1
    low = lax.broadcasted_iota(i32, (2 * G, HEAD_DIM), 0) < G
    for k in range(NP):
        for j in range(cpp // 2):
            for l in range(CMP_STRIDE):
                ra = ((2 * j) * CMP_STRIDE + l) * 2 * G
                rb = ((2 * j + 1) * CMP_STRIDE + l) * 2 * G
                a = pages[k][ra:ra + 2 * G, :]
                b = pages[k][rb:rb + 2 * G, :]
                r0 = (k * (cpp // 2) + j) * 2 * G
                xk_ref[r0:r0 + 2 * G, l * HEAD_DIM:(l + 1) * HEAD_DIM] = jnp.where(low, a, pltpu.roll(b, G, 0))
                xv_ref[r0:r0 + 2 * G, l * HEAD_DIM:(l + 1) * HEAD_DIM] = jnp.where(low, pltpu.roll(a, G, 0), b)

    def compress(x_ref, w1_ref, pb_ref, w2_ref):
        parts = _dot(x_ref[...].astype(bf16), w1_ref[...])
        F = parts.shape[1] // 2
        h = parts[:, :F] + pltpu.roll(parts[:, F:], NR - G, 0) + pb_ref[...]
        return _dot(jax.nn.gelu(h, approximate=True).astype(bf16), w2_ref[...]).astype(bf16)

    kc = compress(xk_ref, w1k_ref, pbk_ref, w2k_ref)
    vc = compress(xv_ref, w1v_ref, pbv_ref, w2v_ref)
    qs = (q_ref[0] * (HEAD_DIM ** -0.5)).astype(bf16)
    row = lax.broadcasted_iota(i32, (NR, G * hg * tp), 0)
    lane = lax.broadcasted_iota(i32, (NR, G * hg * tp), 1)
    dist = past + (lane & (tp - 1)) - ((row >> lg) * CMP_STRIDE + (CMP_BLOCK - 1))
    valid = ((row & (G - 1)) == (lane >> lq)) & (dist >= 0)
    p = _softmax_rows(_dot_nt(kc, qs) - slope_ref[...] * dist.astype(f32), valid)
    o_ref[0] = _dot_tn(p.astype(bf16), vc)
    hi = p.astype(bf16)
    lo = (p - hi.astype(f32)).astype(bf16)
    imp = _dot_tn(ovr_ref[...], hi) + _dot_tn(ovr_ref[...], lo)
    imp = _dot_hilo(imp, rsum_ref[...])
    bp = imp.shape[0]
    cur = (past + (lax.broadcasted_iota(i32, (bp, G * hg * tp), 1) & (tp - 1))) >> 6
    sel_ref[0] = _select_blocks_t(imp, cur, nslc, n_sel)


def _page_specs(NP, rows):
    return [pl.BlockSpec((rows, HEAD_DIM), lambda b, pt, k=k: (pt[b * NP + k], 0)) for k in range(NP)]


def _nsa_sample_cmp(pt, pages, w1c, pb, w2, q_s, slope_l, ovr, rsum, DB, NP, G, hg, past, nslc):
    R = q_s.shape[1]
    NR = NP * (PAGE_SIZE // CMP_STRIDE) * G
    bp = ovr.shape[1]
    const2 = lambda b, pt: (0, 0)
    body = functools.partial(_nsa_sample_cmp_body, NP=NP, G=G, hg=hg, past=past, nslc=nslc,
                             n_sel=min(N_SELECT, nslc))
    consts = [w1c[0], w1c[1], pb[0], pb[1], w2[0], w2[1]]
    gs = pltpu.PrefetchScalarGridSpec(
        num_scalar_prefetch=1, grid=(DB,),
        in_specs=_page_specs(NP, PAGE_SIZE * 2 * G) + [pl.BlockSpec(c.shape, const2) for c in consts] + [
            pl.BlockSpec((1, R, HEAD_DIM), lambda b, pt: (b, 0, 0)),
            pl.BlockSpec(slope_l.shape, const2), pl.BlockSpec(ovr.shape, const2), pl.BlockSpec(rsum.shape, const2)],
        out_specs=[pl.BlockSpec((1, R, HEAD_DIM), lambda b, pt: (b, 0, 0)),
                   pl.BlockSpec((1, bp, R), lambda b, pt: (b, 0, 0))],
        scratch_shapes=[pltpu.VMEM((NR, CMP_STRIDE * HEAD_DIM), f32), pltpu.VMEM((NR, CMP_STRIDE * HEAD_DIM), f32)])
    return pl.pallas_call(
        body, grid_spec=gs,
        out_shape=[jax.ShapeDtypeStruct((DB, R, HEAD_DIM), f32), jax.ShapeDtypeStruct((DB, bp, R), f32)],
        compiler_params=_cp(("parallel",), 52), name="nsa_sample_cmp")(
            pt, *([pages] * NP), *consts, q_s, slope_l, ovr, rsum)


def _nsa_sample_attn_body(pt_ref, *refs, NP, G, hg, past, ts, window):
    pages = refs[:NP]
    (win_ref, q_ref, sel_ref, kvn_ref, gate_ref, slope_ref, ocmp_ref, o_ref, wout_ref,
     ks_ref, vs_ref, kw_ref, vw_ref) = refs[NP:]
    tp, npad = SAMPLE_TOK_PAD, SAMPLE_NEW_PAD
    R = G * hg * tp
    lq = (hg * tp).bit_length() - 1
    wb = win_ref.shape[0] // (2 * G)
    kvn = kvn_ref[0]
    nrow = kvn.shape[0]

    def fill(k_ref, v_ref, srcs, rows, br):
        for g in range(G):
            cols = slice(g * HEAD_DIM, (g + 1) * HEAD_DIM)
            for i, src in enumerate(srcs):
                k_ref[i * rows:(i + 1) * rows, cols] = src[pl.ds(g, rows, stride=2 * G), :].astype(bf16)
                v_ref[i * rows:(i + 1) * rows, cols] = src[pl.ds(G + g, rows, stride=2 * G), :].astype(bf16)
            n0 = len(srcs) * rows
            ok, ov = ((br * 2) * G + g) * HEAD_DIM, ((br * 2 + 1) * G + g) * HEAD_DIM
            k_ref[n0:n0 + nrow, cols] = kvn[:, ok:ok + HEAD_DIM].astype(bf16)
            v_ref[n0:n0 + nrow, cols] = kvn[:, ov:ov + HEAD_DIM].astype(bf16)
            k_ref[n0 + nrow:n0 + npad, cols] = jnp.zeros((npad - nrow, HEAD_DIM), bf16)
            v_ref[n0 + nrow:n0 + npad, cols] = jnp.zeros((npad - nrow, HEAD_DIM), bf16)

    fill(ks_ref, vs_ref, pages, PAGE_SIZE, 1)
    fill(kw_ref, vw_ref, [win_ref], wb, 2)

    qs = q_ref[0] * (HEAD_DIM ** -0.5)
    rowg = lax.broadcasted_iota(i32, (R, HEAD_DIM), 0) >> lq
    q_bd = jnp.concatenate([jnp.where(rowg == g, qs, 0.0) for g in range(G)], axis=1).astype(bf16)
    slope = slope_ref[...]

    def attend(k_ref, v_ref, kpos0, mask_fn):
        n = k_ref.shape[0]
        kpos = kpos0 + lax.broadcasted_iota(i32, (n, R), 0)
        d = past + (lax.broadcasted_iota(i32, (n, R), 1) & (tp - 1)) - kpos
        p = _softmax_rows(_dot_nt(k_ref[...], q_bd) - slope * d.astype(f32), mask_fn(kpos, d))
        o_all = _dot_tn(p.astype(bf16), v_ref[...])
        o = jnp.where(rowg == 0, o_all[:, :HEAD_DIM], 0.0)
        for g in range(1, G):
            o = o + jnp.where(rowg == g, o_all[:, g * HEAD_DIM:(g + 1) * HEAD_DIM], 0.0)
        return o

    sel = sel_ref[0]
    nblk = past // SLC_BLOCK
    selk = jnp.concatenate([jnp.broadcast_to(sel[j:j + 1, :], (SLC_BLOCK, R)) for j in range(nblk)]
                           + [jnp.broadcast_to(sel[nblk:nblk + 1, :], (npad, R))], axis=0)
    o_slc = attend(ks_ref, vs_ref, 0,
                   lambda kpos, d: (selk > 0.5) & (d >= 0) & (kpos < past + ts))
    o_win = attend(kw_ref, vw_ref, past - wb,
                   lambda kpos, d: (d >= 0) & (d < window) & (kpos < past + ts))
    ga = _sigmoid(gate_ref[0])
    o_ref[0] = ga[:, 0:1] * ocmp_ref[0] + ga[:, 1:2] * o_slc + ga[:, 2:3] * o_win

    wrows, new = win_ref.shape[0], ts * 2 * G
    wout_ref[0:wrows - new, :] = win_ref[new:wrows, :]
    for t in range(ts):
        for cg in range(2 * G):
            o = (2 * 2 * G + cg) * HEAD_DIM
            r = wrows - new + t * 2 * G + cg
            wout_ref[r:r + 1, :] = kvn[t:t + 1, o:o + HEAD_DIM]


def _nsa_sample_attn(pt, pages, win, q_s, sel, kvn, gates_s, slope_l, o_cmp, DB, NP, G, hg, past, ts):
    R = q_s.shape[1]
    wrows = win.shape[0] // DB
    wb = wrows // (2 * G)
    body = functools.partial(_nsa_sample_attn_body, NP=NP, G=G, hg=hg, past=past, ts=ts, window=WINDOW)
    seq3 = lambda b, pt: (b, 0, 0)
    gs = pltpu.PrefetchScalarGridSpec(
        num_scalar_prefetch=1, grid=(DB,),
        in_specs=_page_specs(NP, PAGE_SIZE * 2 * G) + [
            pl.BlockSpec((wrows, HEAD_DIM), lambda b, pt: (b, 0)),
            pl.BlockSpec((1, R, HEAD_DIM), seq3),
            pl.BlockSpec((1, sel.shape[1], R), seq3),
            pl.BlockSpec((1,) + kvn.shape[1:], seq3),
            pl.BlockSpec((1, R, 128), seq3),
            pl.BlockSpec(slope_l.shape, lambda b, pt: (0, 0)),
            pl.BlockSpec((1, R, HEAD_DIM), seq3)],
        out_specs=[pl.BlockSpec((1, R, HEAD_DIM), seq3), pl.BlockSpec((wrows, HEAD_DIM), lambda b, pt: (b, 0))],
        scratch_shapes=[pltpu.VMEM((past + SAMPLE_NEW_PAD, G * HEAD_DIM), bf16),
                        pltpu.VMEM((past + SAMPLE_NEW_PAD, G * HEAD_DIM), bf16),
                        pltpu.VMEM((wb + SAMPLE_NEW_PAD, G * HEAD_DIM), bf16),
                        pltpu.VMEM((wb + SAMPLE_NEW_PAD, G * HEAD_DIM), bf16)])
    return pl.pallas_call(
        body, grid_spec=gs,
        out_shape=[jax.ShapeDtypeStruct((DB, R, HEAD_DIM), f32), jax.ShapeDtypeStruct(win.shape, f32)],
        compiler_params=_cp(("parallel",), 52), name="nsa_sample_attn")(
            pt, *([pages] * NP), win, q_s, sel, kvn, gates_s, slope_l, o_cmp)


def _mlstm_chunk(q, k, v, ob, gc, gr, C, n, m_prev, gn, dqk):
    L = q.shape[0]
    k = k * (dqk ** -0.5)
    i_col, f_col = gc[:, 0:1], _log_sigmoid(gc[:, 1:2])
    i_row, f_row = gr[0:1, :], _log_sigmoid(gr[1:2, :])
    r = lax.broadcasted_iota(i32, (L, L), 0)
    s = lax.broadcasted_iota(i32, (L, L), 1)
    causal = r >= s
    b_col = jnp.sum(jnp.where(causal, f_row, 0.0), axis=1, keepdims=True)
    b_row = jnp.sum(jnp.where(r <= s, f_col, 0.0), axis=0, keepdims=True)
    g = b_col + m_prev
    dlog = jnp.where(causal, b_col - b_row + i_row, NEG_INF)
    m_t = jnp.maximum(g, jnp.max(dlog, axis=1, keepdims=True))
    w = jnp.exp(dlog - m_t)
    gw = jnp.exp(g - m_t)
    qb_, kb_, vb_ = q.astype(bf16), k.astype(bf16), v.astype(bf16)
    qk = _dot_nt(qb_, kb_) * w
    num = gw * _dot(qb_, C.astype(bf16)) + _dot(qk.astype(bf16), vb_)
    den = gw * jnp.sum(q * n, axis=1, keepdims=True) + jnp.sum(qk, axis=1, keepdims=True)
    hh = num / jnp.maximum(jnp.abs(den), jnp.exp(-m_t))
    hn = hh * lax.rsqrt(jnp.mean(hh * hh, axis=1, keepdims=True) + RMS_EPS) * gn
    h_out = hn * _sigmoid(ob.astype(f32))

    b_last = b_col[L - 1:L, :]
    ws = b_last - b_col + i_col
    m_new = jnp.maximum(b_last + m_prev, jnp.max(ws, axis=0, keepdims=True))
    sw = jnp.exp(ws - m_new)
    cw = jnp.exp(b_last + m_prev - m_new)
    ksw = k * sw
    C_new = cw * C + _dot_tn(ksw.astype(bf16), vb_)
    n_new = cw * n + jnp.sum(ksw, axis=0, keepdims=True)
    return h_out, C_new, n_new, m_new


def _mlstm_body(q_ref, k_ref, *refs, dqk, nv):
    v_refs = refs[:nv]
    (ob_ref, gc_ref, gr_ref, c0_ref, n0_ref, m0_ref, gn_ref,
     h_ref, c_ref, n_ref, mo_ref, cs_ref, ns_ref, ms_ref) = refs[nv:]
    c = pl.program_id(2)
    nc = pl.num_programs(2)

    @pl.when(c == 0)
    def _():
        cs_ref[...] = c0_ref[0, 0]
        ns_ref[...] = n0_ref[0, 0]
        ms_ref[...] = m0_ref[0, 0]

    v = jnp.concatenate([r[...] for r in v_refs], axis=1)
    h_out, C_new, n_new, m_new = _mlstm_chunk(q_ref[...], k_ref[...], v, ob_ref[...], gc_ref[0, 0], gr_ref[0, 0],
                                              cs_ref[...], ns_ref[...], ms_ref[...], gn_ref[0], dqk)
    h_ref[...] = h_out.astype(h_ref.dtype)
    cs_ref[...] = C_new
    ns_ref[...] = n_new
    ms_ref[...] = m_new

    @pl.when(c == nc - 1)
    def _():
        c_ref[0, 0] = cs_ref[...]
        n_ref[0, 0] = ns_ref[...]
        mo_ref[0, 0] = ms_ref[...]


def _mlstm_step_body(q_ref, k_ref, v_ref, ob_ref, gc_ref, gr_ref, c0_ref, n0_ref, m0_ref, gn_ref,
                     h_ref, c_ref, n_ref, mo_ref, *, dqk, dv, nv, hb):
    for h in range(hb):
        v = jnp.concatenate([v_ref[h * nv + j] for j in range(nv)], axis=1)
        h_out, C_new, n_new, m_new = _mlstm_chunk(
            q_ref[h], k_ref[h], v, ob_ref[:, h * dv:(h + 1) * dv], gc_ref[h, 0], gr_ref[h, 0],
            c0_ref[0, h], n0_ref[0, h], m0_ref[0, h], gn_ref[h], dqk)
        h_ref[:, h * dv:(h + 1) * dv] = h_out.astype(h_ref.dtype)
        c_ref[0, h] = C_new
        n_ref[0, h] = n_new
        mo_ref[0, h] = m_new


def _mlstm_step(z, ob, gcol, grow, c0, n0, m0, gnorm, NS, L, H, dqk, dv):
    nv = dv // dqk
    hb = H
    assert (2 * H) % (nv * hb) == 0
    st = lambda b: (b, 0, 0, 0)
    return pl.pallas_call(
        functools.partial(_mlstm_step_body, dqk=dqk, dv=dv, nv=nv, hb=hb), grid=(NS,),
        in_specs=[pl.BlockSpec((hb, L, dqk), lambda b: (0, b, 0)),
                  pl.BlockSpec((hb, L, dqk), lambda b: (H // hb, b, 0)),
                  pl.BlockSpec((nv * hb, L, dqk), lambda b: ((2 * H) // (nv * hb), b, 0)),
                  pl.BlockSpec((L, hb * dv), lambda b: (b, 0)),
                  pl.BlockSpec((hb, 1, L, 2), lambda b: (0, b, 0, 0)),
                  pl.BlockSpec((hb, 1, 2, L), lambda b: (0, b, 0, 0)),
                  pl.BlockSpec((1, hb, dqk, dv), st),
                  pl.BlockSpec((1, hb, 1, dqk), st),
                  pl.BlockSpec((1, hb, 1, 1), st),
                  pl.BlockSpec((hb, 1, dv), lambda b: (0, 0, 0))],
        out_specs=[pl.BlockSpec((L, hb * dv), lambda b: (b, 0)),
                   pl.BlockSpec((1, hb, dqk, dv), st),
                   pl.BlockSpec((1, hb, 1, dqk), st),
                   pl.BlockSpec((1, hb, 1, 1), st)],
        out_shape=[jax.ShapeDtypeStruct((NS * L, H * dv), bf16),
                   jax.ShapeDtypeStruct((NS, H, dqk, dv), f32),
                   jax.ShapeDtypeStruct((NS, H, 1, dqk), f32),
                   jax.ShapeDtypeStruct((NS, H, 1, 1), f32)],
        compiler_params=_cp(("parallel",), 40), name="mlstm_step")(
            z, z, z, ob, gcol, grow, c0, n0, m0, gnorm)


def _mlstm(z, ob, gcol, grow, c0, n0, m0, gnorm, NS, NC, L, H, dqk, dv):
    R = NS * NC * L
    nv = dv // dqk
    st = lambda b, h, c: (b, h, 0, 0)
    slab = lambda f: pl.BlockSpec((None, L, dqk), lambda b, h, c: (f(h), b * NC + c, 0))
    return pl.pallas_call(
        functools.partial(_mlstm_body, dqk=dqk, nv=nv), grid=(NS, H, NC),
        in_specs=[slab(lambda h: h), slab(lambda h: H + h)]
                 + [slab(lambda h, j=j: 2 * H + nv * h + j) for j in range(nv)] + [
                  pl.BlockSpec((L, dv), lambda b, h, c: (b * NC + c, h)),
                  pl.BlockSpec((1, 1, L, 2), lambda b, h, c: (h, b * NC + c, 0, 0)),
                  pl.BlockSpec((1, 1, 2, L), lambda b, h, c: (h, b * NC + c, 0, 0)),
                  pl.BlockSpec((1, 1, dqk, dv), st),
                  pl.BlockSpec((1, 1, 1, dqk), st),
                  pl.BlockSpec((1, 1, 1, 1), st),
                  pl.BlockSpec((1, 1, dv), lambda b, h, c: (h, 0, 0))],
        out_specs=[pl.BlockSpec((L, dv), lambda b, h, c: (b * NC + c, h)),
                   pl.BlockSpec((1, 1, dqk, dv), st),
                   pl.BlockSpec((1, 1, 1, dqk), st),
                   pl.BlockSpec((1, 1, 1, 1), st)],
        out_shape=[jax.ShapeDtypeStruct((R, H * dv), bf16),
                   jax.ShapeDtypeStruct((NS, H, dqk, dv), f32),
                   jax.ShapeDtypeStruct((NS, H, 1, dqk), f32),
                   jax.ShapeDtypeStruct((NS, H, 1, 1), f32)],
        scratch_shapes=[pltpu.VMEM((dqk, dv), f32), pltpu.VMEM((1, dqk), f32), pltpu.VMEM((1, 1), f32)],
        compiler_params=_cp(("parallel", "parallel", "arbitrary"), 32), name="mlstm")(
            *([z] * (2 + nv)), ob, gcol, grow, c0, n0, m0, gnorm)


def _router_body(x_ref, g_ref, w_ref, b_ref, xn_ref, e_ref, wt_ref, *, n_groups, epg):
    x = x_ref[...]
    xn = x * lax.rsqrt(jnp.mean(x * x, axis=-1, keepdims=True) + RMS_EPS) * g_ref[...]
    xn_ref[...] = xn.reshape(xn_ref.shape)
    logits = _dot(xn.astype(bf16), w_ref[...]) + b_ref[...]
    R, W = logits.shape
    lane = lax.broadcasted_iota(i32, (R, W), 1)
    big = jnp.int32(W)

    def first_argmax(vals, mask):
        mx = jnp.max(jnp.where(mask, vals, -jnp.inf), axis=-1, keepdims=True)
        idx = jnp.min(jnp.where(mask & (vals == mx), lane, big), axis=-1, keepdims=True)
        return mx, idx

    gmask = lane < n_groups
    gprob = _masked_softmax_plain(logits, gmask)
    g_w, grp = first_argmax(gprob, gmask)
    lo = n_groups + grp * epg
    emask = (lane >= lo) & (lane < lo + epg)
    eprob = _masked_softmax_plain(logits, emask)
    p1, i1 = first_argmax(eprob, emask)
    p2, i2 = first_argmax(eprob, emask & (lane != i1))
    tot = p1 + p2
    e_ref[...] = jnp.where(lane == 0, i1 - n_groups, jnp.where(lane == 1, i2 - n_groups, 0))
    wt_ref[...] = jnp.where(lane == 0, g_w * p1 / tot, jnp.where(lane == 1, g_w * p2 / tot, 0.0))


def _masked_softmax_plain(x, mask):
    mx = jnp.max(jnp.where(mask, x, -jnp.inf), axis=-1, keepdims=True)
    e = jnp.where(mask, jnp.exp(x - mx), 0.0)
    return e / jnp.sum(e, axis=-1, keepdims=True)


def _router(x, g, w_r, b_r, n_groups, epg):
    M, D = x.shape
    tm = _pick(M, (256, 128, 8))
    row = lambda i: (i, 0)
    return pl.pallas_call(
        functools.partial(_router_body, n_groups=n_groups, epg=epg), grid=(M // tm,),
        in_specs=[pl.BlockSpec((tm, D), row), pl.BlockSpec((1, D), lambda i: (0, 0)),
                  pl.BlockSpec((D, 128), lambda i: (0, 0)), pl.BlockSpec((1, 128), lambda i: (0, 0))],
        out_specs=[pl.BlockSpec((tm, 1, D), lambda i: (i, 0, 0)), pl.BlockSpec((tm, 128), row),
                   pl.BlockSpec((tm, 128), row)],
        out_shape=[jax.ShapeDtypeStruct((M, 1, D), f32), jax.ShapeDtypeStruct((M, 128), i32),
                   jax.ShapeDtypeStruct((M, 128), f32)],
        compiler_params=_cp(("parallel",), 40), name="router")(x, g.reshape(1, D).astype(f32), w_r, b_r)


def _gather_rows_body(idx_ref, nu_ref, x_hbm, o_ref, buf_ref, x2d_ref, sem, *, rb):
    b = pl.program_id(0)
    nu = nu_ref[0]

    def start(blk):
        slot = blk % 2

        def issue(r, carry):
            pltpu.make_async_copy(x_hbm.at[idx_ref[blk * rb + r]], buf_ref.at[slot, r], sem.at[slot]).start()
            return carry

        lax.fori_loop(0, rb, issue, 0)

    @pl.when((b == 0) & (nu > 0))
    def _():
        start(b)

    @pl.when(b + 1 < nu)
    def _():
        start(b + 1)

    @pl.when(b < nu)
    def _():
        slot = b % 2
        pltpu.make_async_copy(buf_ref.at[slot], buf_ref.at[slot], sem.at[slot]).wait()
        x2d_ref[...] = buf_ref[slot].reshape(x2d_ref.shape)
        o_ref[...] = x2d_ref[...].astype(o_ref.dtype)

    @pl.when(b >= nu)
    def _():
        o_ref[...] = jnp.zeros(o_ref.shape, o_ref.dtype)


def _gather_rows(x, idx, n_used, rb, name):
    R = idx.shape[0]
    D = x.shape[2]
    gs = pltpu.PrefetchScalarGridSpec(
        num_scalar_prefetch=2, grid=(R // rb,),
        in_specs=[pl.BlockSpec(memory_space=pl.ANY)],
        out_specs=pl.BlockSpec((rb, D), lambda b, idx, nu: (b, 0)),
        scratch_shapes=[pltpu.VMEM((2, rb, 1, D), x.dtype), pltpu.VMEM((rb, D), x.dtype),
                        pltpu.SemaphoreType.DMA((2,))])
    return pl.pallas_call(
        functools.partial(_gather_rows_body, rb=rb), grid_spec=gs,
        out_shape=jax.ShapeDtypeStruct((R, D), bf16),
        compiler_params=_cp(("arbitrary",), 32), name=name)(idx, n_used, x)


def _moe_up_body(be_ref, nu_ref, x_ref, wg_ref, wu_ref, h_ref, wgb_ref, wub_ref):
    b = pl.program_id(1)
    changed = (b == 0) | (be_ref[b] != be_ref[jnp.maximum(b - 1, 0)])

    @pl.when(changed)
    def _():
        wgb_ref[...] = wg_ref[0].astype(bf16)
        wub_ref[...] = wu_ref[0].astype(bf16)

    @pl.when(b < nu_ref[0])
    def _():
        x = x_ref[...]
        a = _dot(x, wgb_ref[...])
        u = _dot(x, wub_ref[...])
        h_ref[...] = (a * _sigmoid(a) * u).astype(h_ref.dtype)

    @pl.when(b >= nu_ref[0])
    def _():
        h_ref[...] = jnp.zeros(h_ref.shape, h_ref.dtype)


def _moe_up(blk_expert, n_used, xs, w_g, w_u):
    R, D = xs.shape
    F = w_g.shape[2]
    tf = _pick(F, (512, 256, 128))
    nblk = R // MOE_ROWS
    gs = pltpu.PrefetchScalarGridSpec(
        num_scalar_prefetch=2, grid=(F // tf, nblk),
        in_specs=[pl.BlockSpec((MOE_ROWS, D), lambda f, b, be, nu: (jnp.minimum(b, nu[0] - 1), 0)),
                  pl.BlockSpec((1, D, tf), lambda f, b, be, nu: (be[b], 0, f)),
                  pl.BlockSpec((1, D, tf), lambda f, b, be, nu: (be[b], 0, f))],
        out_specs=pl.BlockSpec((MOE_ROWS, tf), lambda f, b, be, nu: (b, f)),
        scratch_shapes=[pltpu.VMEM((D, tf), bf16), pltpu.VMEM((D, tf), bf16)])
    return pl.pallas_call(
        _moe_up_body, grid_spec=gs, out_shape=jax.ShapeDtypeStruct((R, F), bf16),
        compiler_params=_cp(("arbitrary", "arbitrary"), 52), name="moe_up")(blk_expert, n_used, xs, w_g, w_u)


def _moe_down_body(be_ref, nu_ref, h_ref, wd_ref, rw_ref, y_ref, wdb_ref):
    b = pl.program_id(1)
    changed = (b == 0) | (be_ref[b] != be_ref[jnp.maximum(b - 1, 0)])

    @pl.when(changed)
    def _():
        wdb_ref[...] = wd_ref[0].astype(bf16)

    @pl.when(b < nu_ref[0])
    def _():
        y_ref[...] = (_dot(h_ref[...], wdb_ref[...]) * rw_ref[...]).reshape(y_ref.shape)

    @pl.when(b >= nu_ref[0])
    def _():
        y_ref[...] = jnp.zeros(y_ref.shape, y_ref.dtype)


def _moe_down(blk_expert, n_used, h, w_d, row_w):
    R, F = h.shape
    D = w_d.shape[2]
    td = _pick(D, (2048, 1024, 512, 256, 128))
    nblk = R // MOE_ROWS
    gs = pltpu.PrefetchScalarGridSpec(
        num_scalar_prefetch=2, grid=(D // td, nblk),
        in_specs=[pl.BlockSpec((MOE_ROWS, F), lambda d, b, be, nu: (jnp.minimum(b, nu[0] - 1), 0)),
                  pl.BlockSpec((1, F, td), lambda d, b, be, nu: (be[b], 0, d)),
                  pl.BlockSpec((MOE_ROWS, 1), lambda d, b, be, nu: (b, 0))],
        out_specs=pl.BlockSpec((MOE_ROWS, 1, td), lambda d, b, be, nu: (b, 0, d)),
        scratch_shapes=[pltpu.VMEM((F, td), bf16)])
    return pl.pallas_call(
        _moe_down_body, grid_spec=gs, out_shape=jax.ShapeDtypeStruct((R, 1, D), f32),
        compiler_params=_cp(("arbitrary", "arbitrary"), 40), name="moe_down")(blk_expert, n_used, h, w_d, row_w)


def _final_body(dest_ref, x_ref, y_hbm, g_ref, o_ref, ybuf_ref, y2d_ref, sem, *, tm, topk):
    i = pl.program_id(0)

    def start(step):
        slot = step % 2

        def issue(t, carry):
            for k in range(topk):
                pltpu.make_async_copy(y_hbm.at[dest_ref[(step * tm + t) * topk + k]],
                                      ybuf_ref.at[slot, k * tm + t], sem.at[slot]).start()
            return carry

        lax.fori_loop(0, tm, issue, 0)

    @pl.when(i == 0)
    def _():
        start(i)

    @pl.when(i + 1 < pl.num_programs(0))
    def _():
        start(i + 1)

    slot = i % 2
    pltpu.make_async_copy(ybuf_ref.at[slot], ybuf_ref.at[slot], sem.at[slot]).wait()
    y2d_ref[...] = ybuf_ref[slot].reshape(y2d_ref.shape)
    x = x_ref[...]
    for k in range(topk):
        x = x + y2d_ref[k * tm:(k + 1) * tm, :]
    y = x * lax.rsqrt(jnp.mean(x * x, axis=-1, keepdims=True) + RMS_EPS)
    o_ref[...] = y * g_ref[...]


def _final(x, y_rows, dest, g):
    M, D = x.shape
    tm = _pick(M, (128, 8))
    topk = dest.shape[0] // M
    gs = pltpu.PrefetchScalarGridSpec(
        num_scalar_prefetch=1, grid=(M // tm,),
        in_specs=[pl.BlockSpec((tm, D), lambda i, d: (i, 0)), pl.BlockSpec(memory_space=pl.ANY),
                  pl.BlockSpec((1, D), lambda i, d: (0, 0))],
        out_specs=pl.BlockSpec((tm, D), lambda i, d: (i, 0)),
        scratch_shapes=[pltpu.VMEM((2, topk * tm, 1, D), f32), pltpu.VMEM((topk * tm, D), f32),
                        pltpu.SemaphoreType.DMA((2,))])
    return pl.pallas_call(
        functools.partial(_final_body, tm=tm, topk=topk), grid_spec=gs,
        out_shape=jax.ShapeDtypeStruct((M, D), f32),
        compiler_params=_cp(("arbitrary",), 40), name="final_norm")(dest, x, y_rows, g.reshape(1, D).astype(f32))


def _moe(x2, g_ffn, w_rg, b_rg, w_re, b_re, w_eg, w_eu, w_ed):
    M, D = x2.shape
    n_groups = w_rg.shape[1]
    E = w_re.shape[1]
    epg = E // n_groups
    w_r = jnp.zeros((D, 128), f32).at[:, :n_groups].set(w_rg).at[:, n_groups:n_groups + E].set(w_re).astype(bf16)
    b_r = jnp.zeros((1, 128), f32).at[0, :n_groups].set(b_rg).at[0, n_groups:n_groups + E].set(b_re)
    xn, e_out, w_out = _router(x2, g_ffn, w_r, b_r, n_groups, epg)

    A = M * TOP_K_WITHIN
    e_flat = e_out[:, :TOP_K_WITHIN].reshape(A)
    w_flat = w_out[:, :TOP_K_WITHIN].reshape(A)
    order = jnp.argsort(e_flat)
    e_sorted = e_flat[order]
    counts = jnp.bincount(e_flat, length=E).astype(i32)
    padded = (counts + MOE_ROWS - 1) // MOE_ROWS * MOE_ROWS
    pad_end = jnp.cumsum(padded)
    pad_start = pad_end - padded
    cnt_start = jnp.cumsum(counts) - counts
    dest_sorted = (pad_start[e_sorted] + jnp.arange(A, dtype=i32) - cnt_start[e_sorted]).astype(i32)
    nblk = -(-(A + E * (MOE_ROWS - 1)) // MOE_ROWS)
    R = nblk * MOE_ROWS
    row_tok = jnp.zeros((R,), i32).at[dest_sorted].set((order // TOP_K_WITHIN).astype(i32))
    row_w = jnp.zeros((R,), f32).at[dest_sorted].set(w_flat[order])
    dest = jnp.zeros((A,), i32).at[order].set(dest_sorted)
    n_used = (pad_end[-1] // MOE_ROWS).astype(i32)
    blk = jnp.arange(nblk, dtype=i32)
    blk_expert = jnp.minimum(jnp.searchsorted(pad_end, blk * MOE_ROWS, side="right"), E - 1).astype(i32)
    blk_expert = jnp.where(blk < n_used, blk_expert, blk_expert[jnp.maximum(n_used - 1, 0)])
    n_used1 = n_used.reshape(1)

    xs = _gather_rows(xn, row_tok, n_used1, MOE_ROWS, "moe_gather")
    h = _moe_up(blk_expert, n_used1, xs, w_eg, w_eu)
    y_rows = _moe_down(blk_expert, n_used1, h, w_ed, row_w.reshape(R, 1))
    return y_rows, dest


def _alibi_slopes(G, hg):
    h = np.arange(1, G * hg + 1, dtype=np.float32)
    return (2.0 ** (-8.0 * h / (G * hg))).astype(np.float32).reshape(G, hg)


def _overlap(n_cmp, n_slc):
    cs = np.arange(n_cmp)[:, None] * CMP_STRIDE
    ss = np.arange(n_slc)[None, :] * SLC_BLOCK
    ov = np.clip(np.minimum(cs + CMP_BLOCK, ss + SLC_BLOCK) - np.maximum(cs, ss), 0, None)
    return ov.astype(np.float32) / np.float32(CMP_BLOCK)


def _pad2(a, rows, cols):
    out = np.zeros((rows, cols), np.float32)
    out[:a.shape[0], :a.shape[1]] = a
    return out


def kernel(x_prompt, x_sample, cache_cmp_kv, cache_slc_kv, cache_win_kv, state_mlstm_C, state_mlstm_n,
           state_mlstm_m, page_table, g_norm_mix, w_in, b_in, cmp_pos, w_cmp1, b_cmp1, w_cmp2, g_mlstm_norm,
           w_branch_a, w_branch_b, w_out, g_norm_ffn, w_router_group, b_router_group, w_router_expert,
           b_router_expert, w_exp_gate, w_exp_up, w_exp_down, g_norm_final):
    B, T, D = x_prompt.shape
    DB, TS, _ = x_sample.shape
    depth = w_in.shape[0]
    assert depth == 1, "single layer only"
    G, hd = cache_cmp_kv.shape[4], cache_cmp_kv.shape[5]
    assert hd == HEAD_DIM and cache_cmp_kv.shape[2] == PAGE_SIZE and SLC_BLOCK == 64
    NH = w_branch_a.shape[1] // HEAD_DIM
    hg = NH // G
    H, dqk, dv = state_mlstm_C.shape[2:]
    NP = page_table.shape[1]
    past = NP * PAGE_SIZE
    win_buf = cache_win_kv.shape[2]
    F_cmp = w_cmp1.shape[-1]
    tp, sp = SAMPLE_TOK_PAD, SAMPLE_PAD
    assert T % Q_BLOCK == 0 and T >= win_buf and TS <= tp and TS & (TS - 1) == 0
    assert 2 * G == 8 and (hg * tp) & (hg * tp - 1) == 0, "sample kernels: one token's k/v rows fill one sublane tile"
    assert TS <= SAMPLE_NEW_ROWS and past % SLC_BLOCK == 0
    n_cmp_s = (past + TS - CMP_BLOCK) // CMP_STRIDE + 1
    assert n_cmp_s == past // CMP_STRIDE - 1, "sample compression must not reach the new tokens"
    nslc_s = -(-(past + TS) // SLC_BLOCK)
    assert nslc_s == past // SLC_BLOCK + 1 and nslc_s <= 128 and -(-T // SLC_BLOCK) <= 128

    n_prompt, n_sample = B * T, DB * TS
    M0 = n_prompt + n_sample
    M = -(-M0 // ROW_ALIGN) * ROW_ALIGN
    xa = jnp.concatenate([x_prompt.reshape(n_prompt, D), x_sample.reshape(n_sample, D),
                          jnp.zeros((M - M0, D), f32)], axis=0)

    sizes = (NH * HEAD_DIM, 6 * G * HEAD_DIM, 3 * NH, H * dqk, H * dqk, H * dv, 2 * H, H * dv, 2 * D)
    offs = np.concatenate([[0], np.cumsum(sizes)])
    seg = lambda a, i, j=None: a[..., int(offs[i]):int(offs[(i if j is None else j) + 1])]
    w0, b0 = w_in[0], b_in[0]
    ngate = sizes[2] + sizes[6]
    assert ngate <= 128
    w_g = jnp.concatenate([seg(w0, 2), seg(w0, 6), jnp.zeros((D, 128 - ngate), f32)], axis=1).astype(bf16)
    b_g = jnp.concatenate([seg(b0, 2), seg(b0, 6), jnp.zeros((128 - ngate,), f32)])
    xn = _rmsnorm(xa, g_norm_mix[0], bf16)
    zA = _mm_bias(xn, seg(w0, 0, 1).astype(bf16), seg(b0, 0, 1), f32, "in_proj_nsa")
    assert dv % dqk == 0
    zB = _mm_bias_slabs(xn, seg(w0, 3, 5).astype(bf16), seg(b0, 3, 5), dqk, "in_proj_mlstm")
    zC = _mm_bias(xn, seg(w0, 7, 8).astype(bf16), seg(b0, 7, 8), bf16, "in_proj_gates")
    zG = _mm_bias(xn, w_g, b_g, f32, "in_proj_small")
    qblocks = NH

    w1c = (w_cmp1[0].reshape(2, CMP_STRIDE, 2, HEAD_DIM, F_cmp).transpose(2, 1, 3, 0, 4)
           .reshape(2, CMP_STRIDE * HEAD_DIM, 2 * F_cmp).astype(bf16))
    pos_bias = (b_cmp1[0] + jnp.einsum("lcd,lcdf->cf", cmp_pos[0], w_cmp1[0],
                                       precision=lax.Precision.HIGHEST)).reshape(2, 1, F_cmp)
    w2c = w_cmp2[0].astype(bf16)
    kvc = _compress_prompt(zA, w1c, pos_bias, w2c, B, T, G, qblocks)
    ga_all = zG[:, :3 * NH].reshape(M, 3, G, hg)
    gates_p = jnp.pad(ga_all.transpose(2, 0, 1, 3).reshape(G, M, 3 * hg), ((0, 0), (0, 0), (0, 128 - 3 * hg)))
    slopes = _alibi_slopes(G, hg)
    nch_p, nslc_p = T // CMP_STRIDE, -(-T // SLC_BLOCK)
    ov_t = jnp.asarray(_pad2(_overlap(nch_p - 1, nslc_p).T, -(-nslc_p // 8) * 8, nch_p), bf16)
    h_a_p = _nsa_prompt(zA, kvc, gates_p, jnp.asarray(slopes.reshape(-1)), ov_t, B, T, G, hg, qblocks)

    R_s = G * hg * tp
    zA_s = zA[n_prompt:M0]
    q_s = zA_s[:, :NH * HEAD_DIM].reshape(DB, TS, G, hg, HEAD_DIM).transpose(0, 2, 3, 1, 4)
    q_s = jnp.pad(q_s, ((0, 0), (0, 0), (0, 0), (0, tp - TS), (0, 0))).reshape(DB, R_s, HEAD_DIM)
    kvn = jnp.pad(zA_s[:, NH * HEAD_DIM:].reshape(DB, TS, 6 * G * HEAD_DIM),
                  ((0, 0), (0, SAMPLE_NEW_ROWS - TS), (0, 0)))
    gates_s = ga_all[n_prompt:M0].reshape(DB, TS, 3, G, hg).transpose(0, 3, 4, 1, 2)
    gates_s = jnp.pad(gates_s, ((0, 0), (0, 0), (0, 0), (0, tp - TS), (0, 125))).reshape(DB, R_s, 128)
    slope_l = jnp.asarray(np.repeat(slopes.reshape(-1), tp)[None, :])
    lane_g, lane_t = np.arange(R_s) // (hg * tp), np.arange(R_s) % tp
    rsum = jnp.asarray((lane_g[:, None] == lane_g[None, :]) & (lane_t[:, None] == lane_t[None, :]), bf16)
    nch_s = past // CMP_STRIDE
    ovr = jnp.asarray(np.repeat(_pad2(_overlap(n_cmp_s, nslc_s), nch_s, -(-nslc_s // 8) * 8), G, axis=0), bf16)
    pt_flat = page_table.reshape(-1).astype(i32)
    as_rows = lambda c: c.reshape(-1, HEAD_DIM)
    o_cmp_s, sel_s = _nsa_sample_cmp(pt_flat, as_rows(cache_cmp_kv), w1c, pos_bias, w2c, q_s, slope_l, ovr, rsum,
                                     DB, NP, G, hg, past, nslc_s)
    assert TS <= win_buf
    h_a_s, win_s = _nsa_sample_attn(pt_flat, as_rows(cache_slc_kv), as_rows(cache_win_kv), q_s, sel_s, kvn, gates_s,
                                    slope_l, o_cmp_s, DB, NP, G, hg, past, TS)
    h_a_s = (h_a_s.reshape(DB, G, hg, tp, HEAD_DIM)[:, :, :, :TS].transpose(0, 3, 1, 2, 4)
             .reshape(n_sample, NH * HEAD_DIM).astype(bf16))
    h_a = jnp.concatenate([h_a_p, h_a_s, jnp.zeros((M - M0, NH * HEAD_DIM), bf16)], axis=0)

    if_all = zG[:, 3 * NH:3 * NH + 2 * H].reshape(M, 2, H)
    gnorm = g_mlstm_norm[0].reshape(H, 1, dv).astype(f32)
    L = math.gcd(T, MLSTM_CHUNK)
    NC = T // L
    if_p = if_all[:n_prompt].reshape(B * NC, L, 2, H)
    h_b_p, C_p, n_p, m_p = _mlstm(
        zB, zC, if_p.transpose(3, 0, 1, 2), if_p.transpose(3, 0, 2, 1),
        jnp.zeros((B, H, dqk, dv), f32), jnp.zeros((B, H, 1, dqk), f32), jnp.zeros((B, H, 1, 1), f32),
        gnorm, B, NC, L, H, dqk, dv)
    padrows = lambda a: jnp.pad(a.reshape(DB, TS, -1), ((0, 0), (0, sp - TS), (0, 0))).reshape(DB * sp, -1)
    zB_s = jnp.pad(zB[:, n_prompt:M0].reshape(-1, DB, TS, dqk), ((0, 0), (0, 0), (0, sp - TS), (0, 0)))
    if_s = if_all[n_prompt:M0].reshape(DB, TS, 2, H)
    pad_gate = jnp.broadcast_to(jnp.asarray([NEG_INF, 1e4], f32)[None, None, :, None], (DB, sp - TS, 2, H))
    if_s = jnp.concatenate([if_s, pad_gate], axis=1)
    h_b_s, C_s, n_s, m_s = _mlstm_step(
        zB_s.reshape(-1, DB * sp, dqk), padrows(zC[n_prompt:M0, :H * dv]),
        if_s.transpose(3, 0, 1, 2), if_s.transpose(3, 0, 2, 1),
        state_mlstm_C[0], state_mlstm_n[0].reshape(DB, H, 1, dqk), state_mlstm_m[0].reshape(DB, H, 1, 1),
        gnorm, DB, sp, H, dqk, dv)
    h_b = jnp.concatenate([h_b_p, h_b_s.reshape(DB, sp, H * dv)[:, :TS].reshape(n_sample, H * dv),
                           jnp.zeros((M - M0, H * dv), bf16)], axis=0)

    mix = _merge(h_a, w_branch_a[0].astype(bf16), h_b, w_branch_b[0].astype(bf16), zC, H * dv)
    x2 = _mm_residual(mix, w_out[0].astype(bf16), xa)
    y_rows, dest = _moe(x2, g_norm_ffn[0], w_router_group[0], b_router_group[0], w_router_expert[0],
                        b_router_expert[0], w_exp_gate[0], w_exp_up[0], w_exp_down[0])
    y = _final(x2, y_rows, dest, g_norm_final)

    kv_shape = (2, G, HEAD_DIM)
    kv_p = zA[:n_prompt, NH * HEAD_DIM:].reshape(B, T, 3, *kv_shape)
    kv_s = zA_s[:, NH * HEAD_DIM:].reshape(DB, TS, 3, *kv_shape)
    return (y[:n_prompt].reshape(B, T, D), y[n_prompt:M0].reshape(DB, TS, D),
            kv_p[:, :, 0][None], kv_s[:, :, 0][None], kv_p[:, :, 1][None], kv_s[:, :, 1][None],
            kv_p[:, -win_buf:, 2][None], win_s.reshape(cache_win_kv.shape),
            C_p[None], C_s[None], n_p.reshape(1, B, H, dqk), n_s.reshape(1, DB, H, dqk),
            m_p.reshape(1, B, H), m_s.reshape(1, DB, H))
```

```python
import functools
import math

import numpy as np
import jax
import jax.numpy as jnp
from jax import lax
from jax.experimental import pallas as pl
from jax.experimental.pallas import tpu as pltpu

f32 = jnp.float32
bf16 = jnp.bfloat16
i32 = jnp.int32

HEAD_DIM = 128
PAGE_SIZE = 128
CMP_BLOCK = 32
CMP_STRIDE = 16
SLC_BLOCK = 64
N_SELECT = 16
WINDOW = 512
Q_BLOCK = 128
TOP_K_WITHIN = 2
RMS_EPS = 1e-6
NEG_INF = -1e30
FORCE_BONUS = 1e4

ROW_ALIGN = 256
MOE_ROWS = 256
MLSTM_CHUNK = 256
SAMPLE_PAD = 16
SAMPLE_TOK_PAD = 8
SAMPLE_NEW_ROWS = 16
SAMPLE_NEW_PAD = 128
NSA_KEY_TILE = 512
MIB = 1024 * 1024


def _cp(sem, vmem_mib=None):
    kw = dict(dimension_semantics=sem)
    if vmem_mib is not None:
        kw["vmem_limit_bytes"] = int(vmem_mib * MIB)
    return pltpu.CompilerParams(**kw)


def _pick(n, cands):
    for c in cands:
        if n % c == 0:
            return c
    return n


def _sigmoid(x):
    return 1.0 / (1.0 + jnp.exp(-x))


def _log_sigmoid(x):
    return jnp.minimum(x, 0.0) - jnp.log(1.0 + jnp.exp(-jnp.abs(x)))


def _dot(a, b):
    return jnp.dot(a, b, preferred_element_type=f32)


def _dot_nt(a, b):
    return lax.dot_general(a, b, (((1,), (1,)), ((), ())), preferred_element_type=f32)


def _dot_tn(a, b):
    return lax.dot_general(a, b, (((0,), (0,)), ((), ())), preferred_element_type=f32)


def _dot_hilo(a, b_bf16):
    hi = a.astype(bf16)
    lo = (a - hi.astype(f32)).astype(bf16)
    return _dot(hi, b_bf16) + _dot(lo, b_bf16)


def _masked_softmax(s, mask):
    s = jnp.where(mask, s, NEG_INF)
    p = jnp.where(mask, jnp.exp(s - jnp.max(s, axis=-1, keepdims=True)), 0.0)
    return p / jnp.maximum(jnp.sum(p, axis=-1, keepdims=True), 1.0)


def _rmsnorm_body(x_ref, g_ref, o_ref):
    x = x_ref[...]
    y = x * lax.rsqrt(jnp.mean(x * x, axis=-1, keepdims=True) + RMS_EPS)
    o_ref[...] = (y * g_ref[...]).astype(o_ref.dtype)


def _rmsnorm(x, g, out_dtype):
    M, D = x.shape
    tm = _pick(M, (256, 128, 8))
    return pl.pallas_call(
        _rmsnorm_body, grid=(M // tm,),
        in_specs=[pl.BlockSpec((tm, D), lambda i: (i, 0)), pl.BlockSpec((1, D), lambda i: (0, 0))],
        out_specs=pl.BlockSpec((tm, D), lambda i: (i, 0)),
        out_shape=jax.ShapeDtypeStruct((M, D), out_dtype),
        compiler_params=_cp(("parallel",), 40), name="rmsnorm")(x, g.reshape(1, D).astype(f32))


def _mm_bias_body(x_ref, w_ref, b_ref, o_ref):
    o_ref[...] = (_dot(x_ref[...], w_ref[...]) + b_ref[...]).astype(o_ref.dtype)


def _mm_bias(x, w, b, out_dtype, name):
    M, K = x.shape
    N = w.shape[1]
    tm = _pick(M, (1088, 1024, 768, 512, 256))
    tn = _pick(N, (512, 256, 128))
    return pl.pallas_call(
        _mm_bias_body, grid=(M // tm, N // tn),
        in_specs=[pl.BlockSpec((tm, K), lambda i, j: (i, 0)),
                  pl.BlockSpec((K, tn), lambda i, j: (0, j)),
                  pl.BlockSpec((1, tn), lambda i, j: (0, j))],
        out_specs=pl.BlockSpec((tm, tn), lambda i, j: (i, j)),
        out_shape=jax.ShapeDtypeStruct((M, N), out_dtype),
        compiler_params=_cp(("parallel", "parallel"), 48), name=name)(x, w, b.reshape(1, N).astype(f32))


def _mm_bias_slabs(x, w, b, ws, name):
    M, K = x.shape
    N = w.shape[1]
    tm = _pick(M, (1088, 1024, 768, 512, 256))
    ns = 2 if N % (2 * ws) == 0 else 1

    def body(x_ref, w_ref, b_ref, o_ref):
        r = _dot(x_ref[...], w_ref[...]) + b_ref[...]
        for s in range(ns):
            o_ref[s] = r[:, s * ws:(s + 1) * ws]

    return pl.pallas_call(
        body, grid=(M // tm, N // (ns * ws)),
        in_specs=[pl.BlockSpec((tm, K), lambda i, j: (i, 0)),
                  pl.BlockSpec((K, ns * ws), lambda i, j: (0, j)),
                  pl.BlockSpec((1, ns * ws), lambda i, j: (0, j))],
        out_specs=pl.BlockSpec((ns, tm, ws), lambda i, j: (j, i, 0)),
        out_shape=jax.ShapeDtypeStruct((N // ws, M, ws), f32),
        compiler_params=_cp(("parallel", "parallel"), 48), name=name)(x, w, b.reshape(1, N).astype(f32))


def _merge_body(ha_ref, wa_ref, hb_ref, wb_ref, ga_ref, gb_ref, o_ref):
    a = _dot(ha_ref[...], wa_ref[...])
    b = _dot(hb_ref[...], wb_ref[...])
    o_ref[...] = (_sigmoid(ga_ref[...].astype(f32)) * a + _sigmoid(gb_ref[...].astype(f32)) * b).astype(o_ref.dtype)


def _merge(h_a, w_a, h_b, w_b, zc, gate_col0):
    M, Ka = h_a.shape
    Kb = h_b.shape[1]
    D = w_a.shape[1]
    tm = _pick(M, (544, 512, 256))
    tn = _pick(D, (512, 256, 128))
    c0 = gate_col0 // tn
    nd = D // tn
    return pl.pallas_call(
        _merge_body, grid=(M // tm, nd),
        in_specs=[pl.BlockSpec((tm, Ka), lambda i, j: (i, 0)),
                  pl.BlockSpec((Ka, tn), lambda i, j: (0, j)),
                  pl.BlockSpec((tm, Kb), lambda i, j: (i, 0)),
                  pl.BlockSpec((Kb, tn), lambda i, j: (0, j)),
                  pl.BlockSpec((tm, tn), lambda i, j: (i, c0 + j)),
                  pl.BlockSpec((tm, tn), lambda i, j: (i, c0 + nd + j))],
        out_specs=pl.BlockSpec((tm, tn), lambda i, j: (i, j)),
        out_shape=jax.ShapeDtypeStruct((M, D), bf16),
        compiler_params=_cp(("parallel", "parallel"), 48), name="merge")(h_a, w_a, h_b, w_b, zc, zc)


def _mm_res_body(x_ref, w_ref, r_ref, o_ref):
    o_ref[...] = r_ref[...] + _dot(x_ref[...], w_ref[...])


def _mm_residual(x, w, res):
    M, K = x.shape
    N = w.shape[1]
    tm = _pick(M, (1088, 1024, 768, 512, 256))
    tn = _pick(N, (512, 256, 128))
    return pl.pallas_call(
        _mm_res_body, grid=(M // tm, N // tn),
        in_specs=[pl.BlockSpec((tm, K), lambda i, j: (i, 0)),
                  pl.BlockSpec((K, tn), lambda i, j: (0, j)),
                  pl.BlockSpec((tm, tn), lambda i, j: (i, j))],
        out_specs=pl.BlockSpec((tm, tn), lambda i, j: (i, j)),
        out_shape=jax.ShapeDtypeStruct((M, N), f32),
        compiler_params=_cp(("parallel", "parallel"), 48), name="out_proj")(x, w, res)


def _compress_body(x_ref, w1_ref, pb_ref, w2_ref, o_ref, xcat_ref, *, nch):
    for l in range(CMP_STRIDE):
        xcat_ref[:, l * HEAD_DIM:(l + 1) * HEAD_DIM] = x_ref[pl.ds(l, nch, stride=CMP_STRIDE), :].astype(bf16)
    parts = _dot(xcat_ref[...], w1_ref[0])
    F = parts.shape[1] // 2
    h = parts[:, :F] + pltpu.roll(parts[:, F:], nch - 1, 0) + pb_ref[0]
    h = jax.nn.gelu(h, approximate=True)
    o_ref[0, 0, 0] = _dot(h.astype(bf16), w2_ref[0])


def _compress_prompt(zA, w1c, pb, w2, B, T, G, qblocks):
    nch = T // CMP_STRIDE
    F2 = w1c.shape[2]
    return pl.pallas_call(
        functools.partial(_compress_body, nch=nch), grid=(B, 2, G),
        in_specs=[pl.BlockSpec((T, HEAD_DIM), lambda b, c, g: (b, qblocks + c * G + g)),
                  pl.BlockSpec((1, CMP_STRIDE * HEAD_DIM, F2), lambda b, c, g: (c, 0, 0)),
                  pl.BlockSpec((1, 1, F2 // 2), lambda b, c, g: (c, 0, 0)),
                  pl.BlockSpec((1, F2 // 2, HEAD_DIM), lambda b, c, g: (c, 0, 0))],
        out_specs=pl.BlockSpec((1, 1, 1, nch, HEAD_DIM), lambda b, c, g: (b, c, g, 0, 0)),
        out_shape=jax.ShapeDtypeStruct((B, 2, G, nch, HEAD_DIM), f32),
        scratch_shapes=[pltpu.VMEM((nch, CMP_STRIDE * HEAD_DIM), bf16)],
        compiler_params=_cp(("parallel", "parallel", "parallel"), 32), name="compress_prompt")(zA, w1c, pb, w2)


def _select_blocks_t(imp_t, cur, nblk, n_sel):
    Bk, W = imp_t.shape
    blk = lax.broadcasted_iota(i32, (Bk, W), 0)
    valid = blk <= cur
    forced = (blk == 0) | (blk == cur) | (blk == cur - 1)
    v = jnp.where(valid, imp_t + jnp.where(forced, FORCE_BONUS, 0.0), NEG_INF)
    cnt = jnp.zeros((Bk, W), f32)
    for i in range(nblk):
        ci = v[i:i + 1, :]
        tie = jnp.where(blk > i, 1.0, 0.0)
        cnt = cnt + jnp.where(ci > v, 1.0, jnp.where(ci == v, tie, 0.0))
    return jnp.where(valid, jnp.where(cnt < n_sel, 1.0, 0.0), 0.0)


def _nsa_prompt_body(slope_ref, q_ref, kc_ref, vc_ref, ks_ref, vs_ref, kw_ref, vw_ref, gate_ref, ovt_ref, o_ref,
                     m_ref, l_ref, acc_ref, kbs_ref, vbs_ref, kbw_ref, vbw_ref,
                     *, T, hg, ncp, nslc, n_sel, window, wk, kt):
    g = pl.program_id(1)
    qb = pl.program_id(2)
    QB = Q_BLOCK
    q = q_ref[...] * (HEAD_DIM ** -0.5)
    qs = jnp.concatenate([q[:, h * HEAD_DIM:(h + 1) * HEAD_DIM] for h in range(hg)], axis=0).astype(bf16)
    slopes = [slope_ref[g * hg + h] for h in range(hg)]
    heads = lambda a: [a[h * QB:(h + 1) * QB] for h in range(hg)]
    qpos = lambda w: qb * QB + lax.broadcasted_iota(i32, (QB, w), 0)

    kc = kc_ref[0, 0, 0].astype(bf16)
    vc = vc_ref[0, 0, 0].astype(bf16)
    dist = qpos(ncp) - (lax.broadcasted_iota(i32, (QB, ncp), 1) * CMP_STRIDE + (CMP_BLOCK - 1))
    distf, valid = dist.astype(f32), dist >= 0
    ps = [_masked_softmax(sh - slopes[h] * distf, valid) for h, sh in enumerate(heads(_dot_nt(qs, kc)))]
    o_cmp = jnp.concatenate([_dot(ph.astype(bf16), vc) for ph in ps], axis=0)
    psum = ps[0]
    for h in range(1, hg):
        psum = psum + ps[h]
    hi = psum.astype(bf16)
    lo = (psum - hi.astype(f32)).astype(bf16)
    imp_t = _dot_nt(ovt_ref[...], hi) + _dot_nt(ovt_ref[...], lo)
    nbp = imp_t.shape[0]
    cur = (qb * QB + lax.broadcasted_iota(i32, (nbp, QB), 1)) >> 6
    sel_t = _select_blocks_t(imp_t, cur, nslc, n_sel).astype(bf16)
    eye = jnp.where(lax.broadcasted_iota(i32, (nbp, nbp), 0) == lax.broadcasted_iota(i32, (nbp, nbp), 1), 1.0, 0.0)
    sel_q = _dot_tn(sel_t, eye.astype(bf16)).astype(bf16)

    @pl.when(qb == 0)
    def _():
        for src, dst in ((ks_ref, kbs_ref), (vs_ref, vbs_ref), (kw_ref, kbw_ref), (vw_ref, vbw_ref)):
            dst[...] = src[...].astype(bf16)

    m_ref[...] = jnp.full(m_ref.shape, NEG_INF, f32)
    l_ref[...] = jnp.zeros(l_ref.shape, f32)
    acc_ref[...] = jnp.zeros(acc_ref.shape, f32)

    def tile(i, carry):
        k0 = pl.multiple_of(i * kt, kt)
        k = kbs_ref[pl.ds(k0, kt), :]
        v = vbs_ref[pl.ds(k0, kt), :]
        kpos = k0 + lax.broadcasted_iota(i32, (QB, kt), 1)
        d = qpos(kt) - kpos
        e = jnp.where(lax.broadcasted_iota(i32, (nbp, kt), 0)
                      == ((k0 + lax.broadcasted_iota(i32, (nbp, kt), 1)) >> 6), 1.0, 0.0).astype(bf16)
        madd = jnp.where(d >= 0, jnp.where(_dot(sel_q, e) > 0.5, 0.0, NEG_INF), NEG_INF)
        df = d.astype(f32)
        for h, sh in enumerate(heads(_dot_nt(qs, k))):
            rows = slice(h * QB, (h + 1) * QB)
            sc = sh - slopes[h] * df + madd
            m_old = m_ref[rows, :]
            m_new = jnp.maximum(m_old, jnp.max(sc, axis=-1, keepdims=True))
            alpha = jnp.exp(m_old - m_new)
            pt = jnp.exp(sc - m_new)
            l_ref[rows, :] = alpha * l_ref[rows, :] + jnp.sum(pt, axis=-1, keepdims=True)
            acc_ref[rows, :] = alpha * acc_ref[rows, :] + _dot(pt.astype(bf16), v)
            m_ref[rows, :] = m_new
        return carry

    lax.fori_loop(0, (qb * QB + QB + kt - 1) // kt, tile, 0)
    o_slc = acc_ref[...] / jnp.maximum(l_ref[...], 1.0)

    start = pl.multiple_of(jnp.clip(qb * QB - window, 0, T - wk), QB)
    kw = kbw_ref[pl.ds(start, wk), :]
    vw = vbw_ref[pl.ds(start, wk), :]
    dw = qpos(wk) - (start + lax.broadcasted_iota(i32, (QB, wk), 1))
    maddw = jnp.where(dw >= 0, jnp.where(dw < window, 0.0, NEG_INF), NEG_INF)
    dwf = dw.astype(f32)
    o_win = []
    for h, sh in enumerate(heads(_dot_nt(qs, kw))):
        sw = sh - slopes[h] * dwf + maddw
        ew = jnp.exp(sw - jnp.max(sw, axis=-1, keepdims=True))
        pw = ew / jnp.maximum(jnp.sum(ew, axis=-1, keepdims=True), 1.0)
        o_win.append(_dot(pw.astype(bf16), vw))
    o_win = jnp.concatenate(o_win, axis=0)

    ga = _sigmoid(gate_ref[0])
    outs = []
    for h in range(hg):
        sl = slice(h * Q_BLOCK, (h + 1) * Q_BLOCK)
        outs.append(ga[:, h:h + 1] * o_cmp[sl] + ga[:, hg + h:hg + h + 1] * o_slc[sl]
                    + ga[:, 2 * hg + h:2 * hg + h + 1] * o_win[sl])
    o_ref[...] = jnp.concatenate(outs, axis=1).astype(o_ref.dtype)


def _nsa_prompt(zA, kvc, gates, slopes, ov_t, B, T, G, hg, qblocks):
    NQ = T // Q_BLOCK
    ncp = kvc.shape[3]
    nslc = -(-T // SLC_BLOCK)
    wk = min(WINDOW + Q_BLOCK, T)
    kt = math.gcd(T, NSA_KEY_TILE)
    HQ = hg * Q_BLOCK

    def kvspec(br, c):
        return pl.BlockSpec((T, HEAD_DIM), lambda b, g, i: (b, qblocks + (br * 2 + c) * G + g))

    body = functools.partial(_nsa_prompt_body, T=T, hg=hg, ncp=ncp, nslc=nslc, n_sel=min(N_SELECT, nslc),
                             window=WINDOW, wk=wk, kt=kt)
    return pl.pallas_call(
        body, grid=(B, G, NQ),
        in_specs=[pl.BlockSpec(memory_space=pltpu.SMEM),
                  pl.BlockSpec((Q_BLOCK, hg * HEAD_DIM), lambda b, g, i: (b * NQ + i, g)),
                  pl.BlockSpec((1, 1, 1, ncp, HEAD_DIM), lambda b, g, i: (b, 0, g, 0, 0)),
                  pl.BlockSpec((1, 1, 1, ncp, HEAD_DIM), lambda b, g, i: (b, 1, g, 0, 0)),
                  kvspec(1, 0), kvspec(1, 1), kvspec(2, 0), kvspec(2, 1),
                  pl.BlockSpec((1, Q_BLOCK, 128), lambda b, g, i: (g, b * NQ + i, 0)),
                  pl.BlockSpec(ov_t.shape, lambda b, g, i: (0, 0))],
        out_specs=pl.BlockSpec((Q_BLOCK, hg * HEAD_DIM), lambda b, g, i: (b * NQ + i, g)),
        out_shape=jax.ShapeDtypeStruct((B * T, G * hg * HEAD_DIM), bf16),
        scratch_shapes=[pltpu.VMEM((HQ, 1), f32), pltpu.VMEM((HQ, 1), f32), pltpu.VMEM((HQ, HEAD_DIM), f32)]
                       + [pltpu.VMEM((T, HEAD_DIM), bf16)] * 4,
        compiler_params=_cp(("parallel", "parallel", "arbitrary"), 48), name="nsa_prompt")(
            slopes, zA, kvc, kvc, zA, zA, zA, zA, gates, ov_t)


def _softmax_rows(sc, mask):
    sc = jnp.where(mask, sc, NEG_INF)
    e = jnp.where(mask, jnp.exp(sc - jnp.max(sc, axis=0, keepdims=True)), 0.0)
    return e / jnp.maximum(jnp.sum(e, axis=0, keepdims=True), 1.0)


def _nsa_sample_cmp_body(pt_ref, *refs, NP, G, hg, past, nslc, n_sel):
    pages = refs[:NP]
    (w1k_ref, w1v_ref, pbk_ref, pbv_ref, w2k_ref, w2v_ref, q_ref, slope_ref, ovr_ref, rsum_ref,
     o_ref, sel_ref, xk_ref, xv_ref) = refs[NP:]
    tp = SAMPLE_TOK_PAD
    cpp = PAGE_SIZE // CMP_STRIDE
    NR = NP * cpp * G
    lg, lq = G.bit_length() - 1, (hg * tp).bit_length() - 1
    low = lax.broadcasted_iota(i32, (2 * G, HEAD_DIM), 0) < G
    for k in range(NP):
        for j in range(cpp // 2):
            for l in range(CMP_STRIDE):
                ra = ((2 * j) * CMP_STRIDE + l) * 2 * G
                rb = ((2 * j + 1) * CMP_STRIDE + l) * 2 * G
                a = pages[k][ra:ra + 2 * G, :]
                b = pages[k][rb:rb + 2 * G, :]
                r0 = (k * (cpp // 2) + j) * 2 * G
                xk_ref[r0:r0 + 2 * G, l * HEAD_DIM:(l + 1) * HEAD_DIM] = jnp.where(low, a, pltpu.roll(b, G, 0))
                xv_ref[r0:r0 + 2 * G, l * HEAD_DIM:(l + 1) * HEAD_DIM] = jnp.where(low, pltpu.roll(a, G, 0), b)

    def compress(x_ref, w1_ref, pb_ref, w2_ref):
        parts = _dot(x_ref[...].astype(bf16), w1_ref[...])
        F = parts.shape[1] // 2
        h = parts[:, :F] + pltpu.roll(parts[:, F:], NR - G, 0) + pb_ref[...]
        return _dot(jax.nn.gelu(h, approximate=True).astype(bf16), w2_ref[...]).astype(bf16)

    kc = compress(xk_ref, w1k_ref, pbk_ref, w2k_ref)
    vc = compress(xv_ref, w1v_ref, pbv_ref, w2v_ref)
    qs = (q_ref[0] * (HEAD_DIM ** -0.5)).astype(bf16)
    row = lax.broadcasted_iota(i32, (NR, G * hg * tp), 0)
    lane = lax.broadcasted_iota(i32, (NR, G * hg * tp), 1)
    dist = past + (lane & (tp - 1)) - ((row >> lg) * CMP_STRIDE + (CMP_BLOCK - 1))
    valid = ((row & (G - 1)) == (lane >> lq)) & (dist >= 0)
    p = _softmax_rows(_dot_nt(kc, qs) - slope_ref[...] * dist.astype(f32), valid)
    o_ref[0] = _dot_tn(p.astype(bf16), vc)
    hi = p.astype(bf16)
    lo = (p - hi.astype(f32)).astype(bf16)
    imp = _dot_tn(ovr_ref[...], hi) + _dot_tn(ovr_ref[...], lo)
    imp = _dot_hilo(imp, rsum_ref[...])
    bp = imp.shape[0]
    cur = (past + (lax.broadcasted_iota(i32, (bp, G * hg * tp), 1) & (tp - 1))) >> 6
    sel_ref[0] = _select_blocks_t(imp, cur, nslc, n_sel)


def _page_specs(NP, rows):
    return [pl.BlockSpec((rows, HEAD_DIM), lambda b, pt, k=k: (pt[b * NP + k], 0)) for k in range(NP)]


def _nsa_sample_cmp(pt, pages, w1c, pb, w2, q_s, slope_l, ovr, rsum, DB, NP, G, hg, past, nslc):
    R = q_s.shape[1]
    NR = NP * (PAGE_SIZE // CMP_STRIDE) * G
    bp = ovr.shape[1]
    const2 = lambda b, pt: (0, 0)
    body = functools.partial(_nsa_sample_cmp_body, NP=NP, G=G, hg=hg, past=past, nslc=nslc,
                             n_sel=min(N_SELECT, nslc))
    consts = [w1c[0], w1c[1], pb[0], pb[1], w2[0], w2[1]]
    gs = pltpu.PrefetchScalarGridSpec(
        num_scalar_prefetch=1, grid=(DB,),
        in_specs=_page_specs(NP, PAGE_SIZE * 2 * G) + [pl.BlockSpec(c.shape, const2) for c in consts] + [
            pl.BlockSpec((1, R, HEAD_DIM), lambda b, pt: (b, 0, 0)),
            pl.BlockSpec(slope_l.shape, const2), pl.BlockSpec(ovr.shape, const2), pl.BlockSpec(rsum.shape, const2)],
        out_specs=[pl.BlockSpec((1, R, HEAD_DIM), lambda b, pt: (b, 0, 0)),
                   pl.BlockSpec((1, bp, R), lambda b, pt: (b, 0, 0))],
        scratch_shapes=[pltpu.VMEM((NR, CMP_STRIDE * HEAD_DIM), f32), pltpu.VMEM((NR, CMP_STRIDE * HEAD_DIM), f32)])
    return pl.pallas_call(
        body, grid_spec=gs,
        out_shape=[jax.ShapeDtypeStruct((DB, R, HEAD_DIM), f32), jax.ShapeDtypeStruct((DB, bp, R), f32)],
        compiler_params=_cp(("parallel",), 52), name="nsa_sample_cmp")(
            pt, *([pages] * NP), *consts, q_s, slope_l, ovr, rsum)


def _nsa_sample_attn_body(pt_ref, *refs, NP, G, hg, past, ts, window):
    pages = refs[:NP]
    (win_ref, q_ref, sel_ref, kvn_ref, gate_ref, slope_ref, ocmp_ref, o_ref, wout_ref,
     ks_ref, vs_ref, kw_ref, vw_ref) = refs[NP:]
    tp, npad = SAMPLE_TOK_PAD, SAMPLE_NEW_PAD
    R = G * hg * tp
    lq = (hg * tp).bit_length() - 1
    wb = win_ref.shape[0] // (2 * G)
    kvn = kvn_ref[0]
    nrow = kvn.shape[0]

    def fill(k_ref, v_ref, srcs, rows, br):
        for g in range(G):
            cols = slice(g * HEAD_DIM, (g + 1) * HEAD_DIM)
            for i, src in enumerate(srcs):
                k_ref[i * rows:(i + 1) * rows, cols] = src[pl.ds(g, rows, stride=2 * G), :].astype(bf16)
                v_ref[i * rows:(i + 1) * rows, cols] = src[pl.ds(G + g, rows, stride=2 * G), :].astype(bf16)
            n0 = len(srcs) * rows
            ok, ov = ((br * 2) * G + g) * HEAD_DIM, ((br * 2 + 1) * G + g) * HEAD_DIM
            k_ref[n0:n0 + nrow, cols] = kvn[:, ok:ok + HEAD_DIM].astype(bf16)
            v_ref[n0:n0 + nrow, cols] = kvn[:, ov:ov + HEAD_DIM].astype(bf16)
            k_ref[n0 + nrow:n0 + npad, cols] = jnp.zeros((npad - nrow, HEAD_DIM), bf16)
            v_ref[n0 + nrow:n0 + npad, cols] = jnp.zeros((npad - nrow, HEAD_DIM), bf16)

    fill(ks_ref, vs_ref, pages, PAGE_SIZE, 1)
    fill(kw_ref, vw_ref, [win_ref], wb, 2)

    qs = q_ref[0] * (HEAD_DIM ** -0.5)
    rowg = lax.broadcasted_iota(i32, (R, HEAD_DIM), 0) >> lq
    q_bd = jnp.concatenate([jnp.where(rowg == g, qs, 0.0) for g in range(G)], axis=1).astype(bf16)
    slope = slope_ref[...]

    def attend(k_ref, v_ref, kpos0, mask_fn):
        n = k_ref.shape[0]
        kpos = kpos0 + lax.broadcasted_iota(i32, (n, R), 0)
        d = past + (lax.broadcasted_iota(i32, (n, R), 1) & (tp - 1)) - kpos
        p = _softmax_rows(_dot_nt(k_ref[...], q_bd) - slope * d.astype(f32), mask_fn(kpos, d))
        o_all = _dot_tn(p.astype(bf16), v_ref[...])
        o = jnp.where(rowg == 0, o_all[:, :HEAD_DIM], 0.0)
        for g in range(1, G):
            o = o + jnp.where(rowg == g, o_all[:, g * HEAD_DIM:(g + 1) * HEAD_DIM], 0.0)
        return o

    sel = sel_ref[0]
    nblk = past // SLC_BLOCK
    selk = jnp.concatenate([jnp.broadcast_to(sel[j:j + 1, :], (SLC_BLOCK, R)) for j in range(nblk)]
                           + [jnp.broadcast_to(sel[nblk:nblk + 1, :], (npad, R))], axis=0)
    o_slc = attend(ks_ref, vs_ref, 0,
                   lambda kpos, d: (selk > 0.5) & (d >= 0) & (kpos < past + ts))
    o_win = attend(kw_ref, vw_ref, past - wb,
                   lambda kpos, d: (d >= 0) & (d < window) & (kpos < past + ts))
    ga = _sigmoid(gate_ref[0])
    o_ref[0] = ga[:, 0:1] * ocmp_ref[0] + ga[:, 1:2] * o_slc + ga[:, 2:3] * o_win

    wrows, new = win_ref.shape[0], ts * 2 * G
    wout_ref[0:wrows - new, :] = win_ref[new:wrows, :]
    for t in range(ts):
        for cg in range(2 * G):
            o = (2 * 2 * G + cg) * HEAD_DIM
            r = wrows - new + t * 2 * G + cg
            wout_ref[r:r + 1, :] = kvn[t:t + 1, o:o + HEAD_DIM]


def _nsa_sample_attn(pt, pages, win, q_s, sel, kvn, gates_s, slope_l, o_cmp, DB, NP, G, hg, past, ts):
    R = q_s.shape[1]
    wrows = win.shape[0] // DB
    wb = wrows // (2 * G)
    body = functools.partial(_nsa_sample_attn_body, NP=NP, G=G, hg=hg, past=past, ts=ts, window=WINDOW)
    seq3 = lambda b, pt: (b, 0, 0)
    gs = pltpu.PrefetchScalarGridSpec(
        num_scalar_prefetch=1, grid=(DB,),
        in_specs=_page_specs(NP, PAGE_SIZE * 2 * G) + [
            pl.BlockSpec((wrows, HEAD_DIM), lambda b, pt: (b, 0)),
            pl.BlockSpec((1, R, HEAD_DIM), seq3),
            pl.BlockSpec((1, sel.shape[1], R), seq3),
            pl.BlockSpec((1,) + kvn.shape[1:], seq3),
            pl.BlockSpec((1, R, 128), seq3),
            pl.BlockSpec(slope_l.shape, lambda b, pt: (0, 0)),
            pl.BlockSpec((1, R, HEAD_DIM), seq3)],
        out_specs=[pl.BlockSpec((1, R, HEAD_DIM), seq3), pl.BlockSpec((wrows, HEAD_DIM), lambda b, pt: (b, 0))],
        scratch_shapes=[pltpu.VMEM((past + SAMPLE_NEW_PAD, G * HEAD_DIM), bf16),
                        pltpu.VMEM((past + SAMPLE_NEW_PAD, G * HEAD_DIM), bf16),
                        pltpu.VMEM((wb + SAMPLE_NEW_PAD, G * HEAD_DIM), bf16),
                        pltpu.VMEM((wb + SAMPLE_NEW_PAD, G * HEAD_DIM), bf16)])
    return pl.pallas_call(
        body, grid_spec=gs,
        out_shape=[jax.ShapeDtypeStruct((DB, R, HEAD_DIM), f32), jax.ShapeDtypeStruct(win.shape, f32)],
        compiler_params=_cp(("parallel",), 52), name="nsa_sample_attn")(
            pt, *([pages] * NP), win, q_s, sel, kvn, gates_s, slope_l, o_cmp)


def _mlstm_chunk(q, k, v, ob, gc, gr, C, n, m_prev, gn, dqk):
    L = q.shape[0]
    k = k * (dqk ** -0.5)
    i_col, f_col = gc[:, 0:1], _log_sigmoid(gc[:, 1:2])
    i_row, f_row = gr[0:1, :], _log_sigmoid(gr[1:2, :])
    r = lax.broadcasted_iota(i32, (L, L), 0)
    s = lax.broadcasted_iota(i32, (L, L), 1)
    causal = r >= s
    b_col = jnp.sum(jnp.where(causal, f_row, 0.0), axis=1, keepdims=True)
    b_row = jnp.sum(jnp.where(r <= s, f_col, 0.0), axis=0, keepdims=True)
    g = b_col + m_prev
    dlog = jnp.where(causal, b_col - b_row + i_row, NEG_INF)
    m_t = jnp.maximum(g, jnp.max(dlog, axis=1, keepdims=True))
    w = jnp.exp(dlog - m_t)
    gw = jnp.exp(g - m_t)
    qb_, kb_, vb_ = q.astype(bf16), k.astype(bf16), v.astype(bf16)
    qk = _dot_nt(qb_, kb_) * w
    num = gw * _dot(qb_, C.astype(bf16)) + _dot(qk.astype(bf16), vb_)
    den = gw * jnp.sum(q * n, axis=1, keepdims=True) + jnp.sum(qk, axis=1, keepdims=True)
    hh = num / jnp.maximum(jnp.abs(den), jnp.exp(-m_t))
    hn = hh * lax.rsqrt(jnp.mean(hh * hh, axis=1, keepdims=True) + RMS_EPS) * gn
    h_out = hn * _sigmoid(ob.astype(f32))

    b_last = b_col[L - 1:L, :]
    ws = b_last - b_col + i_col
    m_new = jnp.maximum(b_last + m_prev, jnp.max(ws, axis=0, keepdims=True))
    sw = jnp.exp(ws - m_new)
    cw = jnp.exp(b_last + m_prev - m_new)
    ksw = k * sw
    C_new = cw * C + _dot_tn(ksw.astype(bf16), vb_)
    n_new = cw * n + jnp.sum(ksw, axis=0, keepdims=True)
    return h_out, C_new, n_new, m_new


def _mlstm_body(q_ref, k_ref, *refs, dqk, nv):
    v_refs = refs[:nv]
    (ob_ref, gc_ref, gr_ref, c0_ref, n0_ref, m0_ref, gn_ref,
     h_ref, c_ref, n_ref, mo_ref, cs_ref, ns_ref, ms_ref) = refs[nv:]
    c = pl.program_id(2)
    nc = pl.num_programs(2)

    @pl.when(c == 0)
    def _():
        cs_ref[...] = c0_ref[0, 0]
        ns_ref[...] = n0_ref[0, 0]
        ms_ref[...] = m0_ref[0, 0]

    v = jnp.concatenate([r[...] for r in v_refs], axis=1)
    h_out, C_new, n_new, m_new = _mlstm_chunk(q_ref[...], k_ref[...], v, ob_ref[...], gc_ref[0, 0], gr_ref[0, 0],
                                              cs_ref[...], ns_ref[...], ms_ref[...], gn_ref[0], dqk)
    h_ref[...] = h_out.astype(h_ref.dtype)
    cs_ref[...] = C_new
    ns_ref[...] = n_new
    ms_ref[...] = m_new

    @pl.when(c == nc - 1)
    def _():
        c_ref[0, 0] = cs_ref[...]
        n_ref[0, 0] = ns_ref[...]
        mo_ref[0, 0] = ms_ref[...]


def _mlstm_step_body(q_ref, k_ref, v_ref, ob_ref, gc_ref, gr_ref, c0_ref, n0_ref, m0_ref, gn_ref,
                     h_ref, c_ref, n_ref, mo_ref, *, dqk, dv, nv, hb):
    for h in range(hb):
        v = jnp.concatenate([v_ref[h * nv + j] for j in range(nv)], axis=1)
        h_out, C_new, n_new, m_new = _mlstm_chunk(
            q_ref[h], k_ref[h], v, ob_ref[:, h * dv:(h + 1) * dv], gc_ref[h, 0], gr_ref[h, 0],
            c0_ref[0, h], n0_ref[0, h], m0_ref[0, h], gn_ref[h], dqk)
        h_ref[:, h * dv:(h + 1) * dv] = h_out.astype(h_ref.dtype)
        c_ref[0, h] = C_new
        n_ref[0, h] = n_new
        mo_ref[0, h] = m_new


def _mlstm_step(z, ob, gcol, grow, c0, n0, m0, gnorm, NS, L, H, dqk, dv):
    nv = dv // dqk
    hb = H
    assert (2 * H) % (nv * hb) == 0
    st = lambda b: (b, 0, 0, 0)
    return pl.pallas_call(
        functools.partial(_mlstm_step_body, dqk=dqk, dv=dv, nv=nv, hb=hb), grid=(NS,),
        in_specs=[pl.BlockSpec((hb, L, dqk), lambda b: (0, b, 0)),
                  pl.BlockSpec((hb, L, dqk), lambda b: (H // hb, b, 0)),
                  pl.BlockSpec((nv * hb, L, dqk), lambda b: ((2 * H) // (nv * hb), b, 0)),
                  pl.BlockSpec((L, hb * dv), lambda b: (b, 0)),
                  pl.BlockSpec((hb, 1, L, 2), lambda b: (0, b, 0, 0)),
                  pl.BlockSpec((hb, 1, 2, L), lambda b: (0, b, 0, 0)),
                  pl.BlockSpec((1, hb, dqk, dv), st),
                  pl.BlockSpec((1, hb, 1, dqk), st),
                  pl.BlockSpec((1, hb, 1, 1), st),
                  pl.BlockSpec((hb, 1, dv), lambda b: (0, 0, 0))],
        out_specs=[pl.BlockSpec((L, hb * dv), lambda b: (b, 0)),
                   pl.BlockSpec((1, hb, dqk, dv), st),
                   pl.BlockSpec((1, hb, 1, dqk), st),
                   pl.BlockSpec((1, hb, 1, 1), st)],
        out_shape=[jax.ShapeDtypeStruct((NS * L, H * dv), bf16),
                   jax.ShapeDtypeStruct((NS, H, dqk, dv), f32),
                   jax.ShapeDtypeStruct((NS, H, 1, dqk), f32),
                   jax.ShapeDtypeStruct((NS, H, 1, 1), f32)],
        compiler_params=_cp(("parallel",), 40), name="mlstm_step")(
            z, z, z, ob, gcol, grow, c0, n0, m0, gnorm)


def _mlstm(z, ob, gcol, grow, c0, n0, m0, gnorm, NS, NC, L, H, dqk, dv):
    R = NS * NC * L
    nv = dv // dqk
    st = lambda b, h, c: (b, h, 0, 0)
    slab = lambda f: pl.BlockSpec((None, L, dqk), lambda b, h, c: (f(h), b * NC + c, 0))
    return pl.pallas_call(
        functools.partial(_mlstm_body, dqk=dqk, nv=nv), grid=(NS, H, NC),
        in_specs=[slab(lambda h: h), slab(lambda h: H + h)]
                 + [slab(lambda h, j=j: 2 * H + nv * h + j) for j in range(nv)] + [
                  pl.BlockSpec((L, dv), lambda b, h, c: (b * NC + c, h)),
                  pl.BlockSpec((1, 1, L, 2), lambda b, h, c: (h, b * NC + c, 0, 0)),
                  pl.BlockSpec((1, 1, 2, L), lambda b, h, c: (h, b * NC + c, 0, 0)),
                  pl.BlockSpec((1, 1, dqk, dv), st),
                  pl.BlockSpec((1, 1, 1, dqk), st),
                  pl.BlockSpec((1, 1, 1, 1), st),
                  pl.BlockSpec((1, 1, dv), lambda b, h, c: (h, 0, 0))],
        out_specs=[pl.BlockSpec((L, dv), lambda b, h, c: (b * NC + c, h)),
                   pl.BlockSpec((1, 1, dqk, dv), st),
                   pl.BlockSpec((1, 1, 1, dqk), st),
                   pl.BlockSpec((1, 1, 1, 1), st)],
        out_shape=[jax.ShapeDtypeStruct((R, H * dv), bf16),
                   jax.ShapeDtypeStruct((NS, H, dqk, dv), f32),
                   jax.ShapeDtypeStruct((NS, H, 1, dqk), f32),
                   jax.ShapeDtypeStruct((NS, H, 1, 1), f32)],
        scratch_shapes=[pltpu.VMEM((dqk, dv), f32), pltpu.VMEM((1, dqk), f32), pltpu.VMEM((1, 1), f32)],
        compiler_params=_cp(("parallel", "parallel", "arbitrary"), 32), name="mlstm")(
            *([z] * (2 + nv)), ob, gcol, grow, c0, n0, m0, gnorm)


def _router_body(x_ref, g_ref, w_ref, b_ref, xn_ref, e_ref, wt_ref, *, n_groups, epg):
    x = x_ref[...]
    xn = x * lax.rsqrt(jnp.mean(x * x, axis=-1, keepdims=True) + RMS_EPS) * g_ref[...]
    xb = xn.astype(bf16)
    half = xn.shape[1] // 2
    lo = pltpu.bitcast(xb[:, :half].astype(f32), jnp.uint32) >> 16
    hi = pltpu.bitcast(xb[:, half:].astype(f32), jnp.uint32) & jnp.uint32(0xFFFF0000)
    xn_ref[...] = (lo | hi).reshape(xn_ref.shape)
    logits = _dot(xb, w_ref[...]) + b_ref[...]
    R, W = logits.shape
    lane = lax.broadcasted_iota(i32, (R, W), 1)
    big = jnp.int32(W)

    def first_argmax(vals, mask):
        mx = jnp.max(jnp.where(mask, vals, -jnp.inf), axis=-1, keepdims=True)
        idx = jnp.min(jnp.where(mask & (vals == mx), lane, big), axis=-1, keepdims=True)
        return mx, idx

    gmask = lane < n_groups
    gprob = _masked_softmax_plain(logits, gmask)
    g_w, grp = first_argmax(gprob, gmask)
    lo = n_groups + grp * epg
    emask = (lane >= lo) & (lane < lo + epg)
    eprob = _masked_softmax_plain(logits, emask)
    p1, i1 = first_argmax(eprob, emask)
    p2, i2 = first_argmax(eprob, emask & (lane != i1))
    tot = p1 + p2
    e_ref[...] = jnp.where(lane == 0, i1 - n_groups, jnp.where(lane == 1, i2 - n_groups, 0))
    wt_ref[...] = jnp.where(lane == 0, g_w * p1 / tot, jnp.where(lane == 1, g_w * p2 / tot, 0.0))


def _masked_softmax_plain(x, mask):
    mx = jnp.max(jnp.where(mask, x, -jnp.inf), axis=-1, keepdims=True)
    e = jnp.where(mask, jnp.exp(x - mx), 0.0)
    return e / jnp.sum(e, axis=-1, keepdims=True)


def _router(x, g, w_r, b_r, n_groups, epg):
    M, D = x.shape
    tm = _pick(M, (256, 128, 8))
    row = lambda i: (i, 0)
    return pl.pallas_call(
        functools.partial(_router_body, n_groups=n_groups, epg=epg), grid=(M // tm,),
        in_specs=[pl.BlockSpec((tm, D), row), pl.BlockSpec((1, D), lambda i: (0, 0)),
                  pl.BlockSpec((D, 128), lambda i: (0, 0)), pl.BlockSpec((1, 128), lambda i: (0, 0))],
        out_specs=[pl.BlockSpec((tm, 1, D // 2), lambda i: (i, 0, 0)), pl.BlockSpec((tm, 128), row),
                   pl.BlockSpec((tm, 128), row)],
        out_shape=[jax.ShapeDtypeStruct((M, 1, D // 2), jnp.uint32), jax.ShapeDtypeStruct((M, 128), i32),
                   jax.ShapeDtypeStruct((M, 128), f32)],
        compiler_params=_cp(("parallel",), 40), name="router")(x, g.reshape(1, D).astype(f32), w_r, b_r)


def _gather_rows_body(idx_ref, nu_ref, x_hbm, o_ref, buf_ref, x2d_ref, sem, *, rb):
    b = pl.program_id(0)
    nu = nu_ref[0]

    def start(blk):
        slot = blk % 2

        def issue(r, carry):
            pltpu.make_async_copy(x_hbm.at[idx_ref[blk * rb + r]], buf_ref.at[slot, r], sem.at[slot]).start()
            return carry

        lax.fori_loop(0, rb, issue, 0)

    @pl.when((b == 0) & (nu > 0))
    def _():
        start(b)

    @pl.when(b + 1 < nu)
    def _():
        start(b + 1)

    @pl.when(b < nu)
    def _():
        slot = b % 2
        pltpu.make_async_copy(buf_ref.at[slot], buf_ref.at[slot], sem.at[slot]).wait()
        x2d_ref[...] = buf_ref[slot].reshape(x2d_ref.shape)
        u = x2d_ref[...]
        half = u.shape[1]
        o_ref[:, :half] = pltpu.bitcast(u << 16, f32).astype(o_ref.dtype)
        o_ref[:, half:] = pltpu.bitcast(u & jnp.uint32(0xFFFF0000), f32).astype(o_ref.dtype)

    @pl.when(b >= nu)
    def _():
        o_ref[...] = jnp.zeros(o_ref.shape, o_ref.dtype)


def _gather_rows(x, idx, n_used, rb, name):
    R = idx.shape[0]
    Dh = x.shape[2]
    D = 2 * Dh
    gs = pltpu.PrefetchScalarGridSpec(
        num_scalar_prefetch=2, grid=(R // rb,),
        in_specs=[pl.BlockSpec(memory_space=pl.ANY)],
        out_specs=pl.BlockSpec((rb, D), lambda b, idx, nu: (b, 0)),
        scratch_shapes=[pltpu.VMEM((2, rb, 1, Dh), x.dtype), pltpu.VMEM((rb, Dh), x.dtype),
                        pltpu.SemaphoreType.DMA((2,))])
    return pl.pallas_call(
        functools.partial(_gather_rows_body, rb=rb), grid_spec=gs,
        out_shape=jax.ShapeDtypeStruct((R, D), bf16),
        compiler_params=_cp(("arbitrary",), 32), name=name)(idx, n_used, x)


def _moe_up_body(be_ref, nu_ref, x_ref, wg_ref, wu_ref, h_ref, wgb_ref, wub_ref):
    b = pl.program_id(1)
    changed = (b == 0) | (be_ref[b] != be_ref[jnp.maximum(b - 1, 0)])

    @pl.when(changed)
    def _():
        wgb_ref[...] = wg_ref[0].astype(bf16)
        wub_ref[...] = wu_ref[0].astype(bf16)

    @pl.when(b < nu_ref[0])
    def _():
        x = x_ref[...]
        a = _dot(x, wgb_ref[...])
        u = _dot(x, wub_ref[...])
        h_ref[...] = (a * _sigmoid(a) * u).astype(h_ref.dtype)

    @pl.when(b >= nu_ref[0])
    def _():
        h_ref[...] = jnp.zeros(h_ref.shape, h_ref.dtype)


def _moe_up(blk_expert, n_used, xs, w_g, w_u):
    R, D = xs.shape
    F = w_g.shape[2]
    tf = _pick(F, (512, 256, 128))
    nblk = R // MOE_ROWS
    gs = pltpu.PrefetchScalarGridSpec(
        num_scalar_prefetch=2, grid=(F // tf, nblk),
        in_specs=[pl.BlockSpec((MOE_ROWS, D), lambda f, b, be, nu: (jnp.minimum(b, nu[0] - 1), 0)),
                  pl.BlockSpec((1, D, tf), lambda f, b, be, nu: (be[b], 0, f)),
                  pl.BlockSpec((1, D, tf), lambda f, b, be, nu: (be[b], 0, f))],
        out_specs=pl.BlockSpec((MOE_ROWS, tf), lambda f, b, be, nu: (b, f)),
        scratch_shapes=[pltpu.VMEM((D, tf), bf16), pltpu.VMEM((D, tf), bf16)])
    return pl.pallas_call(
        _moe_up_body, grid_spec=gs, out_shape=jax.ShapeDtypeStruct((R, F), bf16),
        compiler_params=_cp(("arbitrary", "arbitrary"), 52), name="moe_up")(blk_expert, n_used, xs, w_g, w_u)


def _moe_down_body(be_ref, nu_ref, h_ref, wd_ref, rw_ref, y_ref, wdb_ref):
    b = pl.program_id(1)
    changed = (b == 0) | (be_ref[b] != be_ref[jnp.maximum(b - 1, 0)])

    @pl.when(changed)
    def _():
        wdb_ref[...] = wd_ref[0].astype(bf16)

    @pl.when(b < nu_ref[0])
    def _():
        y_ref[...] = (_dot(h_ref[...], wdb_ref[...]) * rw_ref[...]).reshape(y_ref.shape)

    @pl.when(b >= nu_ref[0])
    def _():
        y_ref[...] = jnp.zeros(y_ref.shape, y_ref.dtype)


def _moe_down(blk_expert, n_used, h, w_d, row_w):
    R, F = h.shape
    D = w_d.shape[2]
    td = _pick(D, (2048, 1024, 512, 256, 128))
    nblk = R // MOE_ROWS
    gs = pltpu.PrefetchScalarGridSpec(
        num_scalar_prefetch=2, grid=(D // td, nblk),
        in_specs=[pl.BlockSpec((MOE_ROWS, F), lambda d, b, be, nu: (jnp.minimum(b, nu[0] - 1), 0)),
                  pl.BlockSpec((1, F, td), lambda d, b, be, nu: (be[b], 0, d)),
                  pl.BlockSpec((MOE_ROWS, 1), lambda d, b, be, nu: (b, 0))],
        out_specs=pl.BlockSpec((MOE_ROWS, 1, td), lambda d, b, be, nu: (b, 0, d)),
        scratch_shapes=[pltpu.VMEM((F, td), bf16)])
    return pl.pallas_call(
        _moe_down_body, grid_spec=gs, out_shape=jax.ShapeDtypeStruct((R, 1, D), f32),
        compiler_params=_cp(("arbitrary", "arbitrary"), 40), name="moe_down")(blk_expert, n_used, h, w_d, row_w)


def _final_body(dest_ref, x_ref, y_hbm, g_ref, o_ref, ybuf_ref, y2d_ref, sem, *, tm, topk):
    i = pl.program_id(0)

    def start(step):
        slot = step % 2

        def issue(t, carry):
            for k in range(topk):
                pltpu.make_async_copy(y_hbm.at[dest_ref[(step * tm + t) * topk + k]],
                                      ybuf_ref.at[slot, k * tm + t], sem.at[slot]).start()
            return carry

        lax.fori_loop(0, tm, issue, 0)

    @pl.when(i == 0)
    def _():
        start(i)

    @pl.when(i + 1 < pl.num_programs(0))
    def _():
        start(i + 1)

    slot = i % 2
    pltpu.make_async_copy(ybuf_ref.at[slot], ybuf_ref.at[slot], sem.at[slot]).wait()
    y2d_ref[...] = ybuf_ref[slot].reshape(y2d_ref.shape)
    x = x_ref[...]
    for k in range(topk):
        x = x + y2d_ref[k * tm:(k + 1) * tm, :]
    y = x * lax.rsqrt(jnp.mean(x * x, axis=-1, keepdims=True) + RMS_EPS)
    o_ref[...] = y * g_ref[...]


def _final(x, y_rows, dest, g):
    M, D = x.shape
    tm = _pick(M, (128, 8))
    topk = dest.shape[0] // M
    gs = pltpu.PrefetchScalarGridSpec(
        num_scalar_prefetch=1, grid=(M // tm,),
        in_specs=[pl.BlockSpec((tm, D), lambda i, d: (i, 0)), pl.BlockSpec(memory_space=pl.ANY),
                  pl.BlockSpec((1, D), lambda i, d: (0, 0))],
        out_specs=pl.BlockSpec((tm, D), lambda i, d: (i, 0)),
        scratch_shapes=[pltpu.VMEM((2, topk * tm, 1, D), f32), pltpu.VMEM((topk * tm, D), f32),
                        pltpu.SemaphoreType.DMA((2,))])
    return pl.pallas_call(
        functools.partial(_final_body, tm=tm, topk=topk), grid_spec=gs,
        out_shape=jax.ShapeDtypeStruct((M, D), f32),
        compiler_params=_cp(("arbitrary",), 40), name="final_norm")(dest, x, y_rows, g.reshape(1, D).astype(f32))


def _moe(x2, g_ffn, w_rg, b_rg, w_re, b_re, w_eg, w_eu, w_ed):
    M, D = x2.shape
    n_groups = w_rg.shape[1]
    E = w_re.shape[1]
    epg = E // n_groups
    w_r = jnp.zeros((D, 128), f32).at[:, :n_groups].set(w_rg).at[:, n_groups:n_groups + E].set(w_re).astype(bf16)
    b_r = jnp.zeros((1, 128), f32).at[0, :n_groups].set(b_rg).at[0, n_groups:n_groups + E].set(b_re)
    xn, e_out, w_out = _router(x2, g_ffn, w_r, b_r, n_groups, epg)

    A = M * TOP_K_WITHIN
    e_flat = e_out[:, :TOP_K_WITHIN].reshape(A)
    w_flat = w_out[:, :TOP_K_WITHIN].reshape(A)
    order = jnp.argsort(e_flat)
    e_sorted = e_flat[order]
    counts = jnp.bincount(e_flat, length=E).astype(i32)
    padded = (counts + MOE_ROWS - 1) // MOE_ROWS * MOE_ROWS
    pad_end = jnp.cumsum(padded)
    pad_start = pad_end - padded
    cnt_start = jnp.cumsum(counts) - counts
    dest_sorted = (pad_start[e_sorted] + jnp.arange(A, dtype=i32) - cnt_start[e_sorted]).astype(i32)
    nblk = -(-(A + E * (MOE_ROWS - 1)) // MOE_ROWS)
    R = nblk * MOE_ROWS
    n_used = (pad_end[-1] // MOE_ROWS).astype(i32)
    blk = jnp.arange(nblk, dtype=i32)
    blk_expert = jnp.minimum(jnp.searchsorted(pad_end, blk * MOE_ROWS, side="right"), E - 1).astype(i32)
    row_e = jnp.repeat(blk_expert, MOE_ROWS)
    off = jnp.arange(R, dtype=i32) - pad_start[row_e]
    row_ok = off < counts[row_e]
    src = jnp.clip(cnt_start[row_e] + off, 0, A - 1)
    row_tok = jnp.where(row_ok, (order // TOP_K_WITHIN).astype(i32)[src], 0)
    row_w = jnp.where(row_ok, w_flat[order][src], 0.0)
    dest = dest_sorted[jnp.argsort(order)]
    blk_expert = jnp.where(blk < n_used, blk_expert, blk_expert[jnp.maximum(n_used - 1, 0)])
    n_used1 = n_used.reshape(1)

    xs = _gather_rows(xn, row_tok, n_used1, MOE_ROWS, "moe_gather")
    h = _moe_up(blk_expert, n_used1, xs, w_eg, w_eu)
    y_rows = _moe_down(blk_expert, n_used1, h, w_ed, row_w.reshape(R, 1))
    return y_rows, dest


def _alibi_slopes(G, hg):
    h = np.arange(1, G * hg + 1, dtype=np.float32)
    return (2.0 ** (-8.0 * h / (G * hg))).astype(np.float32).reshape(G, hg)


def _overlap(n_cmp, n_slc):
    cs = np.arange(n_cmp)[:, None] * CMP_STRIDE
    ss = np.arange(n_slc)[None, :] * SLC_BLOCK
    ov = np.clip(np.minimum(cs + CMP_BLOCK, ss + SLC_BLOCK) - np.maximum(cs, ss), 0, None)
    return ov.astype(np.float32) / np.float32(CMP_BLOCK)


def _pad2(a, rows, cols):
    out = np.zeros((rows, cols), np.float32)
    out[:a.shape[0], :a.shape[1]] = a
    return out


def kernel(x_prompt, x_sample, cache_cmp_kv, cache_slc_kv, cache_win_kv, state_mlstm_C, state_mlstm_n,
           state_mlstm_m, page_table, g_norm_mix, w_in, b_in, cmp_pos, w_cmp1, b_cmp1, w_cmp2, g_mlstm_norm,
           w_branch_a, w_branch_b, w_out, g_norm_ffn, w_router_group, b_router_group, w_router_expert,
           b_router_expert, w_exp_gate, w_exp_up, w_exp_down, g_norm_final):
    B, T, D = x_prompt.shape
    DB, TS, _ = x_sample.shape
    depth = w_in.shape[0]
    assert depth == 1, "single layer only"
    G, hd = cache_cmp_kv.shape[4], cache_cmp_kv.shape[5]
    assert hd == HEAD_DIM and cache_cmp_kv.shape[2] == PAGE_SIZE and SLC_BLOCK == 64
    NH = w_branch_a.shape[1] // HEAD_DIM
    hg = NH // G
    H, dqk, dv = state_mlstm_C.shape[2:]
    NP = page_table.shape[1]
    past = NP * PAGE_SIZE
    win_buf = cache_win_kv.shape[2]
    F_cmp = w_cmp1.shape[-1]
    tp, sp = SAMPLE_TOK_PAD, SAMPLE_PAD
    assert T % Q_BLOCK == 0 and T >= win_buf and TS <= tp and TS & (TS - 1) == 0
    assert 2 * G == 8 and (hg * tp) & (hg * tp - 1) == 0, "sample kernels: one token's k/v rows fill one sublane tile"
    assert TS <= SAMPLE_NEW_ROWS and past % SLC_BLOCK == 0
    n_cmp_s = (past + TS - CMP_BLOCK) // CMP_STRIDE + 1
    assert n_cmp_s == past // CMP_STRIDE - 1, "sample compression must not reach the new tokens"
    nslc_s = -(-(past + TS) // SLC_BLOCK)
    assert nslc_s == past // SLC_BLOCK + 1 and nslc_s <= 128 and -(-T // SLC_BLOCK) <= 128

    n_prompt, n_sample = B * T, DB * TS
    M0 = n_prompt + n_sample
    M = -(-M0 // ROW_ALIGN) * ROW_ALIGN
    xa = jnp.concatenate([x_prompt.reshape(n_prompt, D), x_sample.reshape(n_sample, D),
                          jnp.zeros((M - M0, D), f32)], axis=0)

    sizes = (NH * HEAD_DIM, 6 * G * HEAD_DIM, 3 * NH, H * dqk, H * dqk, H * dv, 2 * H, H * dv, 2 * D)
    offs = np.concatenate([[0], np.cumsum(sizes)])
    seg = lambda a, i, j=None: a[..., int(offs[i]):int(offs[(i if j is None else j) + 1])]
    w0, b0 = w_in[0], b_in[0]
    ngate = sizes[2] + sizes[6]
    assert ngate <= 128
    w_g = jnp.concatenate([seg(w0, 2), seg(w0, 6), jnp.zeros((D, 128 - ngate), f32)], axis=1).astype(bf16)
    b_g = jnp.concatenate([seg(b0, 2), seg(b0, 6), jnp.zeros((128 - ngate,), f32)])
    xn = _rmsnorm(xa, g_norm_mix[0], bf16)
    zA = _mm_bias(xn, seg(w0, 0, 1).astype(bf16), seg(b0, 0, 1), f32, "in_proj_nsa")
    assert dv % dqk == 0
    zB = _mm_bias_slabs(xn, seg(w0, 3, 5).astype(bf16), seg(b0, 3, 5), dqk, "in_proj_mlstm")
    zC = _mm_bias(xn, seg(w0, 7, 8).astype(bf16), seg(b0, 7, 8), bf16, "in_proj_gates")
    zG = _mm_bias(xn, w_g, b_g, f32, "in_proj_small")
    qblocks = NH

    w1c = (w_cmp1[0].reshape(2, CMP_STRIDE, 2, HEAD_DIM, F_cmp).transpose(2, 1, 3, 0, 4)
           .reshape(2, CMP_STRIDE * HEAD_DIM, 2 * F_cmp).astype(bf16))
    pos_bias = (b_cmp1[0] + jnp.einsum("lcd,lcdf->cf", cmp_pos[0], w_cmp1[0],
                                       precision=lax.Precision.HIGHEST)).reshape(2, 1, F_cmp)
    w2c = w_cmp2[0].astype(bf16)
    kvc = _compress_prompt(zA, w1c, pos_bias, w2c, B, T, G, qblocks)
    ga_all = zG[:, :3 * NH].reshape(M, 3, G, hg)
    gates_p = jnp.pad(ga_all.transpose(2, 0, 1, 3).reshape(G, M, 3 * hg), ((0, 0), (0, 0), (0, 128 - 3 * hg)))
    slopes = _alibi_slopes(G, hg)
    nch_p, nslc_p = T // CMP_STRIDE, -(-T // SLC_BLOCK)
    ov_t = jnp.asarray(_pad2(_overlap(nch_p - 1, nslc_p).T, -(-nslc_p // 8) * 8, nch_p), bf16)
    h_a_p = _nsa_prompt(zA, kvc, gates_p, jnp.asarray(slopes.reshape(-1)), ov_t, B, T, G, hg, qblocks)

    R_s = G * hg * tp
    zA_s = zA[n_prompt:M0]
    q_s = zA_s[:, :NH * HEAD_DIM].reshape(DB, TS, G, hg, HEAD_DIM).transpose(0, 2, 3, 1, 4)
    q_s = jnp.pad(q_s, ((0, 0), (0, 0), (0, 0), (0, tp - TS), (0, 0))).reshape(DB, R_s, HEAD_DIM)
    kvn = jnp.pad(zA_s[:, NH * HEAD_DIM:].reshape(DB, TS, 6 * G * HEAD_DIM),
                  ((0, 0), (0, SAMPLE_NEW_ROWS - TS), (0, 0)))
    gates_s = ga_all[n_prompt:M0].reshape(DB, TS, 3, G, hg).transpose(0, 3, 4, 1, 2)
    gates_s = jnp.pad(gates_s, ((0, 0), (0, 0), (0, 0), (0, tp - TS), (0, 125))).reshape(DB, R_s, 128)
    slope_l = jnp.asarray(np.repeat(slopes.reshape(-1), tp)[None, :])
    lane_g, lane_t = np.arange(R_s) // (hg * tp), np.arange(R_s) % tp
    rsum = jnp.asarray((lane_g[:, None] == lane_g[None, :]) & (lane_t[:, None] == lane_t[None, :]), bf16)
    nch_s = past // CMP_STRIDE
    ovr = jnp.asarray(np.repeat(_pad2(_overlap(n_cmp_s, nslc_s), nch_s, -(-nslc_s // 8) * 8), G, axis=0), bf16)
    pt_flat = page_table.reshape(-1).astype(i32)
    as_rows = lambda c: c.reshape(-1, HEAD_DIM)
    o_cmp_s, sel_s = _nsa_sample_cmp(pt_flat, as_rows(cache_cmp_kv), w1c, pos_bias, w2c, q_s, slope_l, ovr, rsum,
                                     DB, NP, G, hg, past, nslc_s)
    assert TS <= win_buf
    h_a_s, win_s = _nsa_sample_attn(pt_flat, as_rows(cache_slc_kv), as_rows(cache_win_kv), q_s, sel_s, kvn, gates_s,
                                    slope_l, o_cmp_s, DB, NP, G, hg, past, TS)
    h_a_s = (h_a_s.reshape(DB, G, hg, tp, HEAD_DIM)[:, :, :, :TS].transpose(0, 3, 1, 2, 4)
             .reshape(n_sample, NH * HEAD_DIM).astype(bf16))
    h_a = jnp.concatenate([h_a_p, h_a_s, jnp.zeros((M - M0, NH * HEAD_DIM), bf16)], axis=0)

    if_all = zG[:, 3 * NH:3 * NH + 2 * H].reshape(M, 2, H)
    gnorm = g_mlstm_norm[0].reshape(H, 1, dv).astype(f32)
    L = math.gcd(T, MLSTM_CHUNK)
    NC = T // L
    if_p = if_all[:n_prompt].reshape(B * NC, L, 2, H)
    h_b_p, C_p, n_p, m_p = _mlstm(
        zB, zC, if_p.transpose(3, 0, 1, 2), if_p.transpose(3, 0, 2, 1),
        jnp.zeros((B, H, dqk, dv), f32), jnp.zeros((B, H, 1, dqk), f32), jnp.zeros((B, H, 1, 1), f32),
        gnorm, B, NC, L, H, dqk, dv)
    padrows = lambda a: jnp.pad(a.reshape(DB, TS, -1), ((0, 0), (0, sp - TS), (0, 0))).reshape(DB * sp, -1)
    zB_s = jnp.pad(zB[:, n_prompt:M0].reshape(-1, DB, TS, dqk), ((0, 0), (0, 0), (0, sp - TS), (0, 0)))
    if_s = if_all[n_prompt:M0].reshape(DB, TS, 2, H)
    pad_gate = jnp.broadcast_to(jnp.asarray([NEG_INF, 1e4], f32)[None, None, :, None], (DB, sp - TS, 2, H))
    if_s = jnp.concatenate([if_s, pad_gate], axis=1)
    h_b_s, C_s, n_s, m_s = _mlstm_step(
        zB_s.reshape(-1, DB * sp, dqk), padrows(zC[n_prompt:M0, :H * dv]),
        if_s.transpose(3, 0, 1, 2), if_s.transpose(3, 0, 2, 1),
        state_mlstm_C[0], state_mlstm_n[0].reshape(DB, H, 1, dqk), state_mlstm_m[0].reshape(DB, H, 1, 1),
        gnorm, DB, sp, H, dqk, dv)
    h_b = jnp.concatenate([h_b_p, h_b_s.reshape(DB, sp, H * dv)[:, :TS].reshape(n_sample, H * dv),
                           jnp.zeros((M - M0, H * dv), bf16)], axis=0)

    mix = _merge(h_a, w_branch_a[0].astype(bf16), h_b, w_branch_b[0].astype(bf16), zC, H * dv)
    x2 = _mm_residual(mix, w_out[0].astype(bf16), xa)
    y_rows, dest = _moe(x2, g_norm_ffn[0], w_router_group[0], b_router_group[0], w_router_expert[0],
                        b_router_expert[0], w_exp_gate[0], w_exp_up[0], w_exp_down[0])
    y = _final(x2, y_rows, dest, g_norm_final)

    kv_shape = (2, G, HEAD_DIM)
    kv_p = zA[:n_prompt, NH * HEAD_DIM:].reshape(B, T, 3, *kv_shape)
    kv_s = zA_s[:, NH * HEAD_DIM:].reshape(DB, TS, 3, *kv_shape)
    return (y[:n_prompt].reshape(B, T, D), y[n_prompt:M0].reshape(DB, TS, D),
            kv_p[:, :, 0][None], kv_s[:, :, 0][None], kv_p[:, :, 1][None], kv_s[:, :, 1][None],
            kv_p[:, -win_buf:, 2][None], win_s.reshape(cache_win_kv.shape),
            C_p[None], C_s[None], n_p.reshape(1, B, H, dqk), n_s.reshape(1, DB, H, dqk),
            m_p.reshape(1, B, H), m_s.reshape(1, DB, H))
```

```python
import functools
import math

import numpy as np
import jax
import jax.numpy as jnp
from jax import lax
from jax.experimental import pallas as pl
from jax.experimental.pallas import tpu as pltpu

f32 = jnp.float32
bf16 = jnp.bfloat16
i32 = jnp.int32

HEAD_DIM = 128
PAGE_SIZE = 128
CMP_BLOCK = 32
CMP_STRIDE = 16
SLC_BLOCK = 64
N_SELECT = 16
WINDOW = 512
Q_BLOCK = 128
TOP_K_WITHIN = 2
RMS_EPS = 1e-6
NEG_INF = -1e30
FORCE_BONUS = 1e4

ROW_ALIGN = 256
MOE_ROWS = 256
MLSTM_CHUNK = 256
SAMPLE_PAD = 16
SAMPLE_TOK_PAD = 8
SAMPLE_NEW_ROWS = 16
SAMPLE_NEW_PAD = 128
NSA_KEY_TILE = 512
MIB = 1024 * 1024


def _cp(sem, vmem_mib=None):
    kw = dict(dimension_semantics=sem)
    if vmem_mib is not None:
        kw["vmem_limit_bytes"] = int(vmem_mib * MIB)
    return pltpu.CompilerParams(**kw)


def _pick(n, cands):
    for c in cands:
        if n % c == 0:
            return c
    return n


def _sigmoid(x):
    return 1.0 / (1.0 + jnp.exp(-x))


def _log_sigmoid(x):
    return jnp.minimum(x, 0.0) - jnp.log(1.0 + jnp.exp(-jnp.abs(x)))


def _dot(a, b):
    return jnp.dot(a, b, preferred_element_type=f32)


def _dot_nt(a, b):
    return lax.dot_general(a, b, (((1,), (1,)), ((), ())), preferred_element_type=f32)


def _dot_tn(a, b):
    return lax.dot_general(a, b, (((0,), (0,)), ((), ())), preferred_element_type=f32)


def _dot_hilo(a, b_bf16):
    hi = a.astype(bf16)
    lo = (a - hi.astype(f32)).astype(bf16)
    return _dot(hi, b_bf16) + _dot(lo, b_bf16)


def _masked_softmax(s, mask):
    s = jnp.where(mask, s, NEG_INF)
    p = jnp.where(mask, jnp.exp(s - jnp.max(s, axis=-1, keepdims=True)), 0.0)
    return p / jnp.maximum(jnp.sum(p, axis=-1, keepdims=True), 1.0)


def _rmsnorm_body(x_ref, g_ref, o_ref):
    x = x_ref[...]
    y = x * lax.rsqrt(jnp.mean(x * x, axis=-1, keepdims=True) + RMS_EPS)
    o_ref[...] = (y * g_ref[...]).astype(o_ref.dtype)


def _rmsnorm(x, g, out_dtype):
    M, D = x.shape
    tm = _pick(M, (256, 128, 8))
    return pl.pallas_call(
        _rmsnorm_body, grid=(M // tm,),
        in_specs=[pl.BlockSpec((tm, D), lambda i: (i, 0)), pl.BlockSpec((1, D), lambda i: (0, 0))],
        out_specs=pl.BlockSpec((tm, D), lambda i: (i, 0)),
        out_shape=jax.ShapeDtypeStruct((M, D), out_dtype),
        compiler_params=_cp(("parallel",), 40), name="rmsnorm")(x, g.reshape(1, D).astype(f32))


def _mm_bias_body(x_ref, w_ref, b_ref, o_ref):
    o_ref[...] = (_dot(x_ref[...], w_ref[...]) + b_ref[...]).astype(o_ref.dtype)


def _mm_bias(x, w, b, out_dtype, name):
    M, K = x.shape
    N = w.shape[1]
    tm = _pick(M, (1088, 1024, 768, 512, 256))
    tn = _pick(N, (512, 256, 128))
    return pl.pallas_call(
        _mm_bias_body, grid=(M // tm, N // tn),
        in_specs=[pl.BlockSpec((tm, K), lambda i, j: (i, 0)),
                  pl.BlockSpec((K, tn), lambda i, j: (0, j)),
                  pl.BlockSpec((1, tn), lambda i, j: (0, j))],
        out_specs=pl.BlockSpec((tm, tn), lambda i, j: (i, j)),
        out_shape=jax.ShapeDtypeStruct((M, N), out_dtype),
        compiler_params=_cp(("parallel", "parallel"), 48), name=name)(x, w, b.reshape(1, N).astype(f32))


def _mm_bias_slabs(x, w, b, ws, name):
    M, K = x.shape
    N = w.shape[1]
    tm = _pick(M, (1088, 1024, 768, 512, 256))
    ns = 2 if N % (2 * ws) == 0 else 1

    def body(x_ref, w_ref, b_ref, o_ref):
        r = _dot(x_ref[...], w_ref[...]) + b_ref[...]
        for s in range(ns):
            o_ref[s] = r[:, s * ws:(s + 1) * ws]

    return pl.pallas_call(
        body, grid=(M // tm, N // (ns * ws)),
        in_specs=[pl.BlockSpec((tm, K), lambda i, j: (i, 0)),
                  pl.BlockSpec((K, ns * ws), lambda i, j: (0, j)),
                  pl.BlockSpec((1, ns * ws), lambda i, j: (0, j))],
        out_specs=pl.BlockSpec((ns, tm, ws), lambda i, j: (j, i, 0)),
        out_shape=jax.ShapeDtypeStruct((N // ws, M, ws), f32),
        compiler_params=_cp(("parallel", "parallel"), 48), name=name)(x, w, b.reshape(1, N).astype(f32))


def _merge_body(ha_ref, wa_ref, hb_ref, wb_ref, ga_ref, gb_ref, o_ref):
    a = _dot(ha_ref[...], wa_ref[...])
    b = _dot(hb_ref[...], wb_ref[...])
    o_ref[...] = (_sigmoid(ga_ref[...].astype(f32)) * a + _sigmoid(gb_ref[...].astype(f32)) * b).astype(o_ref.dtype)


def _merge(h_a, w_a, h_b, w_b, zc, gate_col0):
    M, Ka = h_a.shape
    Kb = h_b.shape[1]
    D = w_a.shape[1]
    tm = _pick(M, (544, 512, 256))
    tn = _pick(D, (512, 256, 128))
    c0 = gate_col0 // tn
    nd = D // tn
    return pl.pallas_call(
        _merge_body, grid=(M // tm, nd),
        in_specs=[pl.BlockSpec((tm, Ka), lambda i, j: (i, 0)),
                  pl.BlockSpec((Ka, tn), lambda i, j: (0, j)),
                  pl.BlockSpec((tm, Kb), lambda i, j: (i, 0)),
                  pl.BlockSpec((Kb, tn), lambda i, j: (0, j)),
                  pl.BlockSpec((tm, tn), lambda i, j: (i, c0 + j)),
                  pl.BlockSpec((tm, tn), lambda i, j: (i, c0 + nd + j))],
        out_specs=pl.BlockSpec((tm, tn), lambda i, j: (i, j)),
        out_shape=jax.ShapeDtypeStruct((M, D), bf16),
        compiler_params=_cp(("parallel", "parallel"), 48), name="merge")(h_a, w_a, h_b, w_b, zc, zc)


def _mm_res_body(x_ref, w_ref, r_ref, o_ref):
    o_ref[...] = r_ref[...] + _dot(x_ref[...], w_ref[...])


def _mm_residual(x, w, res):
    M, K = x.shape
    N = w.shape[1]
    tm = _pick(M, (1088, 1024, 768, 512, 256))
    tn = _pick(N, (512, 256, 128))
    return pl.pallas_call(
        _mm_res_body, grid=(M // tm, N // tn),
        in_specs=[pl.BlockSpec((tm, K), lambda i, j: (i, 0)),
                  pl.BlockSpec((K, tn), lambda i, j: (0, j)),
                  pl.BlockSpec((tm, tn), lambda i, j: (i, j))],
        out_specs=pl.BlockSpec((tm, tn), lambda i, j: (i, j)),
        out_shape=jax.ShapeDtypeStruct((M, N), f32),
        compiler_params=_cp(("parallel", "parallel"), 48), name="out_proj")(x, w, res)


def _compress_body(x_ref, w1_ref, pb_ref, w2_ref, o_ref, xcat_ref, *, nch):
    for l in range(CMP_STRIDE):
        xcat_ref[:, l * HEAD_DIM:(l + 1) * HEAD_DIM] = x_ref[pl.ds(l, nch, stride=CMP_STRIDE), :].astype(bf16)
    parts = _dot(xcat_ref[...], w1_ref[0])
    F = parts.shape[1] // 2
    h = parts[:, :F] + pltpu.roll(parts[:, F:], nch - 1, 0) + pb_ref[0]
    h = jax.nn.gelu(h, approximate=True)
    o_ref[0, 0, 0] = _dot(h.astype(bf16), w2_ref[0])


def _compress_prompt(zA, w1c, pb, w2, B, T, G, qblocks):
    nch = T // CMP_STRIDE
    F2 = w1c.shape[2]
    return pl.pallas_call(
        functools.partial(_compress_body, nch=nch), grid=(B, 2, G),
        in_specs=[pl.BlockSpec((T, HEAD_DIM), lambda b, c, g: (b, qblocks + c * G + g)),
                  pl.BlockSpec((1, CMP_STRIDE * HEAD_DIM, F2), lambda b, c, g: (c, 0, 0)),
                  pl.BlockSpec((1, 1, F2 // 2), lambda b, c, g: (c, 0, 0)),
                  pl.BlockSpec((1, F2 // 2, HEAD_DIM), lambda b, c, g: (c, 0, 0))],
        out_specs=pl.BlockSpec((1, 1, 1, nch, HEAD_DIM), lambda b, c, g: (b, c, g, 0, 0)),
        out_shape=jax.ShapeDtypeStruct((B, 2, G, nch, HEAD_DIM), f32),
        scratch_shapes=[pltpu.VMEM((nch, CMP_STRIDE * HEAD_DIM), bf16)],
        compiler_params=_cp(("parallel", "parallel", "parallel"), 32), name="compress_prompt")(zA, w1c, pb, w2)


def _select_blocks_t(imp_t, cur, nblk, n_sel):
    Bk, W = imp_t.shape
    blk = lax.broadcasted_iota(i32, (Bk, W), 0)
    valid = blk <= cur
    forced = (blk == 0) | (blk == cur) | (blk == cur - 1)
    v = jnp.where(valid, imp_t + jnp.where(forced, FORCE_BONUS, 0.0), NEG_INF)
    cnt = jnp.zeros((Bk, W), f32)
    for i in range(nblk):
        ci = v[i:i + 1, :]
        tie = jnp.where(blk > i, 1.0, 0.0)
        cnt = cnt + jnp.where(ci > v, 1.0, jnp.where(ci == v, tie, 0.0))
    return jnp.where(valid, jnp.where(cnt < n_sel, 1.0, 0.0), 0.0)


def _nsa_prompt_body(slope_ref, q_ref, kc_ref, vc_ref, ks_ref, vs_ref, kw_ref, vw_ref, gate_ref, ovt_ref, o_ref,
                     m_ref, l_ref, acc_ref, kbs_ref, vbs_ref, kbw_ref, vbw_ref,
                     *, T, hg, ncp, nslc, n_sel, window, wk, kt):
    g = pl.program_id(1)
    qb = pl.program_id(2)
    QB = Q_BLOCK
    q = q_ref[...] * (HEAD_DIM ** -0.5)
    qs = jnp.concatenate([q[:, h * HEAD_DIM:(h + 1) * HEAD_DIM] for h in range(hg)], axis=0).astype(bf16)
    slopes = [slope_ref[g * hg + h] for h in range(hg)]
    heads = lambda a: [a[h * QB:(h + 1) * QB] for h in range(hg)]
    qpos = lambda w: qb * QB + lax.broadcasted_iota(i32, (QB, w), 0)

    kc = kc_ref[0, 0, 0].astype(bf16)
    vc = vc_ref[0, 0, 0].astype(bf16)
    dist = qpos(ncp) - (lax.broadcasted_iota(i32, (QB, ncp), 1) * CMP_STRIDE + (CMP_BLOCK - 1))
    distf, valid = dist.astype(f32), dist >= 0
    ps = [_masked_softmax(sh - slopes[h] * distf, valid) for h, sh in enumerate(heads(_dot_nt(qs, kc)))]
    o_cmp = jnp.concatenate([_dot(ph.astype(bf16), vc) for ph in ps], axis=0)
    psum = ps[0]
    for h in range(1, hg):
        psum = psum + ps[h]
    hi = psum.astype(bf16)
    lo = (psum - hi.astype(f32)).astype(bf16)
    imp_t = _dot_nt(ovt_ref[...], hi) + _dot_nt(ovt_ref[...], lo)
    nbp = imp_t.shape[0]
    cur = (qb * QB + lax.broadcasted_iota(i32, (nbp, QB), 1)) >> 6
    sel_t = _select_blocks_t(imp_t, cur, nslc, n_sel).astype(bf16)
    eye = jnp.where(lax.broadcasted_iota(i32, (nbp, nbp), 0) == lax.broadcasted_iota(i32, (nbp, nbp), 1), 1.0, 0.0)
    sel_q = _dot_tn(sel_t, eye.astype(bf16)).astype(bf16)

    @pl.when(qb == 0)
    def _():
        for src, dst in ((ks_ref, kbs_ref), (vs_ref, vbs_ref), (kw_ref, kbw_ref), (vw_ref, vbw_ref)):
            dst[...] = src[...].astype(bf16)

    m_ref[...] = jnp.full(m_ref.shape, NEG_INF, f32)
    l_ref[...] = jnp.zeros(l_ref.shape, f32)
    acc_ref[...] = jnp.zeros(acc_ref.shape, f32)

    def tile(i, carry):
        k0 = pl.multiple_of(i * kt, kt)
        k = kbs_ref[pl.ds(k0, kt), :]
        v = vbs_ref[pl.ds(k0, kt), :]
        kpos = k0 + lax.broadcasted_iota(i32, (QB, kt), 1)
        d = qpos(kt) - kpos
        e = jnp.where(lax.broadcasted_iota(i32, (nbp, kt), 0)
                      == ((k0 + lax.broadcasted_iota(i32, (nbp, kt), 1)) >> 6), 1.0, 0.0).astype(bf16)
        madd = jnp.where(d >= 0, jnp.where(_dot(sel_q, e) > 0.5, 0.0, NEG_INF), NEG_INF)
        df = d.astype(f32)
        for h, sh in enumerate(heads(_dot_nt(qs, k))):
            rows = slice(h * QB, (h + 1) * QB)
            sc = sh - slopes[h] * df + madd
            m_old = m_ref[rows, :]
            m_new = jnp.maximum(m_old, jnp.max(sc, axis=-1, keepdims=True))
            alpha = jnp.exp(m_old - m_new)
            pt = jnp.exp(sc - m_new)
            l_ref[rows, :] = alpha * l_ref[rows, :] + jnp.sum(pt, axis=-1, keepdims=True)
            acc_ref[rows, :] = alpha * acc_ref[rows, :] + _dot(pt.astype(bf16), v)
            m_ref[rows, :] = m_new
        return carry

    lax.fori_loop(0, (qb * QB + QB + kt - 1) // kt, tile, 0)
    o_slc = acc_ref[...] / jnp.maximum(l_ref[...], 1.0)

    start = pl.multiple_of(jnp.clip(qb * QB - window, 0, T - wk), QB)
    kw = kbw_ref[pl.ds(start, wk), :]
    vw = vbw_ref[pl.ds(start, wk), :]
    dw = qpos(wk) - (start + lax.broadcasted_iota(i32, (QB, wk), 1))
    maddw = jnp.where(dw >= 0, jnp.where(dw < window, 0.0, NEG_INF), NEG_INF)
    dwf = dw.astype(f32)
    o_win = []
    for h, sh in enumerate(heads(_dot_nt(qs, kw))):
        sw = sh - slopes[h] * dwf + maddw
        ew = jnp.exp(sw - jnp.max(sw, axis=-1, keepdims=True))
        pw = ew / jnp.maximum(jnp.sum(ew, axis=-1, keepdims=True), 1.0)
        o_win.append(_dot(pw.astype(bf16), vw))
    o_win = jnp.concatenate(o_win, axis=0)

    ga = _sigmoid(gate_ref[0])
    outs = []
    for h in range(hg):
        sl = slice(h * Q_BLOCK, (h + 1) * Q_BLOCK)
        outs.append(ga[:, h:h + 1] * o_cmp[sl] + ga[:, hg + h:hg + h + 1] * o_slc[sl]
                    + ga[:, 2 * hg + h:2 * hg + h + 1] * o_win[sl])
    o_ref[...] = jnp.concatenate(outs, axis=1).astype(o_ref.dtype)


def _nsa_prompt(zA, kvc, gates, slopes, ov_t, B, T, G, hg, qblocks):
    NQ = T // Q_BLOCK
    ncp = kvc.shape[3]
    nslc = -(-T // SLC_BLOCK)
    wk = min(WINDOW + Q_BLOCK, T)
    kt = math.gcd(T, NSA_KEY_TILE)
    HQ = hg * Q_BLOCK

    def kvspec(br, c):
        return pl.BlockSpec((T, HEAD_DIM), lambda b, g, i: (b, qblocks + (br * 2 + c) * G + g))

    body = functools.partial(_nsa_prompt_body, T=T, hg=hg, ncp=ncp, nslc=nslc, n_sel=min(N_SELECT, nslc),
                             window=WINDOW, wk=wk, kt=kt)
    return pl.pallas_call(
        body, grid=(B, G, NQ),
        in_specs=[pl.BlockSpec(memory_space=pltpu.SMEM),
                  pl.BlockSpec((Q_BLOCK, hg * HEAD_DIM), lambda b, g, i: (b * NQ + i, g)),
                  pl.BlockSpec((1, 1, 1, ncp, HEAD_DIM), lambda b, g, i: (b, 0, g, 0, 0)),
                  pl.BlockSpec((1, 1, 1, ncp, HEAD_DIM), lambda b, g, i: (b, 1, g, 0, 0)),
                  kvspec(1, 0), kvspec(1, 1), kvspec(2, 0), kvspec(2, 1),
                  pl.BlockSpec((1, Q_BLOCK, 128), lambda b, g, i: (g, b * NQ + i, 0)),
                  pl.BlockSpec(ov_t.shape, lambda b, g, i: (0, 0))],
        out_specs=pl.BlockSpec((Q_BLOCK, hg * HEAD_DIM), lambda b, g, i: (b * NQ + i, g)),
        out_shape=jax.ShapeDtypeStruct((B * T, G * hg * HEAD_DIM), bf16),
        scratch_shapes=[pltpu.VMEM((HQ, 1), f32), pltpu.VMEM((HQ, 1), f32), pltpu.VMEM((HQ, HEAD_DIM), f32)]
                       + [pltpu.VMEM((T, HEAD_DIM), bf16)] * 4,
        compiler_params=_cp(("parallel", "parallel", "arbitrary"), 48), name="nsa_prompt")(
            slopes, zA, kvc, kvc, zA, zA, zA, zA, gates, ov_t)


def _softmax_rows(sc, mask):
    sc = jnp.where(mask, sc, NEG_INF)
    e = jnp.where(mask, jnp.exp(sc - jnp.max(sc, axis=0, keepdims=True)), 0.0)
    return e / jnp.maximum(jnp.sum(e, axis=0, keepdims=True), 1.0)


def _nsa_sample_cmp_body(pt_ref, *refs, NP, G, hg, past, nslc, n_sel):
    pages = refs[:NP]
    (w1k_ref, w1v_ref, pbk_ref, pbv_ref, w2k_ref, w2v_ref, q_ref, slope_ref, ovr_ref, rsum_ref,
     o_ref, sel_ref, xk_ref, xv_ref) = refs[NP:]
    tp = SAMPLE_TOK_PAD
    cpp = PAGE_SIZE // CMP_STRIDE
    NR = NP * cpp * G
    lg, lq = G.bit_length() - 1, (hg * tp).bit_length() - 1
    low = lax.broadcasted_iota(i32, (2 * G, HEAD_DIM), 0) < G
    for k in range(NP):
        for j in range(cpp // 2):
            for l in range(CMP_STRIDE):
                ra = ((2 * j) * CMP_STRIDE + l) * 2 * G
                rb = ((2 * j + 1) * CMP_STRIDE + l) * 2 * G
                a = pages[k][ra:ra + 2 * G, :]
                b = pages[k][rb:rb + 2 * G, :]
                r0 = (k * (cpp // 2) + j) * 2 * G
                xk_ref[r0:r0 + 2 * G, l * HEAD_DIM:(l + 1) * HEAD_DIM] = jnp.where(low, a, pltpu.roll(b, G, 0))
                xv_ref[r0:r0 + 2 * G, l * HEAD_DIM:(l + 1) * HEAD_DIM] = jnp.where(low, pltpu.roll(a, G, 0), b)

    def compress(x_ref, w1_ref, pb_ref, w2_ref):
        parts = _dot(x_ref[...].astype(bf16), w1_ref[...])
        F = parts.shape[1] // 2
        h = parts[:, :F] + pltpu.roll(parts[:, F:], NR - G, 0) + pb_ref[...]
        return _dot(jax.nn.gelu(h, approximate=True).astype(bf16), w2_ref[...]).astype(bf16)

    kc = compress(xk_ref, w1k_ref, pbk_ref, w2k_ref)
    vc = compress(xv_ref, w1v_ref, pbv_ref, w2v_ref)
    qs = (q_ref[0] * (HEAD_DIM ** -0.5)).astype(bf16)
    row = lax.broadcasted_iota(i32, (NR, G * hg * tp), 0)
    lane = lax.broadcasted_iota(i32, (NR, G * hg * tp), 1)
    dist = past + (lane & (tp - 1)) - ((row >> lg) * CMP_STRIDE + (CMP_BLOCK - 1))
    valid = ((row & (G - 1)) == (lane >> lq)) & (dist >= 0)
    p = _softmax_rows(_dot_nt(kc, qs) - slope_ref[...] * dist.astype(f32), valid)
    o_ref[0] = _dot_tn(p.astype(bf16), vc)
    hi = p.astype(bf16)
    lo = (p - hi.astype(f32)).astype(bf16)
    imp = _dot_tn(ovr_ref[...], hi) + _dot_tn(ovr_ref[...], lo)
    imp = _dot_hilo(imp, rsum_ref[...])
    bp = imp.shape[0]
    cur = (past + (lax.broadcasted_iota(i32, (bp, G * hg * tp), 1) & (tp - 1))) >> 6
    sel_ref[0] = _select_blocks_t(imp, cur, nslc, n_sel)


def _page_specs(NP, rows):
    return [pl.BlockSpec((rows, HEAD_DIM), lambda b, pt, k=k: (pt[b * NP + k], 0)) for k in range(NP)]


def _nsa_sample_cmp(pt, pages, w1c, pb, w2, q_s, slope_l, ovr, rsum, DB, NP, G, hg, past, nslc):
    R = q_s.shape[1]
    NR = NP * (PAGE_SIZE // CMP_STRIDE) * G
    bp = ovr.shape[1]
    const2 = lambda b, pt: (0, 0)
    body = functools.partial(_nsa_sample_cmp_body, NP=NP, G=G, hg=hg, past=past, nslc=nslc,
                             n_sel=min(N_SELECT, nslc))
    consts = [w1c[0], w1c[1], pb[0], pb[1], w2[0], w2[1]]
    gs = pltpu.PrefetchScalarGridSpec(
        num_scalar_prefetch=1, grid=(DB,),
        in_specs=_page_specs(NP, PAGE_SIZE * 2 * G) + [pl.BlockSpec(c.shape, const2) for c in consts] + [
            pl.BlockSpec((1, R, HEAD_DIM), lambda b, pt: (b, 0, 0)),
            pl.BlockSpec(slope_l.shape, const2), pl.BlockSpec(ovr.shape, const2), pl.BlockSpec(rsum.shape, const2)],
        out_specs=[pl.BlockSpec((1, R, HEAD_DIM), lambda b, pt: (b, 0, 0)),
                   pl.BlockSpec((1, bp, R), lambda b, pt: (b, 0, 0))],
        scratch_shapes=[pltpu.VMEM((NR, CMP_STRIDE * HEAD_DIM), f32), pltpu.VMEM((NR, CMP_STRIDE * HEAD_DIM), f32)])
    return pl.pallas_call(
        body, grid_spec=gs,
        out_shape=[jax.ShapeDtypeStruct((DB, R, HEAD_DIM), f32), jax.ShapeDtypeStruct((DB, bp, R), f32)],
        compiler_params=_cp(("parallel",), 52), name="nsa_sample_cmp")(
            pt, *([pages] * NP), *consts, q_s, slope_l, ovr, rsum)


def _nsa_sample_attn_body(pt_ref, *refs, NP, G, hg, past, ts, window):
    pages = refs[:NP]
    (win_ref, q_ref, sel_ref, kvn_ref, gate_ref, slope_ref, ocmp_ref, o_ref, wout_ref,
     ks_ref, vs_ref, kw_ref, vw_ref) = refs[NP:]
    tp, npad = SAMPLE_TOK_PAD, SAMPLE_NEW_PAD
    R = G * hg * tp
    lq = (hg * tp).bit_length() - 1
    wb = win_ref.shape[0] // (2 * G)
    kvn = kvn_ref[0]
    nrow = kvn.shape[0]

    def fill(k_ref, v_ref, srcs, rows, br):
        for g in range(G):
            cols = slice(g * HEAD_DIM, (g + 1) * HEAD_DIM)
            for i, src in enumerate(srcs):
                k_ref[i * rows:(i + 1) * rows, cols] = src[pl.ds(g, rows, stride=2 * G), :].astype(bf16)
                v_ref[i * rows:(i + 1) * rows, cols] = src[pl.ds(G + g, rows, stride=2 * G), :].astype(bf16)
            n0 = len(srcs) * rows
            ok, ov = ((br * 2) * G + g) * HEAD_DIM, ((br * 2 + 1) * G + g) * HEAD_DIM
            k_ref[n0:n0 + nrow, cols] = kvn[:, ok:ok + HEAD_DIM].astype(bf16)
            v_ref[n0:n0 + nrow, cols] = kvn[:, ov:ov + HEAD_DIM].astype(bf16)
            k_ref[n0 + nrow:n0 + npad, cols] = jnp.zeros((npad - nrow, HEAD_DIM), bf16)
            v_ref[n0 + nrow:n0 + npad, cols] = jnp.zeros((npad - nrow, HEAD_DIM), bf16)

    fill(ks_ref, vs_ref, pages, PAGE_SIZE, 1)
    fill(kw_ref, vw_ref, [win_ref], wb, 2)

    qs = q_ref[0] * (HEAD_DIM ** -0.5)
    rowg = lax.broadcasted_iota(i32, (R, HEAD_DIM), 0) >> lq
    q_bd = jnp.concatenate([jnp.where(rowg == g, qs, 0.0) for g in range(G)], axis=1).astype(bf16)
    slope = slope_ref[...]

    def attend(k_ref, v_ref, kpos0, mask_fn):
        n = k_ref.shape[0]
        kpos = kpos0 + lax.broadcasted_iota(i32, (n, R), 0)
        d = past + (lax.broadcasted_iota(i32, (n, R), 1) & (tp - 1)) - kpos
        p = _softmax_rows(_dot_nt(k_ref[...], q_bd) - slope * d.astype(f32), mask_fn(kpos, d))
        o_all = _dot_tn(p.astype(bf16), v_ref[...])
        o = jnp.where(rowg == 0, o_all[:, :HEAD_DIM], 0.0)
        for g in range(1, G):
            o = o + jnp.where(rowg == g, o_all[:, g * HEAD_DIM:(g + 1) * HEAD_DIM], 0.0)
        return o

    sel = sel_ref[0]
    nblk = past // SLC_BLOCK
    selk = jnp.concatenate([jnp.broadcast_to(sel[j:j + 1, :], (SLC_BLOCK, R)) for j in range(nblk)]
                           + [jnp.broadcast_to(sel[nblk:nblk + 1, :], (npad, R))], axis=0)
    o_slc = attend(ks_ref, vs_ref, 0,
                   lambda kpos, d: (selk > 0.5) & (d >= 0) & (kpos < past + ts))
    o_win = attend(kw_ref, vw_ref, past - wb,
                   lambda kpos, d: (d >= 0) & (d < window) & (kpos < past + ts))
    ga = _sigmoid(gate_ref[0])
    o_ref[0] = ga[:, 0:1] * ocmp_ref[0] + ga[:, 1:2] * o_slc + ga[:, 2:3] * o_win

    wrows, new = win_ref.shape[0], ts * 2 * G
    wout_ref[0:wrows - new, :] = win_ref[new:wrows, :]
    for t in range(ts):
        for cg in range(2 * G):
            o = (2 * 2 * G + cg) * HEAD_DIM
            r = wrows - new + t * 2 * G + cg
            wout_ref[r:r + 1, :] = kvn[t:t + 1, o:o + HEAD_DIM]


def _nsa_sample_attn(pt, pages, win, q_s, sel, kvn, gates_s, slope_l, o_cmp, DB, NP, G, hg, past, ts):
    R = q_s.shape[1]
    wrows = win.shape[0] // DB
    wb = wrows // (2 * G)
    body = functools.partial(_nsa_sample_attn_body, NP=NP, G=G, hg=hg, past=past, ts=ts, window=WINDOW)
    seq3 = lambda b, pt: (b, 0, 0)
    gs = pltpu.PrefetchScalarGridSpec(
        num_scalar_prefetch=1, grid=(DB,),
        in_specs=_page_specs(NP, PAGE_SIZE * 2 * G) + [
            pl.BlockSpec((wrows, HEAD_DIM), lambda b, pt: (b, 0)),
            pl.BlockSpec((1, R, HEAD_DIM), seq3),
            pl.BlockSpec((1, sel.shape[1], R), seq3),
            pl.BlockSpec((1,) + kvn.shape[1:], seq3),
            pl.BlockSpec((1, R, 128), seq3),
            pl.BlockSpec(slope_l.shape, lambda b, pt: (0, 0)),
            pl.BlockSpec((1, R, HEAD_DIM), seq3)],
        out_specs=[pl.BlockSpec((1, R, HEAD_DIM), seq3), pl.BlockSpec((wrows, HEAD_DIM), lambda b, pt: (b, 0))],
        scratch_shapes=[pltpu.VMEM((past + SAMPLE_NEW_PAD, G * HEAD_DIM), bf16),
                        pltpu.VMEM((past + SAMPLE_NEW_PAD, G * HEAD_DIM), bf16),
                        pltpu.VMEM((wb + SAMPLE_NEW_PAD, G * HEAD_DIM), bf16),
                        pltpu.VMEM((wb + SAMPLE_NEW_PAD, G * HEAD_DIM), bf16)])
    return pl.pallas_call(
        body, grid_spec=gs,
        out_shape=[jax.ShapeDtypeStruct((DB, R, HEAD_DIM), f32), jax.ShapeDtypeStruct(win.shape, f32)],
        compiler_params=_cp(("parallel",), 52), name="nsa_sample_attn")(
            pt, *([pages] * NP), win, q_s, sel, kvn, gates_s, slope_l, o_cmp)


def _mlstm_chunk(q, k, v, ob, gc, gr, C, n, m_prev, gn, dqk):
    L = q.shape[0]
    k = k * (dqk ** -0.5)
    i_col, f_col = gc[:, 0:1], _log_sigmoid(gc[:, 1:2])
    i_row, f_row = gr[0:1, :], _log_sigmoid(gr[1:2, :])
    r = lax.broadcasted_iota(i32, (L, L), 0)
    s = lax.broadcasted_iota(i32, (L, L), 1)
    causal = r >= s
    b_col = jnp.sum(jnp.where(causal, f_row, 0.0), axis=1, keepdims=True)
    b_row = jnp.sum(jnp.where(r <= s, f_col, 0.0), axis=0, keepdims=True)
    g = b_col + m_prev
    dlog = jnp.where(causal, b_col - b_row + i_row, NEG_INF)
    m_t = jnp.maximum(g, jnp.max(dlog, axis=1, keepdims=True))
    w = jnp.exp(dlog - m_t)
    gw = jnp.exp(g - m_t)
    qb_, kb_, vb_ = q.astype(bf16), k.astype(bf16), v.astype(bf16)
    qk = _dot_nt(qb_, kb_) * w
    num = gw * _dot(qb_, C.astype(bf16)) + _dot(qk.astype(bf16), vb_)
    den = gw * jnp.sum(q * n, axis=1, keepdims=True) + jnp.sum(qk, axis=1, keepdims=True)
    hh = num / jnp.maximum(jnp.abs(den), jnp.exp(-m_t))
    hn = hh * lax.rsqrt(jnp.mean(hh * hh, axis=1, keepdims=True) + RMS_EPS) * gn
    h_out = hn * _sigmoid(ob.astype(f32))

    b_last = b_col[L - 1:L, :]
    ws = b_last - b_col + i_col
    m_new = jnp.maximum(b_last + m_prev, jnp.max(ws, axis=0, keepdims=True))
    sw = jnp.exp(ws - m_new)
    cw = jnp.exp(b_last + m_prev - m_new)
    ksw = k * sw
    C_new = cw * C + _dot_tn(ksw.astype(bf16), vb_)
    n_new = cw * n + jnp.sum(ksw, axis=0, keepdims=True)
    return h_out, C_new, n_new, m_new


def _mlstm_body(q_ref, k_ref, *refs, dqk, nv):
    v_refs = refs[:nv]
    (ob_ref, gc_ref, gr_ref, c0_ref, n0_ref, m0_ref, gn_ref,
     h_ref, c_ref, n_ref, mo_ref, cs_ref, ns_ref, ms_ref) = refs[nv:]
    c = pl.program_id(2)
    nc = pl.num_programs(2)

    @pl.when(c == 0)
    def _():
        cs_ref[...] = c0_ref[0, 0]
        ns_ref[...] = n0_ref[0, 0]
        ms_ref[...] = m0_ref[0, 0]

    v = jnp.concatenate([r[...] for r in v_refs], axis=1)
    h_out, C_new, n_new, m_new = _mlstm_chunk(q_ref[...], k_ref[...], v, ob_ref[...], gc_ref[0, 0], gr_ref[0, 0],
                                              cs_ref[...], ns_ref[...], ms_ref[...], gn_ref[0], dqk)
    h_ref[...] = h_out.astype(h_ref.dtype)
    cs_ref[...] = C_new
    ns_ref[...] = n_new
    ms_ref[...] = m_new

    @pl.when(c == nc - 1)
    def _():
        c_ref[0, 0] = cs_ref[...]
        n_ref[0, 0] = ns_ref[...]
        mo_ref[0, 0] = ms_ref[...]


def _mlstm_step_body(q_ref, k_ref, v_ref, ob_ref, gc_ref, gr_ref, c0_ref, n0_ref, m0_ref, gn_ref,
                     h_ref, c_ref, n_ref, mo_ref, *, dqk, dv, nv, hb):
    for h in range(hb):
        v = jnp.concatenate([v_ref[h * nv + j] for j in range(nv)], axis=1)
        h_out, C_new, n_new, m_new = _mlstm_chunk(
            q_ref[h], k_ref[h], v, ob_ref[:, h * dv:(h + 1) * dv], gc_ref[h, 0], gr_ref[h, 0],
            c0_ref[0, h], n0_ref[0, h], m0_ref[0, h], gn_ref[h], dqk)
        h_ref[:, h * dv:(h + 1) * dv] = h_out.astype(h_ref.dtype)
        c_ref[0, h] = C_new
        n_ref[0, h] = n_new
        mo_ref[0, h] = m_new


def _mlstm_step(z, ob, gcol, grow, c0, n0, m0, gnorm, NS, L, H, dqk, dv):
    nv = dv // dqk
    hb = H
    assert (2 * H) % (nv * hb) == 0
    st = lambda b: (b, 0, 0, 0)
    return pl.pallas_call(
        functools.partial(_mlstm_step_body, dqk=dqk, dv=dv, nv=nv, hb=hb), grid=(NS,),
        in_specs=[pl.BlockSpec((hb, L, dqk), lambda b: (0, b, 0)),
                  pl.BlockSpec((hb, L, dqk), lambda b: (H // hb, b, 0)),
                  pl.BlockSpec((nv * hb, L, dqk), lambda b: ((2 * H) // (nv * hb), b, 0)),
                  pl.BlockSpec((L, hb * dv), lambda b: (b, 0)),
                  pl.BlockSpec((hb, 1, L, 2), lambda b: (0, b, 0, 0)),
                  pl.BlockSpec((hb, 1, 2, L), lambda b: (0, b, 0, 0)),
                  pl.BlockSpec((1, hb, dqk, dv), st),
                  pl.BlockSpec((1, hb, 1, dqk), st),
                  pl.BlockSpec((1, hb, 1, 1), st),
                  pl.BlockSpec((hb, 1, dv), lambda b: (0, 0, 0))],
        out_specs=[pl.BlockSpec((L, hb * dv), lambda b: (b, 0)),
                   pl.BlockSpec((1, hb, dqk, dv), st),
                   pl.BlockSpec((1, hb, 1, dqk), st),
                   pl.BlockSpec((1, hb, 1, 1), st)],
        out_shape=[jax.ShapeDtypeStruct((NS * L, H * dv), bf16),
                   jax.ShapeDtypeStruct((NS, H, dqk, dv), f32),
                   jax.ShapeDtypeStruct((NS, H, 1, dqk), f32),
                   jax.ShapeDtypeStruct((NS, H, 1, 1), f32)],
        compiler_params=_cp(("parallel",), 40), name="mlstm_step")(
            z, z, z, ob, gcol, grow, c0, n0, m0, gnorm)


def _mlstm(z, ob, gcol, grow, c0, n0, m0, gnorm, NS, NC, L, H, dqk, dv):
    R = NS * NC * L
    nv = dv // dqk
    st = lambda b, h, c: (b, h, 0, 0)
    slab = lambda f: pl.BlockSpec((None, L, dqk), lambda b, h, c: (f(h), b * NC + c, 0))
    return pl.pallas_call(
        functools.partial(_mlstm_body, dqk=dqk, nv=nv), grid=(NS, H, NC),
        in_specs=[slab(lambda h: h), slab(lambda h: H + h)]
                 + [slab(lambda h, j=j: 2 * H + nv * h + j) for j in range(nv)] + [
                  pl.BlockSpec((L, dv), lambda b, h, c: (b * NC + c, h)),
                  pl.BlockSpec((1, 1, L, 2), lambda b, h, c: (h, b * NC + c, 0, 0)),
                  pl.BlockSpec((1, 1, 2, L), lambda b, h, c: (h, b * NC + c, 0, 0)),
                  pl.BlockSpec((1, 1, dqk, dv), st),
                  pl.BlockSpec((1, 1, 1, dqk), st),
                  pl.BlockSpec((1, 1, 1, 1), st),
                  pl.BlockSpec((1, 1, dv), lambda b, h, c: (h, 0, 0))],
        out_specs=[pl.BlockSpec((L, dv), lambda b, h, c: (b * NC + c, h)),
                   pl.BlockSpec((1, 1, dqk, dv), st),
                   pl.BlockSpec((1, 1, 1, dqk), st),
                   pl.BlockSpec((1, 1, 1, 1), st)],
        out_shape=[jax.ShapeDtypeStruct((R, H * dv), bf16),
                   jax.ShapeDtypeStruct((NS, H, dqk, dv), f32),
                   jax.ShapeDtypeStruct((NS, H, 1, dqk), f32),
                   jax.ShapeDtypeStruct((NS, H, 1, 1), f32)],
        scratch_shapes=[pltpu.VMEM((dqk, dv), f32), pltpu.VMEM((1, dqk), f32), pltpu.VMEM((1, 1), f32)],
        compiler_params=_cp(("parallel", "parallel", "arbitrary"), 32), name="mlstm")(
            *([z] * (2 + nv)), ob, gcol, grow, c0, n0, m0, gnorm)


def _router_body(x_ref, g_ref, w_ref, b_ref, xn_ref, e_ref, wt_ref, *, n_groups, epg):
    x = x_ref[...]
    xn = x * lax.rsqrt(jnp.mean(x * x, axis=-1, keepdims=True) + RMS_EPS) * g_ref[...]
    xb = xn.astype(bf16)
    half = xn.shape[1] // 2
    lo = pltpu.bitcast(xb[:, :half].astype(f32), jnp.uint32) >> 16
    hi = pltpu.bitcast(xb[:, half:].astype(f32), jnp.uint32) & jnp.uint32(0xFFFF0000)
    xn_ref[...] = (lo | hi).reshape(xn_ref.shape)
    logits = _dot(xb, w_ref[...]) + b_ref[...]
    R, W = logits.shape
    lane = lax.broadcasted_iota(i32, (R, W), 1)
    big = jnp.int32(W)

    def first_argmax(vals, mask):
        mx = jnp.max(jnp.where(mask, vals, -jnp.inf), axis=-1, keepdims=True)
        idx = jnp.min(jnp.where(mask & (vals == mx), lane, big), axis=-1, keepdims=True)
        return mx, idx

    gmask = lane < n_groups
    gprob = _masked_softmax_plain(logits, gmask)
    g_w, grp = first_argmax(gprob, gmask)
    lo = n_groups + grp * epg
    emask = (lane >= lo) & (lane < lo + epg)
    eprob = _masked_softmax_plain(logits, emask)
    p1, i1 = first_argmax(eprob, emask)
    p2, i2 = first_argmax(eprob, emask & (lane != i1))
    tot = p1 + p2
    e_ref[...] = jnp.where(lane == 0, i1 - n_groups, jnp.where(lane == 1, i2 - n_groups, 0))
    wt_ref[...] = jnp.where(lane == 0, g_w * p1 / tot, jnp.where(lane == 1, g_w * p2 / tot, 0.0))


def _masked_softmax_plain(x, mask):
    mx = jnp.max(jnp.where(mask, x, -jnp.inf), axis=-1, keepdims=True)
    e = jnp.where(mask, jnp.exp(x - mx), 0.0)
    return e / jnp.sum(e, axis=-1, keepdims=True)


def _router(x, g, w_r, b_r, n_groups, epg):
    M, D = x.shape
    tm = _pick(M, (256, 128, 8))
    row = lambda i: (i, 0)
    return pl.pallas_call(
        functools.partial(_router_body, n_groups=n_groups, epg=epg), grid=(M // tm,),
        in_specs=[pl.BlockSpec((tm, D), row), pl.BlockSpec((1, D), lambda i: (0, 0)),
                  pl.BlockSpec((D, 128), lambda i: (0, 0)), pl.BlockSpec((1, 128), lambda i: (0, 0))],
        out_specs=[pl.BlockSpec((tm, 1, D // 2), lambda i: (i, 0, 0)), pl.BlockSpec((tm, 128), row),
                   pl.BlockSpec((tm, 128), row)],
        out_shape=[jax.ShapeDtypeStruct((M, 1, D // 2), jnp.uint32), jax.ShapeDtypeStruct((M, 128), i32),
                   jax.ShapeDtypeStruct((M, 128), f32)],
        compiler_params=_cp(("parallel",), 40), name="router")(x, g.reshape(1, D).astype(f32), w_r, b_r)


def _gather_rows_body(idx_ref, nu_ref, x_hbm, o_ref, buf_ref, x2d_ref, sem, *, rb):
    b = pl.program_id(0)
    nu = nu_ref[0]

    def start(blk):
        slot = blk % 2

        def issue(r2, carry):
            for p in range(2):
                r = 2 * r2 + p
                pltpu.make_async_copy(x_hbm.at[idx_ref[blk * rb + r]], buf_ref.at[slot, r],
                                      sem.at[slot]).start(priority=p)
            return carry

        lax.fori_loop(0, rb // 2, issue, 0)

    @pl.when((b == 0) & (nu > 0))
    def _():
        start(b)

    @pl.when(b + 1 < nu)
    def _():
        start(b + 1)

    @pl.when(b < nu)
    def _():
        slot = b % 2
        pltpu.make_async_copy(buf_ref.at[slot], buf_ref.at[slot], sem.at[slot]).wait()
        x2d_ref[...] = buf_ref[slot].reshape(x2d_ref.shape)
        u = x2d_ref[...]
        half = u.shape[1]
        o_ref[:, :half] = pltpu.bitcast(u << 16, f32).astype(o_ref.dtype)
        o_ref[:, half:] = pltpu.bitcast(u & jnp.uint32(0xFFFF0000), f32).astype(o_ref.dtype)

    @pl.when(b >= nu)
    def _():
        o_ref[...] = jnp.zeros(o_ref.shape, o_ref.dtype)


def _gather_rows(x, idx, n_used, rb, name):
    R = idx.shape[0]
    Dh = x.shape[2]
    D = 2 * Dh
    gs = pltpu.PrefetchScalarGridSpec(
        num_scalar_prefetch=2, grid=(R // rb,),
        in_specs=[pl.BlockSpec(memory_space=pl.ANY)],
        out_specs=pl.BlockSpec((rb, D), lambda b, idx, nu: (b, 0)),
        scratch_shapes=[pltpu.VMEM((2, rb, 1, Dh), x.dtype), pltpu.VMEM((rb, Dh), x.dtype),
                        pltpu.SemaphoreType.DMA((2,))])
    return pl.pallas_call(
        functools.partial(_gather_rows_body, rb=rb), grid_spec=gs,
        out_shape=jax.ShapeDtypeStruct((R, D), bf16),
        compiler_params=_cp(("arbitrary",), 32), name=name)(idx, n_used, x)


def _moe_up_body(be_ref, nu_ref, x_ref, wg_ref, wu_ref, h_ref, wgb_ref, wub_ref):
    b = pl.program_id(1)
    changed = (b == 0) | (be_ref[b] != be_ref[jnp.maximum(b - 1, 0)])

    @pl.when(changed)
    def _():
        wgb_ref[...] = wg_ref[0].astype(bf16)
        wub_ref[...] = wu_ref[0].astype(bf16)

    @pl.when(b < nu_ref[0])
    def _():
        x = x_ref[...]
        a = _dot(x, wgb_ref[...])
        u = _dot(x, wub_ref[...])
        h_ref[...] = (a * _sigmoid(a) * u).astype(h_ref.dtype)

    @pl.when(b >= nu_ref[0])
    def _():
        h_ref[...] = jnp.zeros(h_ref.shape, h_ref.dtype)


def _moe_up(blk_expert, n_used, xs, w_g, w_u):
    R, D = xs.shape
    F = w_g.shape[2]
    tf = _pick(F, (512, 256, 128))
    nblk = R // MOE_ROWS
    gs = pltpu.PrefetchScalarGridSpec(
        num_scalar_prefetch=2, grid=(F // tf, nblk),
        in_specs=[pl.BlockSpec((MOE_ROWS, D), lambda f, b, be, nu: (jnp.minimum(b, nu[0] - 1), 0)),
                  pl.BlockSpec((1, D, tf), lambda f, b, be, nu: (be[b], 0, f)),
                  pl.BlockSpec((1, D, tf), lambda f, b, be, nu: (be[b], 0, f))],
        out_specs=pl.BlockSpec((MOE_ROWS, tf), lambda f, b, be, nu: (b, f)),
        scratch_shapes=[pltpu.VMEM((D, tf), bf16), pltpu.VMEM((D, tf), bf16)])
    return pl.pallas_call(
        _moe_up_body, grid_spec=gs, out_shape=jax.ShapeDtypeStruct((R, F), bf16),
        compiler_params=_cp(("arbitrary", "arbitrary"), 52), name="moe_up")(blk_expert, n_used, xs, w_g, w_u)


def _moe_down_body(be_ref, nu_ref, h_ref, wd_ref, rw_ref, y_ref, wdb_ref):
    b = pl.program_id(1)
    changed = (b == 0) | (be_ref[b] != be_ref[jnp.maximum(b - 1, 0)])

    @pl.when(changed)
    def _():
        wdb_ref[...] = wd_ref[0].astype(bf16)

    @pl.when(b < nu_ref[0])
    def _():
        y_ref[...] = (_dot(h_ref[...], wdb_ref[...]) * rw_ref[...]).reshape(y_ref.shape)

    @pl.when(b >= nu_ref[0])
    def _():
        y_ref[...] = jnp.zeros(y_ref.shape, y_ref.dtype)


def _moe_down(blk_expert, n_used, h, w_d, row_w):
    R, F = h.shape
    D = w_d.shape[2]
    td = _pick(D, (2048, 1024, 512, 256, 128))
    nblk = R // MOE_ROWS
    gs = pltpu.PrefetchScalarGridSpec(
        num_scalar_prefetch=2, grid=(D // td, nblk),
        in_specs=[pl.BlockSpec((MOE_ROWS, F), lambda d, b, be, nu: (jnp.minimum(b, nu[0] - 1), 0)),
                  pl.BlockSpec((1, F, td), lambda d, b, be, nu: (be[b], 0, d)),
                  pl.BlockSpec((MOE_ROWS, 1), lambda d, b, be, nu: (b, 0))],
        out_specs=pl.BlockSpec((MOE_ROWS, 1, td), lambda d, b, be, nu: (b, 0, d)),
        scratch_shapes=[pltpu.VMEM((F, td), bf16)])
    return pl.pallas_call(
        _moe_down_body, grid_spec=gs, out_shape=jax.ShapeDtypeStruct((R, 1, D), f32),
        compiler_params=_cp(("arbitrary", "arbitrary"), 40), name="moe_down")(blk_expert, n_used, h, w_d, row_w)


def _final_body(dest_ref, x_ref, y_hbm, g_ref, o_ref, ybuf_ref, y2d_ref, sem, *, tm, topk):
    i = pl.program_id(0)

    def start(step):
        slot = step % 2

        def issue(t, carry):
            for k in range(topk):
                pltpu.make_async_copy(y_hbm.at[dest_ref[(step * tm + t) * topk + k]],
                                      ybuf_ref.at[slot, k * tm + t], sem.at[slot]).start(priority=k % 2)
            return carry

        lax.fori_loop(0, tm, issue, 0)

    @pl.when(i == 0)
    def _():
        start(i)

    @pl.when(i + 1 < pl.num_programs(0))
    def _():
        start(i + 1)

    slot = i % 2
    pltpu.make_async_copy(ybuf_ref.at[slot], ybuf_ref.at[slot], sem.at[slot]).wait()
    y2d_ref[...] = ybuf_ref[slot].reshape(y2d_ref.shape)
    x = x_ref[...]
    for k in range(topk):
        x = x + y2d_ref[k * tm:(k + 1) * tm, :]
    y = x * lax.rsqrt(jnp.mean(x * x, axis=-1, keepdims=True) + RMS_EPS)
    o_ref[...] = y * g_ref[...]


def _final(x, y_rows, dest, g):
    M, D = x.shape
    tm = _pick(M, (128, 8))
    topk = dest.shape[0] // M
    gs = pltpu.PrefetchScalarGridSpec(
        num_scalar_prefetch=1, grid=(M // tm,),
        in_specs=[pl.BlockSpec((tm, D), lambda i, d: (i, 0)), pl.BlockSpec(memory_space=pl.ANY),
                  pl.BlockSpec((1, D), lambda i, d: (0, 0))],
        out_specs=pl.BlockSpec((tm, D), lambda i, d: (i, 0)),
        scratch_shapes=[pltpu.VMEM((2, topk * tm, 1, D), f32), pltpu.VMEM((topk * tm, D), f32),
                        pltpu.SemaphoreType.DMA((2,))])
    return pl.pallas_call(
        functools.partial(_final_body, tm=tm, topk=topk), grid_spec=gs,
        out_shape=jax.ShapeDtypeStruct((M, D), f32),
        compiler_params=_cp(("arbitrary",), 40), name="final_norm")(dest, x, y_rows, g.reshape(1, D).astype(f32))


def _moe(x2, g_ffn, w_rg, b_rg, w_re, b_re, w_eg, w_eu, w_ed):
    M, D = x2.shape
    n_groups = w_rg.shape[1]
    E = w_re.shape[1]
    epg = E // n_groups
    w_r = jnp.zeros((D, 128), f32).at[:, :n_groups].set(w_rg).at[:, n_groups:n_groups + E].set(w_re).astype(bf16)
    b_r = jnp.zeros((1, 128), f32).at[0, :n_groups].set(b_rg).at[0, n_groups:n_groups + E].set(b_re)
    xn, e_out, w_out = _router(x2, g_ffn, w_r, b_r, n_groups, epg)

    A = M * TOP_K_WITHIN
    e_flat = e_out[:, :TOP_K_WITHIN].reshape(A)
    w_flat = w_out[:, :TOP_K_WITHIN].reshape(A)
    order = jnp.argsort(e_flat)
    e_sorted = e_flat[order]
    counts = jnp.bincount(e_flat, length=E).astype(i32)
    padded = (counts + MOE_ROWS - 1) // MOE_ROWS * MOE_ROWS
    pad_end = jnp.cumsum(padded)
    pad_start = pad_end - padded
    cnt_start = jnp.cumsum(counts) - counts
    dest_sorted = (pad_start[e_sorted] + jnp.arange(A, dtype=i32) - cnt_start[e_sorted]).astype(i32)
    nblk = -(-(A + E * (MOE_ROWS - 1)) // MOE_ROWS)
    R = nblk * MOE_ROWS
    row_tok = jnp.zeros((R,), i32).at[dest_sorted].set((order // TOP_K_WITHIN).astype(i32))
    row_w = jnp.zeros((R,), f32).at[dest_sorted].set(w_flat[order])
    dest = jnp.zeros((A,), i32).at[order].set(dest_sorted)
    n_used = (pad_end[-1] // MOE_ROWS).astype(i32)
    blk = jnp.arange(nblk, dtype=i32)
    blk_expert = jnp.minimum(jnp.searchsorted(pad_end, blk * MOE_ROWS, side="right"), E - 1).astype(i32)
    blk_expert = jnp.where(blk < n_used, blk_expert, blk_expert[jnp.maximum(n_used - 1, 0)])
    n_used1 = n_used.reshape(1)

    xs = _gather_rows(xn, row_tok, n_used1, MOE_ROWS, "moe_gather")
    h = _moe_up(blk_expert, n_used1, xs, w_eg, w_eu)
    y_rows = _moe_down(blk_expert, n_used1, h, w_ed, row_w.reshape(R, 1))
    return y_rows, dest


def _alibi_slopes(G, hg):
    h = np.arange(1, G * hg + 1, dtype=np.float32)
    return (2.0 ** (-8.0 * h / (G * hg))).astype(np.float32).reshape(G, hg)


def _overlap(n_cmp, n_slc):
    cs = np.arange(n_cmp)[:, None] * CMP_STRIDE
    ss = np.arange(n_slc)[None, :] * SLC_BLOCK
    ov = np.clip(np.minimum(cs + CMP_BLOCK, ss + SLC_BLOCK) - np.maximum(cs, ss), 0, None)
    return ov.astype(np.float32) / np.float32(CMP_BLOCK)


def _pad2(a, rows, cols):
    out = np.zeros((rows, cols), np.float32)
    out[:a.shape[0], :a.shape[1]] = a
    return out


def kernel(x_prompt, x_sample, cache_cmp_kv, cache_slc_kv, cache_win_kv, state_mlstm_C, state_mlstm_n,
           state_mlstm_m, page_table, g_norm_mix, w_in, b_in, cmp_pos, w_cmp1, b_cmp1, w_cmp2, g_mlstm_norm,
           w_branch_a, w_branch_b, w_out, g_norm_ffn, w_router_group, b_router_group, w_router_expert,
           b_router_expert, w_exp_gate, w_exp_up, w_exp_down, g_norm_final):
    B, T, D = x_prompt.shape
    DB, TS, _ = x_sample.shape
    depth = w_in.shape[0]
    assert depth == 1, "single layer only"
    G, hd = cache_cmp_kv.shape[4], cache_cmp_kv.shape[5]
    assert hd == HEAD_DIM and cache_cmp_kv.shape[2] == PAGE_SIZE and SLC_BLOCK == 64
    NH = w_branch_a.shape[1] // HEAD_DIM
    hg = NH // G
    H, dqk, dv = state_mlstm_C.shape[2:]
    NP = page_table.shape[1]
    past = NP * PAGE_SIZE
    win_buf = cache_win_kv.shape[2]
    F_cmp = w_cmp1.shape[-1]
    tp, sp = SAMPLE_TOK_PAD, SAMPLE_PAD
    assert T % Q_BLOCK == 0 and T >= win_buf and TS <= tp and TS & (TS - 1) == 0
    assert 2 * G == 8 and (hg * tp) & (hg * tp - 1) == 0, "sample kernels: one token's k/v rows fill one sublane tile"
    assert TS <= SAMPLE_NEW_ROWS and past % SLC_BLOCK == 0
    n_cmp_s = (past + TS - CMP_BLOCK) // CMP_STRIDE + 1
    assert n_cmp_s == past // CMP_STRIDE - 1, "sample compression must not reach the new tokens"
    nslc_s = -(-(past + TS) // SLC_BLOCK)
    assert nslc_s == past // SLC_BLOCK + 1 and nslc_s <= 128 and -(-T // SLC_BLOCK) <= 128

    n_prompt, n_sample = B * T, DB * TS
    M0 = n_prompt + n_sample
    M = -(-M0 // ROW_ALIGN) * ROW_ALIGN
    xa = jnp.concatenate([x_prompt.reshape(n_prompt, D), x_sample.reshape(n_sample, D),
                          jnp.zeros((M - M0, D), f32)], axis=0)

    sizes = (NH * HEAD_DIM, 6 * G * HEAD_DIM, 3 * NH, H * dqk, H * dqk, H * dv, 2 * H, H * dv, 2 * D)
    offs = np.concatenate([[0], np.cumsum(sizes)])
    seg = lambda a, i, j=None: a[..., int(offs[i]):int(offs[(i if j is None else j) + 1])]
    w0, b0 = w_in[0], b_in[0]
    ngate = sizes[2] + sizes[6]
    assert ngate <= 128
    w_g = jnp.concatenate([seg(w0, 2), seg(w0, 6), jnp.zeros((D, 128 - ngate), f32)], axis=1).astype(bf16)
    b_g = jnp.concatenate([seg(b0, 2), seg(b0, 6), jnp.zeros((128 - ngate,), f32)])
    xn = _rmsnorm(xa, g_norm_mix[0], bf16)
    zA = _mm_bias(xn, seg(w0, 0, 1).astype(bf16), seg(b0, 0, 1), f32, "in_proj_nsa")
    assert dv % dqk == 0
    zB = _mm_bias_slabs(xn, seg(w0, 3, 5).astype(bf16), seg(b0, 3, 5), dqk, "in_proj_mlstm")
    zC = _mm_bias(xn, seg(w0, 7, 8).astype(bf16), seg(b0, 7, 8), bf16, "in_proj_gates")
    zG = _mm_bias(xn, w_g, b_g, f32, "in_proj_small")
    qblocks = NH

    w1c = (w_cmp1[0].reshape(2, CMP_STRIDE, 2, HEAD_DIM, F_cmp).transpose(2, 1, 3, 0, 4)
           .reshape(2, CMP_STRIDE * HEAD_DIM, 2 * F_cmp).astype(bf16))
    pos_bias = (b_cmp1[0] + jnp.einsum("lcd,lcdf->cf", cmp_pos[0], w_cmp1[0],
                                       precision=lax.Precision.HIGHEST)).reshape(2, 1, F_cmp)
    w2c = w_cmp2[0].astype(bf16)
    kvc = _compress_prompt(zA, w1c, pos_bias, w2c, B, T, G, qblocks)
    ga_all = zG[:, :3 * NH].reshape(M, 3, G, hg)
    gates_p = jnp.pad(ga_all.transpose(2, 0, 1, 3).reshape(G, M, 3 * hg), ((0, 0), (0, 0), (0, 128 - 3 * hg)))
    slopes = _alibi_slopes(G, hg)
    nch_p, nslc_p = T // CMP_STRIDE, -(-T // SLC_BLOCK)
    ov_t = jnp.asarray(_pad2(_overlap(nch_p - 1, nslc_p).T, -(-nslc_p // 8) * 8, nch_p), bf16)
    h_a_p = _nsa_prompt(zA, kvc, gates_p, jnp.asarray(slopes.reshape(-1)), ov_t, B, T, G, hg, qblocks)

    R_s = G * hg * tp
    zA_s = zA[n_prompt:M0]
    q_s = zA_s[:, :NH * HEAD_DIM].reshape(DB, TS, G, hg, HEAD_DIM).transpose(0, 2, 3, 1, 4)
    q_s = jnp.pad(q_s, ((0, 0), (0, 0), (0, 0), (0, tp - TS), (0, 0))).reshape(DB, R_s, HEAD_DIM)
    kvn = jnp.pad(zA_s[:, NH * HEAD_DIM:].reshape(DB, TS, 6 * G * HEAD_DIM),
                  ((0, 0), (0, SAMPLE_NEW_ROWS - TS), (0, 0)))
    gates_s = ga_all[n_prompt:M0].reshape(DB, TS, 3, G, hg).transpose(0, 3, 4, 1, 2)
    gates_s = jnp.pad(gates_s, ((0, 0), (0, 0), (0, 0), (0, tp - TS), (0, 125))).reshape(DB, R_s, 128)
    slope_l = jnp.asarray(np.repeat(slopes.reshape(-1), tp)[None, :])
    lane_g, lane_t = np.arange(R_s) // (hg * tp), np.arange(R_s) % tp
    rsum = jnp.asarray((lane_g[:, None] == lane_g[None, :]) & (lane_t[:, None] == lane_t[None, :]), bf16)
    nch_s = past // CMP_STRIDE
    ovr = jnp.asarray(np.repeat(_pad2(_overlap(n_cmp_s, nslc_s), nch_s, -(-nslc_s // 8) * 8), G, axis=0), bf16)
    pt_flat = page_table.reshape(-1).astype(i32)
    as_rows = lambda c: c.reshape(-1, HEAD_DIM)
    o_cmp_s, sel_s = _nsa_sample_cmp(pt_flat, as_rows(cache_cmp_kv), w1c, pos_bias, w2c, q_s, slope_l, ovr, rsum,
                                     DB, NP, G, hg, past, nslc_s)
    assert TS <= win_buf
    h_a_s, win_s = _nsa_sample_attn(pt_flat, as_rows(cache_slc_kv), as_rows(cache_win_kv), q_s, sel_s, kvn, gates_s,
                                    slope_l, o_cmp_s, DB, NP, G, hg, past, TS)
    h_a_s = (h_a_s.reshape(DB, G, hg, tp, HEAD_DIM)[:, :, :, :TS].transpose(0, 3, 1, 2, 4)
             .reshape(n_sample, NH * HEAD_DIM).astype(bf16))
    h_a = jnp.concatenate([h_a_p, h_a_s, jnp.zeros((M - M0, NH * HEAD_DIM), bf16)], axis=0)

    if_all = zG[:, 3 * NH:3 * NH + 2 * H].reshape(M, 2, H)
    gnorm = g_mlstm_norm[0].reshape(H, 1, dv).astype(f32)
    L = math.gcd(T, MLSTM_CHUNK)
    NC = T // L
    if_p = if_all[:n_prompt].reshape(B * NC, L, 2, H)
    h_b_p, C_p, n_p, m_p = _mlstm(
        zB, zC, if_p.transpose(3, 0, 1, 2), if_p.transpose(3, 0, 2, 1),
        jnp.zeros((B, H, dqk, dv), f32), jnp.zeros((B, H, 1, dqk), f32), jnp.zeros((B, H, 1, 1), f32),
        gnorm, B, NC, L, H, dqk, dv)
    padrows = lambda a: jnp.pad(a.reshape(DB, TS, -1), ((0, 0), (0, sp - TS), (0, 0))).reshape(DB * sp, -1)
    zB_s = jnp.pad(zB[:, n_prompt:M0].reshape(-1, DB, TS, dqk), ((0, 0), (0, 0), (0, sp - TS), (0, 0)))
    if_s = if_all[n_prompt:M0].reshape(DB, TS, 2, H)
    pad_gate = jnp.broadcast_to(jnp.asarray([NEG_INF, 1e4], f32)[None, None, :, None], (DB, sp - TS, 2, H))
    if_s = jnp.concatenate([if_s, pad_gate], axis=1)
    h_b_s, C_s, n_s, m_s = _mlstm_step(
        zB_s.reshape(-1, DB * sp, dqk), padrows(zC[n_prompt:M0, :H * dv]),
        if_s.transpose(3, 0, 1, 2), if_s.transpose(3, 0, 2, 1),
        state_mlstm_C[0], state_mlstm_n[0].reshape(DB, H, 1, dqk), state_mlstm_m[0].reshape(DB, H, 1, 1),
        gnorm, DB, sp, H, dqk, dv)
    h_b = jnp.concatenate([h_b_p, h_b_s.reshape(DB, sp, H * dv)[:, :TS].reshape(n_sample, H * dv),
                           jnp.zeros((M - M0, H * dv), bf16)], axis=0)

    mix = _merge(h_a, w_branch_a[0].astype(bf16), h_b, w_branch_b[0].astype(bf16), zC, H * dv)
    x2 = _mm_residual(mix, w_out[0].astype(bf16), xa)
    y_rows, dest = _moe(x2, g_norm_ffn[0], w_router_group[0], b_router_group[0], w_router_expert[0],
                        b_router_expert[0], w_exp_gate[0], w_exp_up[0], w_exp_down[0])
    y = _final(x2, y_rows, dest, g_norm_final)

    kv_shape = (2, G, HEAD_DIM)
    kv_p = zA[:n_prompt, NH * HEAD_DIM:].reshape(B, T, 3, *kv_shape)
    kv_s = zA_s[:, NH * HEAD_DIM:].reshape(DB, TS, 3, *kv_shape)
    return (y[:n_prompt].reshape(B, T, D), y[n_prompt:M0].reshape(DB, TS, D),
            kv_p[:, :, 0][None], kv_s[:, :, 0][None], kv_p[:, :, 1][None], kv_s[:, :, 1][None],
            kv_p[:, -win_buf:, 2][None], win_s.reshape(cache_win_kv.shape),
            C_p[None], C_s[None], n_p.reshape(1, B, H, dqk), n_s.reshape(1, DB, H, dqk),
            m_p.reshape(1, B, H), m_s.reshape(1, DB, H))
```
